```python
import math
import jax
import jax.numpy as jnp
from jax import lax
import numpy as np

D_MODEL = 4096
BATCH = 2
SEQ = 8192
DEPTH = 2

BLOCK = 128
MEM_LEN = 256
EPS = 1e-6

A_HEADS = 16
A_Q_RANK = 1024
A_KV_RANK = 512
A_V_DIM = 128
IDX_HEADS = 32
IDX_DIM = 128
TOPK_MAX = 256
B_HEADS = 16
B_DIM = 128
C_HEADS = 32
C_KV_HEADS = 4
C_DIM = 64
WINDOW = 128
D_HEADS = 8
D_QK_DIM = 128
D_V_DIM = 256
X_HEADS = 4
X_DIM = 128
D_FF = 11008
CONV_W = 3
NUM_BUCKETS = 32
MAX_EXACT = 16
MAX_DISTANCE = 128
BIAS_A_OFF = 0
BIAS_C_OFF = A_HEADS
BIAS_D_OFF = A_HEADS + C_HEADS
BIAS_COLS = A_HEADS + C_HEADS + D_HEADS

EVEN_SIZES = (A_Q_RANK, A_KV_RANK, IDX_DIM, IDX_HEADS, B_HEADS * B_DIM, B_HEADS * B_DIM, B_HEADS * B_DIM)
ODD_SIZES = (C_HEADS * C_DIM, C_KV_HEADS * C_DIM, C_KV_HEADS * C_DIM, D_HEADS * 2 * D_QK_DIM, D_HEADS * 2 * D_QK_DIM, D_HEADS * D_V_DIM)
EVEN_IN = sum(EVEN_SIZES)
ODD_IN = sum(ODD_SIZES)
EVEN_OUT = A_HEADS * A_V_DIM + B_HEADS * B_DIM
ODD_OUT = C_HEADS * C_DIM + D_HEADS * D_V_DIM
N_EVEN = (DEPTH + 1) // 2
N_ODD = DEPTH // 2

kernel_name = 'hybrid_dsa_stickbreak_swa_diff_block'


def rmsnorm(x, g):
    xf = x.astype(jnp.float32)
    y = xf * lax.rsqrt(jnp.mean(xf * xf, axis=-1, keepdims=True) + EPS)
    return (y * g.astype(jnp.float32)).astype(x.dtype)


def t5_bucket(dist):
    n = jnp.maximum(dist, 0)
    nf = jnp.maximum(n, 1).astype(jnp.float32)
    large = MAX_EXACT + (jnp.log(nf / MAX_EXACT) / math.log(MAX_DISTANCE / MAX_EXACT) * (NUM_BUCKETS - MAX_EXACT)).astype(jnp.int32)
    return jnp.where(n < MAX_EXACT, n, jnp.minimum(large, NUM_BUCKETS - 1))


def split_cols(x, sizes):
    offs = []
    acc = 0
    for s in sizes[:-1]:
        acc += s
        offs.append(acc)
    return jnp.split(x, offs, axis=-1)


def dsa_attention(q_lat, kv_lat, k_idx, w_idx, q_lat_g, w_qb, q_g, kv_g, w_qi, kidx_g, w_uv, bias_a):
    bsz, seq, _ = q_lat.shape
    topk = min(TOPK_MAX, seq // 4)
    n_blk = seq // BLOCK
    cq = rmsnorm(q_lat, q_lat_g)
    q = rmsnorm(jnp.einsum('bsr,rhc->bshc', cq, w_qb), q_g)
    qi = jnp.einsum('bsr,rhd->bshd', cq, w_qi)
    c = rmsnorm(kv_lat, kv_g)
    ki = rmsnorm(k_idx, kidx_g)
    wi = w_idx * (IDX_HEADS ** -0.5 * IDX_DIM ** -0.5)
    key_pos = jnp.arange(seq, dtype=jnp.int32)

    def block(i):
        start = i * BLOCK
        t = start + jnp.arange(BLOCK, dtype=jnp.int32)
        qb = lax.dynamic_slice_in_dim(q, start, BLOCK, axis=1)
        qib = lax.dynamic_slice_in_dim(qi, start, BLOCK, axis=1)
        wib = lax.dynamic_slice_in_dim(wi, start, BLOCK, axis=1)
        rel = jax.nn.relu(jnp.einsum('bthd,bsd->bths', qib, ki))
        score = jnp.einsum('bths,bth->bts', rel, wib).astype(jnp.float32)
        score = jnp.where((key_pos[None, :] <= t[:, None])[None], score, -jnp.inf)
        _, sel = lax.top_k(score, topk)
        valid = sel <= t[None, :, None]
        cg = jax.vmap(lambda cb, ib: cb[ib])(c, sel)
        logits = jnp.einsum('bthr,btkr->bhtk', qb, cg).astype(jnp.float32) * A_KV_RANK ** -0.5
        bias = bias_a[t5_bucket(t[None, :, None] - sel)].astype(jnp.float32)
        logits = jnp.where(valid[:, None], logits + jnp.moveaxis(bias, -1, 1), -jnp.inf)
        p = jax.nn.softmax(logits, axis=-1).astype(cg.dtype)
        o_lat = jnp.einsum('bhtk,btkr->bthr', p, cg)
        return jnp.einsum('bthr,hrd->bthd', o_lat, w_uv)

    out = lax.map(block, jnp.arange(n_blk))
    return jnp.swapaxes(out, 0, 1).reshape(bsz, seq, A_HEADS * A_V_DIM)


def stick_breaking_attention(q, k, v):
    bsz, seq, nh, dh = q.shape
    n_blk = seq // BLOCK
    key_pos = jnp.arange(seq, dtype=jnp.int32)

    def block(i):
        start = i * BLOCK
        t = start + jnp.arange(BLOCK, dtype=jnp.int32)
        qb = lax.dynamic_slice_in_dim(q, start, BLOCK, axis=1)
        z = jnp.einsum('bthd,bshd->bhts', qb, k).astype(jnp.float32) * dh ** -0.5
        strict = (key_pos[None, :] < t[:, None])[None, None]
        log1m = jnp.where(strict, jax.nn.log_sigmoid(-z), 0.0)
        between = lax.cumsum(log1m, axis=3, reverse=True) - log1m
        w = jnp.where(strict, jnp.exp(jax.nn.log_sigmoid(z) + between), 0.0).astype(v.dtype)
        return jnp.einsum('bhts,bshd->bthd', w, v)

    out = lax.map(block, jnp.arange(n_blk))
    return jnp.swapaxes(out, 0, 1).reshape(bsz, seq, nh * dh)


def swa_sink_attention(q, k, v, sinks, bias_c):
    bsz, seq, nh, dh = q.shape
    n_blk = seq // BLOCK
    grp = nh // C_KV_HEADS
    qb = q.reshape(bsz, n_blk, BLOCK, C_KV_HEADS, grp, dh)

    def band(a):
        ab = a.reshape(bsz, n_blk, BLOCK, C_KV_HEADS, dh)
        prev = jnp.pad(ab, ((0, 0), (1, 0), (0, 0), (0, 0), (0, 0)))[:, :-1]
        return jnp.concatenate([prev, ab], axis=2)

    kb = band(k)
    vb = band(v)
    logits = jnp.einsum('bnqhgd,bnshd->bnhgqs', qb, kb).astype(jnp.float32) * dh ** -0.5
    qa = jnp.arange(BLOCK, dtype=jnp.int32)
    sa = jnp.arange(2 * BLOCK, dtype=jnp.int32)
    rel = qa[:, None] + BLOCK - sa[None, :]
    in_win = (rel >= 0) & (rel < WINDOW)
    blk_ok = (jnp.arange(n_blk)[:, None] > 0) | (sa[None, :] >= BLOCK)
    mask = in_win[None] & blk_ok[:, None, :]
    bias = jnp.moveaxis(bias_c[t5_bucket(rel)], -1, 0).reshape(C_KV_HEADS, grp, BLOCK, 2 * BLOCK)
    logits = jnp.where(mask[None, :, None, None], logits + bias.astype(jnp.float32), -jnp.inf)
    sink = sinks.astype(jnp.float32).reshape(1, 1, C_KV_HEADS, grp, 1, 1)
    m = jnp.maximum(jnp.max(logits, axis=-1, keepdims=True), sink)
    e = jnp.exp(logits - m)
    p = (e / (jnp.sum(e, axis=-1, keepdims=True) + jnp.exp(sink - m))).astype(v.dtype)
    o = jnp.einsum('bnhgqs,bnshd->bnqhgd', p, vb)
    return o.reshape(bsz, seq, nh * dh)


def differential_attention(q, k, v, lam_q1, lam_k1, lam_q2, lam_k2, sub_g, bias_d, lambda_init):
    bsz, seq, nh, _, dq = q.shape
    n_blk = seq // BLOCK
    key_pos = jnp.arange(seq, dtype=jnp.int32)
    lam = (jnp.exp(jnp.sum(lam_q1 * lam_k1)) - jnp.exp(jnp.sum(lam_q2 * lam_k2))).astype(jnp.float32) + lambda_init

    def block(i):
        start = i * BLOCK
        t = start + jnp.arange(BLOCK, dtype=jnp.int32)
        qb = lax.dynamic_slice_in_dim(q, start, BLOCK, axis=1)
        logits = jnp.einsum('bthcd,bshcd->bchts', qb, k).astype(jnp.float32) * dq ** -0.5
        dist = t[:, None] - key_pos[None, :]
        bias = jnp.moveaxis(bias_d[t5_bucket(dist)], -1, 0).astype(jnp.float32)
        logits = jnp.where((dist >= 0)[None, None, None], logits + bias, -jnp.inf)
        p = jax.nn.softmax(logits, axis=-1)
        w = (p[:, 0] - lam * p[:, 1]).astype(v.dtype)
        return jnp.einsum('bhts,bshd->bthd', w, v)

    out = jnp.swapaxes(lax.map(block, jnp.arange(n_blk)), 0, 1).reshape(bsz, seq, nh, v.shape[-1])
    out = rmsnorm(out, sub_g) * (1.0 - lambda_init)
    return out.reshape(bsz, seq, nh * v.shape[-1])


def even_mixer(h, w_in, q_lat_g, w_qb, q_g, kv_g, w_uv, w_qi, kidx_g, w_out, rel_bias):
    bsz, seq, _ = h.shape
    proj = h @ w_in
    a_qlat, a_kv, a_kidx, a_widx, b_q, b_k, b_v = split_cols(proj, EVEN_SIZES)
    o_a = dsa_attention(a_qlat, a_kv, a_kidx, a_widx, q_lat_g, w_qb, q_g, kv_g, w_qi, kidx_g, w_uv,
                        rel_bias[:, BIAS_A_OFF:BIAS_A_OFF + A_HEADS])
    shp = (bsz, seq, B_HEADS, B_DIM)
    o_b = stick_breaking_attention(b_q.reshape(shp), b_k.reshape(shp), b_v.reshape(shp))
    return jnp.concatenate([o_a, o_b], axis=-1) @ w_out


def odd_mixer(h, w_in, c_q_g, c_k_g, sinks, d_q_g, d_k_g, lam_q1, lam_k1, lam_q2, lam_k2, sub_g, w_out, rel_bias, lambda_init):
    bsz, seq, _ = h.shape
    proj = h @ w_in
    c_q, c_k, c_v, d_q, d_k, d_v = split_cols(proj, ODD_SIZES)
    cq = rmsnorm(c_q.reshape(bsz, seq, C_HEADS, C_DIM), c_q_g)
    ck = rmsnorm(c_k.reshape(bsz, seq, C_KV_HEADS, C_DIM), c_k_g)
    cv = c_v.reshape(bsz, seq, C_KV_HEADS, C_DIM)
    o_c = swa_sink_attention(cq, ck, cv, sinks, rel_bias[:, BIAS_C_OFF:BIAS_C_OFF + C_HEADS])
    dq = rmsnorm(d_q.reshape(bsz, seq, D_HEADS, 2, D_QK_DIM), d_q_g)
    dk = rmsnorm(d_k.reshape(bsz, seq, D_HEADS, 2, D_QK_DIM), d_k_g)
    dv = d_v.reshape(bsz, seq, D_HEADS, D_V_DIM)
    o_d = differential_attention(dq, dk, dv, lam_q1, lam_k1, lam_q2, lam_k2, sub_g,
                                 rel_bias[:, BIAS_D_OFF:BIAS_D_OFF + D_HEADS], lambda_init)
    return jnp.concatenate([o_c, o_d], axis=-1) @ w_out


def cross_attention(h, memn, wq, wk, wv, q_g, k_g, wo):
    bsz, seq, _ = h.shape
    q = rmsnorm(jnp.einsum('bsd,dhc->bshc', h, wq), q_g)
    k = rmsnorm(jnp.einsum('bmd,dhc->bmhc', memn, wk), k_g)
    v = jnp.einsum('bmd,dhc->bmhc', memn, wv)
    logits = jnp.einsum('bshc,bmhc->bhsm', q, k).astype(jnp.float32) * X_DIM ** -0.5
    p = jax.nn.softmax(logits, axis=-1).astype(v.dtype)
    o = jnp.einsum('bhsm,bmhc->bshc', p, v).reshape(bsz, seq, X_HEADS * X_DIM)
    return o @ wo


def conv_ffn(h, w_gate, w_up, conv_w, conv_b, w_down):
    g = h @ w_gate
    g = lax.conv_general_dilated(g, conv_w[:, None, :], window_strides=(1,), padding=[(CONV_W - 1, 0)],
                                 dimension_numbers=('NWC', 'WIO', 'NWC'), feature_group_count=D_FF) + conv_b
    return (jax.nn.silu(g) * (h @ w_up)) @ w_down


def setup_inputs(seed: int = 0) -> dict:
    key = jax.random.key(seed)
    ks = iter(jax.random.split(key, 48))

    def nrm(shape, scale):
        return jax.random.normal(next(ks), shape, jnp.float32) * scale

    def gain(shape):
        return 1.0 + nrm(shape, 0.02)

    d = D_MODEL
    return {
        'x': nrm((BATCH, SEQ, d), 1.0),
        'mem': nrm((BATCH, MEM_LEN, d), 1.0),
        'rel_bias': nrm((NUM_BUCKETS, BIAS_COLS), 0.2),
        'mem_norm_g': gain((d,)),
        'mix_norm_g': gain((DEPTH, d)),
        'xattn_norm_g': gain((DEPTH, d)),
        'ffn_norm_g': gain((DEPTH, d)),
        'ev_w_in': nrm((N_EVEN, d, EVEN_IN), d ** -0.5),
        'ev_q_lat_g': gain((N_EVEN, A_Q_RANK)),
        'ev_w_qb': nrm((N_EVEN, A_Q_RANK, A_HEADS, A_KV_RANK), A_Q_RANK ** -0.5),
        'ev_q_g': gain((N_EVEN, A_KV_RANK)),
        'ev_kv_g': gain((N_EVEN, A_KV_RANK)),
        'ev_w_uv': nrm((N_EVEN, A_HEADS, A_KV_RANK, A_V_DIM), A_KV_RANK ** -0.5),
        'ev_w_qi': nrm((N_EVEN, A_Q_RANK, IDX_HEADS, IDX_DIM), A_Q_RANK ** -0.5),
        'ev_kidx_g': gain((N_EVEN, IDX_DIM)),
        'ev_w_out': nrm((N_EVEN, EVEN_OUT, d), EVEN_OUT ** -0.5),
        'od_w_in': nrm((N_ODD, d, ODD_IN), d ** -0.5),
        'od_c_q_g': gain((N_ODD, C_DIM)),
        'od_c_k_g': gain((N_ODD, C_DIM)),
        'od_sinks': nrm((N_ODD, C_HEADS), 0.5),
        'od_d_q_g': gain((N_ODD, D_QK_DIM)),
        'od_d_k_g': gain((N_ODD, D_QK_DIM)),
        'od_lam_q1': nrm((N_ODD, D_QK_DIM), 0.1),
        'od_lam_k1': nrm((N_ODD, D_QK_DIM), 0.1),
        'od_lam_q2': nrm((N_ODD, D_QK_DIM), 0.1),
        'od_lam_k2': nrm((N_ODD, D_QK_DIM), 0.1),
        'od_sub_g': gain((N_ODD, D_V_DIM)),
        'od_w_out': nrm((N_ODD, ODD_OUT, d), ODD_OUT ** -0.5),
        'x_wq': nrm((DEPTH, d, X_HEADS, X_DIM), d ** -0.5),
        'x_wk': nrm((DEPTH, d, X_HEADS, X_DIM), d ** -0.5),
        'x_wv': nrm((DEPTH, d, X_HEADS, X_DIM), d ** -0.5),
        'x_q_g': gain((DEPTH, X_DIM)),
        'x_k_g': gain((DEPTH, X_DIM)),
        'x_wo': nrm((DEPTH, X_HEADS * X_DIM, d), (X_HEADS * X_DIM) ** -0.5),
        'f_w_gate': nrm((DEPTH, d, D_FF), d ** -0.5),
        'f_w_up': nrm((DEPTH, d, D_FF), d ** -0.5),
        'f_conv_w': nrm((DEPTH, CONV_W, D_FF), CONV_W ** -0.5),
        'f_conv_b': nrm((DEPTH, D_FF), 0.01),
        'f_w_down': nrm((DEPTH, D_FF, d), D_FF ** -0.5),
    }


def reference(x, mem, rel_bias, mem_norm_g, mix_norm_g, xattn_norm_g, ffn_norm_g,
              ev_w_in, ev_q_lat_g, ev_w_qb, ev_q_g, ev_kv_g, ev_w_uv, ev_w_qi, ev_kidx_g, ev_w_out,
              od_w_in, od_c_q_g, od_c_k_g, od_sinks, od_d_q_g, od_d_k_g, od_lam_q1, od_lam_k1, od_lam_q2, od_lam_k2,
              od_sub_g, od_w_out,
              x_wq, x_wk, x_wv, x_q_g, x_k_g, x_wo,
              f_w_gate, f_w_up, f_conv_w, f_conv_b, f_w_down):
    memn = rmsnorm(mem, mem_norm_g)
    for l in range(DEPTH):
        h = rmsnorm(x, mix_norm_g[l])
        if l % 2 == 0:
            e = l // 2
            x = x + even_mixer(h, ev_w_in[e], ev_q_lat_g[e], ev_w_qb[e], ev_q_g[e], ev_kv_g[e], ev_w_uv[e],
                               ev_w_qi[e], ev_kidx_g[e], ev_w_out[e], rel_bias)
        else:
            o = l // 2
            lambda_init = 0.8 - 0.6 * math.exp(-0.3 * l)
            x = x + odd_mixer(h, od_w_in[o], od_c_q_g[o], od_c_k_g[o], od_sinks[o], od_d_q_g[o], od_d_k_g[o],
                              od_lam_q1[o], od_lam_k1[o], od_lam_q2[o], od_lam_k2[o], od_sub_g[o], od_w_out[o],
                              rel_bias, lambda_init)
        x = x + cross_attention(rmsnorm(x, xattn_norm_g[l]), memn, x_wq[l], x_wk[l], x_wv[l], x_q_g[l], x_k_g[l], x_wo[l])
        x = x + conv_ffn(rmsnorm(x, ffn_norm_g[l]), f_w_gate[l], f_w_up[l], f_conv_w[l], f_conv_b[l], f_w_down[l])
    return x
```

```python
import functools
import math

import jax
import jax.numpy as jnp
from jax import lax
from jax.experimental import pallas as pl
from jax.experimental.pallas import tpu as pltpu

F32 = jnp.float32
BF16 = jnp.bfloat16
I32 = jnp.int32

EPS = 1e-6
LANES = 128
MIB = 1024 * 1024
VMEM_CAP = 58 * MIB
M_INIT = -1e30
INT_MIN = -(2 ** 31)

A_HEADS, A_Q_RANK, A_KV_RANK, A_V_DIM = 16, 1024, 512, 128
IDX_HEADS, IDX_DIM, TOPK_MAX = 32, 128, 256
B_HEADS, B_DIM = 16, 128
C_HEADS, C_KV_HEADS, C_DIM, WINDOW = 32, 4, 64, 128
D_HEADS, D_QK_DIM, D_V_DIM = 8, 128, 256
X_HEADS, X_DIM = 4, 128
CONV_W = 3
NUM_BUCKETS, MAX_EXACT, MAX_DISTANCE = 32, 16, 128
FAR_DIST = 113
BIAS_A_OFF, BIAS_C_OFF, BIAS_D_OFF = 0, A_HEADS, A_HEADS + C_HEADS
SB_SKIP = -104.0


def _cparams(sem, vmem_bytes):
    return pltpu.CompilerParams(dimension_semantics=sem,
                                vmem_limit_bytes=int(min(max(vmem_bytes, 16 * MIB), VMEM_CAP)))


def _tile(n, pref):
    t = min(n, pref)
    while n % t:
        t //= 2
    return t


def _rmsnorm_kernel(x_ref, g_ref, o_ref):
    x = x_ref[...]
    ms = jnp.mean(x * x, axis=-1, keepdims=True)
    o_ref[...] = (x * lax.rsqrt(ms + EPS) * g_ref[...]).astype(o_ref.dtype)


def rmsnorm_rows(x, g):
    m, d = x.shape
    tm = _tile(m, 256)
    return pl.pallas_call(
        _rmsnorm_kernel,
        grid=(m // tm,),
        in_specs=[pl.BlockSpec((tm, d), lambda i: (i, 0)),
                  pl.BlockSpec((1, d), lambda i: (0, 0))],
        out_specs=pl.BlockSpec((tm, d), lambda i: (i, 0)),
        out_shape=jax.ShapeDtypeStruct((m, d), BF16),
        compiler_params=_cparams(("parallel",), 4 * tm * d * 6),
        name="rmsnorm_rows",
    )(x, g.reshape(1, d).astype(F32))


def _groupnorm_kernel(x_ref, g_ref, o_ref, *, gd):
    x = x_ref[...].astype(F32)
    width = x.shape[-1]
    g = g_ref[...]
    if gd >= LANES:
        for j in range(width // gd):
            xs = x[:, j * gd:(j + 1) * gd]
            ms = jnp.mean(xs * xs, axis=-1, keepdims=True)
            o_ref[:, j * gd:(j + 1) * gd] = (xs * lax.rsqrt(ms + EPS) * g).astype(o_ref.dtype)
    else:
        left = lax.broadcasted_iota(I32, (x.shape[0], LANES), 1) < gd
        for j in range(width // LANES):
            xs = x[:, j * LANES:(j + 1) * LANES]
            sq = xs * xs
            tot = jnp.sum(sq, axis=-1, keepdims=True)
            lsum = jnp.sum(jnp.where(left, sq, 0.0), axis=-1, keepdims=True)
            ms = jnp.where(left, lsum, tot - lsum) * (1.0 / gd)
            o_ref[:, j * LANES:(j + 1) * LANES] = (xs * lax.rsqrt(ms + EPS) * g).astype(o_ref.dtype)


def groupnorm(x, col0, width, gain, gd):
    m = x.shape[0]
    assert col0 % width == 0 and width % gd == 0 and (gd % LANES == 0 or 2 * gd == LANES)
    tm = _tile(m, 256)
    gw = gd if gd >= LANES else LANES
    g = jnp.tile(gain.astype(F32), gw // gd).reshape(1, gw)
    cb = col0 // width
    return pl.pallas_call(
        functools.partial(_groupnorm_kernel, gd=gd),
        grid=(m // tm,),
        in_specs=[pl.BlockSpec((tm, width), lambda i: (i, cb)),
                  pl.BlockSpec((1, gw), lambda i: (0, 0))],
        out_specs=pl.BlockSpec((tm, width), lambda i: (i, 0)),
        out_shape=jax.ShapeDtypeStruct((m, width), BF16),
        compiler_params=_cparams(("parallel",), 4 * tm * width * 8),
        name="groupnorm",
    )(x, g)


def _mm_kernel(*refs, nk, has_res, has_norm, head_dim):
    a_ref, w_ref = refs[0], refs[1]
    pos = 2
    res_ref = g_ref = None
    if has_res:
        res_ref = refs[pos]
        pos += 1
    if has_norm:
        g_ref = refs[pos]
        pos += 1
    o_ref = refs[pos]
    acc_ref = refs[pos + 1] if nk > 1 else None

    def epilogue(acc):
        if has_norm:
            ms = jnp.mean(acc * acc, axis=-1, keepdims=True)
            acc = acc * lax.rsqrt(ms + EPS) * g_ref[...]
        if has_res:
            acc = acc + res_ref[...]
        if head_dim is None:
            o_ref[...] = acc.astype(o_ref.dtype)
        else:
            for j in range(acc.shape[-1] // head_dim):
                o_ref[j] = acc[:, j * head_dim:(j + 1) * head_dim].astype(o_ref.dtype)

    part = jnp.dot(a_ref[...], w_ref[...], preferred_element_type=F32)
    if nk == 1:
        epilogue(part)
    else:
        k = pl.program_id(2)

        @pl.when(k == 0)
        def _():
            acc_ref[...] = part

        @pl.when(k > 0)
        def _():
            acc_ref[...] += part

        @pl.when(k == nk - 1)
        def _():
            epilogue(acc_ref[...])


def matmul(a, w, *, out_dtype, tm=1024, tn=512, tk=None, residual=None, norm_gain=None, head_dim=None):
    m, kdim = a.shape
    n = w.shape[1]
    tm = _tile(m, tm)
    tn = _tile(n, tn)
    tk = kdim if tk is None else _tile(kdim, tk)
    nk = kdim // tk
    assert norm_gain is None or norm_gain.shape[0] == tn
    in_specs = [pl.BlockSpec((tm, tk), lambda i, j, k: (i, k)),
                pl.BlockSpec((tk, tn), lambda i, j, k: (k, j))]
    args = [a, w]
    if residual is not None:
        in_specs.append(pl.BlockSpec((tm, tn), lambda i, j, k: (i, j)))
        args.append(residual)
    if norm_gain is not None:
        in_specs.append(pl.BlockSpec((1, tn), lambda i, j, k: (0, 0)))
        args.append(norm_gain.reshape(1, tn).astype(F32))
    if head_dim is None:
        out_spec = pl.BlockSpec((tm, tn), lambda i, j, k: (i, j))
        out_shape = jax.ShapeDtypeStruct((m, n), out_dtype)
    else:
        hpt = tn // head_dim
        out_spec = pl.BlockSpec((hpt, tm, head_dim), lambda i, j, k: (j, i, 0))
        out_shape = jax.ShapeDtypeStruct((n // head_dim, m, head_dim), out_dtype)
    osz = jnp.dtype(out_dtype).itemsize
    vmem = 2 * (tm * tk * 2 + tk * tn * 2 + tm * tn * osz) + tm * tn * 4 * 3
    if residual is not None:
        vmem += 2 * tm * tn * 4
    return pl.pallas_call(
        functools.partial(_mm_kernel, nk=nk, has_res=residual is not None,
                          has_norm=norm_gain is not None, head_dim=head_dim),
        grid=(m // tm, n // tn, nk),
        in_specs=in_specs,
        out_specs=out_spec,
        out_shape=out_shape,
        scratch_shapes=[pltpu.VMEM((tm, tn), F32)] if nk > 1 else [],
        compiler_params=_cparams(("parallel", "parallel", "arbitrary"), vmem + 4 * MIB),
        name="matmul",
    )(*args)


HALO = 16


def _ffn_gu_kernel(a_ref, ah_ref, wg_ref, wu_ref, cw_ref, cb_ref, o_ref, *, tm, seq):
    i = pl.program_id(0)
    a = a_ref[...]
    wg = wg_ref[...]
    g = jnp.dot(a, wg, preferred_element_type=F32)
    u = jnp.dot(a, wu_ref[...], preferred_element_type=F32)
    gh = jnp.dot(ah_ref[...], wg, preferred_element_type=F32)
    seq_start = (i * tm) % seq == 0
    gh = jnp.where(seq_start, 0.0, gh)
    rows = lax.broadcasted_iota(I32, g.shape, 0)
    hm1 = gh[HALO - 1:HALO, :]
    hm2 = gh[HALO - 2:HALO - 1, :]
    g1 = jnp.where(rows == 0, hm1, pltpu.roll(g, 1, 0))
    g2 = jnp.where(rows == 0, hm2, jnp.where(rows == 1, hm1, pltpu.roll(g, 2, 0)))
    cw = cw_ref[...]
    c = cw[0:1, :] * g2 + cw[1:2, :] * g1 + cw[2:3, :] * g + cb_ref[...]
    o_ref[...] = (c * jax.nn.sigmoid(c) * u).astype(o_ref.dtype)


def ffn_gate_up(h, wg, wu, conv_w, conv_b, seq):
    m, d = h.shape
    f = wg.shape[1]
    tm = _tile(seq, 1024)
    tn = _tile(f, 256)
    hb = tm // HALO
    vmem = 2 * (tm * d * 2 + HALO * d * 2 + 2 * d * tn * 2 + tm * tn * 2) + 8 * tm * tn * 4
    return pl.pallas_call(
        functools.partial(_ffn_gu_kernel, tm=tm, seq=seq),
        grid=(m // tm, f // tn),
        in_specs=[pl.BlockSpec((tm, d), lambda i, j: (i, 0)),
                  pl.BlockSpec((HALO, d), lambda i, j: (jnp.maximum(i * hb - 1, 0), 0)),
                  pl.BlockSpec((d, tn), lambda i, j: (0, j)),
                  pl.BlockSpec((d, tn), lambda i, j: (0, j)),
                  pl.BlockSpec((CONV_W, tn), lambda i, j: (0, j)),
                  pl.BlockSpec((1, tn), lambda i, j: (0, j))],
        out_specs=pl.BlockSpec((tm, tn), lambda i, j: (i, j)),
        out_shape=jax.ShapeDtypeStruct((m, f), BF16),
        compiler_params=_cparams(("parallel", "parallel"), vmem + 4 * MIB),
        name="ffn_gate_up",
    )(h, h, wg, wu, conv_w, conv_b.reshape(1, f))


def _t5_bucket(dist):
    n = jnp.maximum(dist, 0)
    nf = jnp.maximum(n, 1).astype(F32)
    large = MAX_EXACT + (jnp.log(nf / MAX_EXACT) / math.log(MAX_DISTANCE / MAX_EXACT)
                         * (NUM_BUCKETS - MAX_EXACT)).astype(I32)
    return jnp.where(n < MAX_EXACT, n, jnp.minimum(large, NUM_BUCKETS - 1))


def _near_kinds(tq, tk):
    return -(-(FAR_DIST - 1 + tk) // tq)


def _toeplitz_bias(table, tq, tk):
    kinds = _near_kinds(tq, tk) + 1
    d = jnp.arange(kinds, dtype=I32)[:, None, None] * tq
    dist = d + jnp.arange(tq, dtype=I32)[None, :, None] - jnp.arange(tk, dtype=I32)[None, None, :]
    tiles = table.astype(F32)[_t5_bucket(dist)]
    return jnp.moveaxis(tiles, -1, 1)


def _sortable_key(x):
    b = lax.bitcast_convert_type(x, I32)
    return b ^ ((b >> 31) & jnp.int32(0x7FFFFFFF))


def _indexer_kernel(q_ref, kt_ref, wi_ref, sc_ref, thr_ref, w_scr, key_scr, *, tq, tkc, topk, wscale):
    i = pl.program_id(1)
    nch = (i * tq + tq + tkc - 1) // tkc
    wi = wi_ref[...] * wscale
    for h in range(IDX_HEADS):
        w_scr[h] = jnp.broadcast_to(wi[:, h:h + 1], (tq, LANES))
    q2 = q_ref[...].reshape(IDX_HEADS * tq, IDX_DIM)
    sc_ref[...] = jnp.full(sc_ref.shape, -jnp.inf, F32)
    t_idx = i * tq + lax.broadcasted_iota(I32, (tq, tkc), 0)
    s_loc = lax.broadcasted_iota(I32, (tq, tkc), 1)

    def chunk(c, carry):
        off = pl.multiple_of(c * tkc, tkc)
        kc = kt_ref[:, pl.ds(off, tkc)]
        r = jnp.dot(q2, kc, preferred_element_type=F32)
        r = jnp.maximum(r, 0.0).reshape(IDX_HEADS, tq, tkc)
        w = w_scr[...]
        w = jnp.concatenate([w] * (tkc // LANES), axis=-1)
        s = jnp.sum(r * w, axis=0)
        s = jnp.where(off + s_loc <= t_idx, s, -jnp.inf)
        sc_ref[:, pl.ds(off, tkc)] = s
        key_scr[:, pl.ds(off, tkc)] = _sortable_key(s)
        return carry

    lax.fori_loop(0, nch, chunk, 0)

    def bit_step(it, v):
        cand = v + jnp.left_shift(jnp.int32(1), 31 - it)

        def count(c, acc):
            off = pl.multiple_of(c * tkc, tkc)
            return acc + (key_scr[:, pl.ds(off, tkc)] >= cand).astype(I32)

        acc = lax.fori_loop(0, nch, count, jnp.zeros((tq, tkc), I32))
        cnt = jnp.sum(acc, axis=-1, keepdims=True)
        return jnp.where(cnt >= topk, cand, v)

    thr_ref[...] = lax.fori_loop(0, 32, bit_step, jnp.full((tq, 1), INT_MIN, I32))


def dsa_indexer(qi3, kit, pa, wi_colblock, bsz, seq, topk):
    n = bsz * seq
    tq = _tile(seq, 128)
    tkc = _tile(seq, 256)
    nq = seq // tq
    vmem = 2 * (IDX_HEADS * tq * IDX_DIM * 2 + IDX_DIM * seq * 2 + tq * LANES * 4 + tq * seq * 4) \
        + IDX_HEADS * tq * LANES * 4 + tq * seq * 4 + 3 * IDX_HEADS * tq * tkc * 4
    return pl.pallas_call(
        functools.partial(_indexer_kernel, tq=tq, tkc=tkc, topk=topk,
                          wscale=IDX_HEADS ** -0.5 * IDX_DIM ** -0.5),
        grid=(bsz, nq),
        in_specs=[pl.BlockSpec((IDX_HEADS, tq, IDX_DIM), lambda b, i: (0, b * nq + i, 0)),
                  pl.BlockSpec((None, IDX_DIM, seq), lambda b, i: (b, 0, 0)),
                  pl.BlockSpec((tq, LANES), lambda b, i: (b * nq + i, wi_colblock))],
        out_specs=[pl.BlockSpec((tq, seq), lambda b, i: (b * nq + i, 0)),
                   pl.BlockSpec((tq, 1), lambda b, i: (b * nq + i, 0))],
        out_shape=[jax.ShapeDtypeStruct((n, seq), F32),
                   jax.ShapeDtypeStruct((n, 1), I32)],
        scratch_shapes=[pltpu.VMEM((IDX_HEADS, tq, LANES), F32),
                        pltpu.VMEM((tq, seq), I32)],
        compiler_params=_cparams(("parallel", "parallel"), vmem + 4 * MIB),
        name="dsa_indexer",
    )(qi3, kit, pa)


def _dsa_attn_kernel(q_ref, c_ref, sc_ref, thr_ref, bias_ref, wuv_ref, o_ref, m_scr, l_scr, acc_scr,
                     *, tq, tk, nkb, kinds, scale):
    i = pl.program_id(1)
    kb = pl.program_id(2)
    kbmax = (i * tq + tq - 1) // tk
    nh = A_HEADS

    @pl.when(kb == 0)
    def _():
        m_scr[...] = jnp.full(m_scr.shape, M_INIT, F32)
        l_scr[...] = jnp.zeros(l_scr.shape, F32)
        acc_scr[...] = jnp.zeros(acc_scr.shape, F32)

    @pl.when(kb <= kbmax)
    def _():
        q2 = q_ref[...].reshape(nh * tq, A_KV_RANK)
        cb = c_ref[...]
        logits = lax.dot_general(q2, cb, (((1,), (1,)), ((), ())), preferred_element_type=F32)
        key = _sortable_key(sc_ref[...])
        t_idx = i * tq + lax.broadcasted_iota(I32, (tq, tk), 0)
        s_idx = kb * tk + lax.broadcasted_iota(I32, (tq, tk), 1)
        madd = jnp.where(s_idx <= t_idx, jnp.where(key >= thr_ref[...], 0.0, -jnp.inf), -jnp.inf)
        kind = jnp.minimum(i - kb * (tk // tq), kinds - 1)
        l3 = logits.reshape(nh, tq, tk) * scale + bias_ref[kind] + madd[None]
        m_old = m_scr[...]
        m_new = jnp.maximum(m_old, jnp.max(l3, axis=-1, keepdims=True))
        alpha = jnp.exp(m_old - m_new)
        p = jnp.exp(l3 - m_new)
        l_scr[...] = alpha * l_scr[...] + jnp.sum(p, axis=-1, keepdims=True)
        pv = jnp.dot(p.reshape(nh * tq, tk).astype(BF16), cb, preferred_element_type=F32)
        acc_scr[...] = alpha.reshape(nh * tq, 1) * acc_scr[...] + pv
        m_scr[...] = m_new

    @pl.when(kb == nkb - 1)
    def _():
        o_lat = (acc_scr[...] / l_scr[...].reshape(nh * tq, 1)).astype(BF16)
        for h in range(nh):
            o_ref[:, h * A_V_DIM:(h + 1) * A_V_DIM] = jnp.dot(
                o_lat[h * tq:(h + 1) * tq], wuv_ref[h], preferred_element_type=F32).astype(o_ref.dtype)


def dsa_attention(q3, c, scores, thr, bias_a, w_uv, bsz, seq):
    n = bsz * seq
    tq = _tile(seq, 128)
    tk = _tile(seq, 256)
    nq, nkb = seq // tq, seq // tk
    bias = _toeplitz_bias(bias_a, tq, tk)
    kinds = bias.shape[0]

    def kclamp(i, kb):
        return jnp.minimum(kb, (i * tq + tq - 1) // tk)

    rows = A_HEADS * tq
    vmem = 2 * (rows * A_KV_RANK * 2 + tk * A_KV_RANK * 2 + tq * tk * 4 + kinds * rows * tk * 4
                + A_HEADS * A_KV_RANK * A_V_DIM * 2 + tq * A_HEADS * A_V_DIM * 2) \
        + rows * A_KV_RANK * 4 + 2 * rows * LANES * 4 + 5 * rows * tk * 4
    return pl.pallas_call(
        functools.partial(_dsa_attn_kernel, tq=tq, tk=tk, nkb=nkb, kinds=kinds, scale=A_KV_RANK ** -0.5),
        grid=(bsz, nq, nkb),
        in_specs=[pl.BlockSpec((A_HEADS, tq, A_KV_RANK), lambda b, i, kb: (0, b * nq + i, 0)),
                  pl.BlockSpec((tk, A_KV_RANK), lambda b, i, kb: (b * nkb + kclamp(i, kb), 0)),
                  pl.BlockSpec((tq, tk), lambda b, i, kb: (b * nq + i, kclamp(i, kb))),
                  pl.BlockSpec((tq, 1), lambda b, i, kb: (b * nq + i, 0)),
                  pl.BlockSpec((kinds, A_HEADS, tq, tk), lambda b, i, kb: (0, 0, 0, 0)),
                  pl.BlockSpec((A_HEADS, A_KV_RANK, A_V_DIM), lambda b, i, kb: (0, 0, 0))],
        out_specs=pl.BlockSpec((tq, A_HEADS * A_V_DIM), lambda b, i, kb: (b * nq + i, 0)),
        out_shape=jax.ShapeDtypeStruct((n, A_HEADS * A_V_DIM), BF16),
        scratch_shapes=[pltpu.VMEM((A_HEADS, tq, 1), F32),
                        pltpu.VMEM((A_HEADS, tq, 1), F32),
                        pltpu.VMEM((rows, A_KV_RANK), F32)],
        compiler_params=_cparams(("parallel", "parallel", "arbitrary"), vmem + 4 * MIB),
        name="dsa_attention",
    )(q3, c, scores, thr, bias, w_uv)


def _sb_kernel(q_ref, k_ref, v_ref, u_ref, o_ref, carry_scr, acc_scr, *, tq, scale):
    i = pl.program_id(2)
    q = q_ref[...]
    tri = u_ref[...]
    row = lax.broadcasted_iota(I32, (tq, tq), 0)
    col = lax.broadcasted_iota(I32, (tq, tq), 1)
    strict = col < row

    def step(kb, diag):
        off = pl.multiple_of(kb * tq, tq)
        k = k_ref[pl.ds(off, tq), :]
        v = v_ref[pl.ds(off, tq), :]
        z = lax.dot_general(q, k, (((1,), (1,)), ((), ())), preferred_element_type=F32) * scale
        log1m = -(jnp.maximum(z, 0.0) + jnp.log1p(jnp.exp(-jnp.abs(z))))
        if diag:
            log1m = jnp.where(strict, log1m, 0.0)
        hi = log1m.astype(BF16)
        lo = (log1m - hi.astype(F32)).astype(BF16)
        between = (jnp.dot(hi, tri, preferred_element_type=F32)
                   + jnp.dot(lo, tri, preferred_element_type=F32))
        carry = carry_scr[...]
        w = jnp.exp(z + log1m + between + carry)
        if diag:
            w = jnp.where(strict, w, 0.0)
        acc_scr[...] += jnp.dot(w.astype(BF16), v, preferred_element_type=F32)
        carry = carry + jnp.sum(log1m, axis=-1, keepdims=True)
        carry_scr[...] = carry
        return jnp.max(carry)

    carry_scr[...] = jnp.zeros(carry_scr.shape, F32)
    acc_scr[...] = jnp.zeros(acc_scr.shape, F32)
    cmax = step(i, True)

    def cond(state):
        kb, cmax = state
        return jnp.logical_and(kb >= 0, cmax > SB_SKIP)

    def body(state):
        kb, _ = state
        return kb - 1, step(kb, False)

    lax.while_loop(cond, body, (i - 1, cmax))
    o_ref[...] = acc_scr[...].astype(o_ref.dtype)


def stickbreak_attention(qkv, bsz, seq):
    n = bsz * seq
    tq = _tile(seq, 256)
    nq = seq // tq
    tri = (jnp.arange(tq)[:, None] > jnp.arange(tq)[None, :]).astype(BF16)
    vmem = 2 * (2 * seq * B_DIM * 2 + 2 * tq * B_DIM * 2 + tq * tq * 2) + 12 * tq * tq * 4
    return pl.pallas_call(
        functools.partial(_sb_kernel, tq=tq, scale=B_DIM ** -0.5),
        grid=(bsz, B_HEADS, nq),
        in_specs=[pl.BlockSpec((tq, B_DIM), lambda b, h, i: (b * nq + i, h)),
                  pl.BlockSpec((seq, B_DIM), lambda b, h, i: (b, B_HEADS + h)),
                  pl.BlockSpec((seq, B_DIM), lambda b, h, i: (b, 2 * B_HEADS + h)),
                  pl.BlockSpec((tq, tq), lambda b, h, i: (0, 0))],
        out_specs=pl.BlockSpec((tq, B_DIM), lambda b, h, i: (b * nq + i, h)),
        out_shape=jax.ShapeDtypeStruct((n, B_HEADS * B_DIM), BF16),
        scratch_shapes=[pltpu.VMEM((tq, 1), F32), pltpu.VMEM((tq, B_DIM), F32)],
        compiler_params=_cparams(("parallel", "parallel", "parallel"), vmem + 4 * MIB),
        name="stickbreak_attention",
    )(qkv, qkv, qkv, tri)


def _diff_kernel(q_ref, k_ref, v_ref, bias_ref, lam_ref, subg_ref, o_ref, m_scr, l_scr, acc_scr,
                 *, tq, kinds, scale, lambda_init):
    i = pl.program_id(2)
    dq = D_QK_DIM
    qs = (q_ref[:, :dq], q_ref[:, dq:])
    lv = lam_ref[...]
    lam = (jnp.exp(jnp.sum(lv[0:1] * lv[1:2], axis=-1, keepdims=True))
           - jnp.exp(jnp.sum(lv[2:3] * lv[3:4], axis=-1, keepdims=True))) + lambda_init
    row = lax.broadcasted_iota(I32, (tq, tq), 0)
    col = lax.broadcasted_iota(I32, (tq, tq), 1)
    causal = col <= row

    m_scr[...] = jnp.full(m_scr.shape, M_INIT, F32)
    l_scr[...] = jnp.zeros(l_scr.shape, F32)
    acc_scr[...] = jnp.zeros(acc_scr.shape, F32)

    def step(kb, diag):
        off = pl.multiple_of(kb * tq, tq)
        k = k_ref[pl.ds(off, tq), :]
        v = v_ref[pl.ds(off, tq), :]
        bias = bias_ref[jnp.minimum(i - kb, kinds - 1)]
        for c in range(2):
            s = lax.dot_general(qs[c], k[:, c * dq:(c + 1) * dq], (((1,), (1,)), ((), ())),
                                preferred_element_type=F32) * scale + bias
            if diag:
                s = jnp.where(causal, s, -jnp.inf)
            m_old = m_scr[c]
            m_new = jnp.maximum(m_old, jnp.max(s, axis=-1, keepdims=True))
            alpha = jnp.exp(m_old - m_new)
            p = jnp.exp(s - m_new)
            l_scr[c] = alpha * l_scr[c] + jnp.sum(p, axis=-1, keepdims=True)
            acc_scr[c] = alpha * acc_scr[c] + jnp.dot(p.astype(BF16), v, preferred_element_type=F32)
            m_scr[c] = m_new

    def far(kb, carry):
        step(kb, False)
        return carry

    lax.fori_loop(0, i, far, 0)
    step(i, True)
    out = acc_scr[0] / l_scr[0] - lam * (acc_scr[1] / l_scr[1])
    ms = jnp.mean(out * out, axis=-1, keepdims=True)
    o_ref[...] = (out * lax.rsqrt(ms + EPS) * subg_ref[...] * (1.0 - lambda_init)).astype(o_ref.dtype)


def diff_attention(dq, dk, dv, bias_d, lam_vecs, sub_g, bsz, seq, lambda_init):
    n = bsz * seq
    tq = _tile(seq, 256)
    nq = seq // tq
    bias = _toeplitz_bias(bias_d, tq, tq)
    kinds = bias.shape[0]
    width = 2 * D_QK_DIM
    vmem = 2 * (tq * width * 2 + seq * width * 2 + seq * D_V_DIM * 2 + kinds * tq * tq * 4
                + tq * D_V_DIM * 2) + 2 * tq * D_V_DIM * 4 + 4 * tq * LANES * 4 + 10 * tq * tq * 4
    return pl.pallas_call(
        functools.partial(_diff_kernel, tq=tq, kinds=kinds, scale=D_QK_DIM ** -0.5, lambda_init=lambda_init),
        grid=(bsz, D_HEADS, nq),
        in_specs=[pl.BlockSpec((tq, width), lambda b, h, i: (b * nq + i, h)),
                  pl.BlockSpec((seq, width), lambda b, h, i: (b, h)),
                  pl.BlockSpec((seq, D_V_DIM), lambda b, h, i: (b, h)),
                  pl.BlockSpec((kinds, None, tq, tq), lambda b, h, i: (0, h, 0, 0)),
                  pl.BlockSpec((4, D_QK_DIM), lambda b, h, i: (0, 0)),
                  pl.BlockSpec((1, D_V_DIM), lambda b, h, i: (0, 0))],
        out_specs=pl.BlockSpec((tq, D_V_DIM), lambda b, h, i: (b * nq + i, h)),
        out_shape=jax.ShapeDtypeStruct((n, D_HEADS * D_V_DIM), BF16),
        scratch_shapes=[pltpu.VMEM((2, tq, 1), F32), pltpu.VMEM((2, tq, 1), F32),
                        pltpu.VMEM((2, tq, D_V_DIM), F32)],
        compiler_params=_cparams(("parallel", "parallel", "parallel"), vmem + 4 * MIB),
        name="diff_attention",
    )(dq, dk, dv, bias, lam_vecs, sub_g.reshape(1, D_V_DIM).astype(F32))


def _swa_kernel(q_ref, kp_ref, ko_ref, vp_ref, vo_ref, bias_ref, sink_ref, o_ref, *, blk, scale):
    i = pl.program_id(1)
    left = lax.broadcasted_iota(I32, (blk, LANES), 1) < C_DIM
    qa = lax.broadcasted_iota(I32, (blk, 2 * blk), 0)
    sa = lax.broadcasted_iota(I32, (blk, 2 * blk), 1)
    rel = qa + blk - sa
    lo = jnp.where(i > 0, 0, blk)
    madd = jnp.where(rel >= 0, jnp.where(rel < WINDOW, jnp.where(sa >= lo, 0.0, -jnp.inf), -jnp.inf), -jnp.inf)
    grp = C_HEADS // C_KV_HEADS
    for g in range(C_KV_HEADS):
        ksl = slice(g * LANES, (g + 1) * LANES)
        kk = jnp.concatenate([kp_ref[:, ksl], ko_ref[:, ksl]], axis=0)
        vv = jnp.concatenate([vp_ref[:, ksl], vo_ref[:, ksl]], axis=0)
        for pr in range(grp // 2):
            cb = g * (grp // 2) + pr
            qp = q_ref[:, cb * LANES:(cb + 1) * LANES]
            outs = []
            for half in range(2):
                h = 2 * cb + half
                qh = jnp.where(left if half == 0 else jnp.logical_not(left), qp, jnp.zeros_like(qp))
                lg = lax.dot_general(qh, kk, (((1,), (1,)), ((), ())), preferred_element_type=F32) * scale
                lg = lg + bias_ref[h] + madd
                sink = sink_ref[h]
                m = jnp.maximum(jnp.max(lg, axis=-1, keepdims=True), sink)
                e = jnp.exp(lg - m)
                p = e / (jnp.sum(e, axis=-1, keepdims=True) + jnp.exp(sink - m))
                outs.append(jnp.dot(p.astype(BF16), vv, preferred_element_type=F32))
            o_ref[:, cb * LANES:(cb + 1) * LANES] = jnp.where(left, outs[0], outs[1]).astype(o_ref.dtype)


def swa_attention(cq, ck2, cv2, bias_c, sinks, bsz, seq):
    n = bsz * seq
    blk = WINDOW
    nb = seq // blk
    rel = jnp.arange(blk, dtype=I32)[:, None] + blk - jnp.arange(2 * blk, dtype=I32)[None, :]
    bias = jnp.moveaxis(bias_c.astype(F32)[_t5_bucket(rel)], -1, 0)
    qw = C_HEADS * C_DIM
    kw = C_KV_HEADS * LANES
    own = lambda b, i: (b * nb + i, 0)
    prev = lambda b, i: (b * nb + jnp.maximum(i - 1, 0), 0)
    vmem = 2 * (2 * blk * qw * 2 + 4 * blk * kw * 2 + C_HEADS * blk * 2 * blk * 4) + 16 * blk * 2 * blk * 4
    return pl.pallas_call(
        functools.partial(_swa_kernel, blk=blk, scale=C_DIM ** -0.5),
        grid=(bsz, nb),
        in_specs=[pl.BlockSpec((blk, qw), own),
                  pl.BlockSpec((blk, kw), prev), pl.BlockSpec((blk, kw), own),
                  pl.BlockSpec((blk, kw), prev), pl.BlockSpec((blk, kw), own),
                  pl.BlockSpec((C_HEADS, blk, 2 * blk), lambda b, i: (0, 0, 0)),
                  pl.BlockSpec(memory_space=pltpu.SMEM)],
        out_specs=pl.BlockSpec((blk, qw), own),
        out_shape=jax.ShapeDtypeStruct((n, qw), BF16),
        compiler_params=_cparams(("parallel", "parallel"), vmem + 4 * MIB),
        name="swa_attention",
    )(cq, ck2, ck2, cv2, cv2, bias, sinks.astype(F32))


def _xattn_kernel(q_ref, k_ref, v_ref, qg_ref, kg_ref, o_ref, *, scale):
    for h in range(X_HEADS):
        sl = slice(h * X_DIM, (h + 1) * X_DIM)
        q = q_ref[:, sl]
        k = k_ref[:, sl]
        qn = (q * lax.rsqrt(jnp.mean(q * q, axis=-1, keepdims=True) + EPS) * qg_ref[...]).astype(BF16)
        kn = (k * lax.rsqrt(jnp.mean(k * k, axis=-1, keepdims=True) + EPS) * kg_ref[...]).astype(BF16)
        lg = lax.dot_general(qn, kn, (((1,), (1,)), ((), ())), preferred_element_type=F32) * scale
        m = jnp.max(lg, axis=-1, keepdims=True)
        e = jnp.exp(lg - m)
        p = (e / jnp.sum(e, axis=-1, keepdims=True)).astype(BF16)
        o_ref[:, sl] = jnp.dot(p, v_ref[:, sl], preferred_element_type=F32).astype(o_ref.dtype)


def cross_attention(qx, kx, vx, q_g, k_g, bsz, seq, mem_len):
    n = bsz * seq
    tq = _tile(seq, 512)
    nq = seq // tq
    w = X_HEADS * X_DIM
    vmem = 2 * (tq * w * 4 + mem_len * w * 6 + tq * w * 2) + 8 * tq * mem_len * 4
    return pl.pallas_call(
        functools.partial(_xattn_kernel, scale=X_DIM ** -0.5),
        grid=(bsz, nq),
        in_specs=[pl.BlockSpec((tq, w), lambda b, i: (b * nq + i, 0)),
                  pl.BlockSpec((mem_len, w), lambda b, i: (b, 0)),
                  pl.BlockSpec((mem_len, w), lambda b, i: (b, 0)),
                  pl.BlockSpec((1, X_DIM), lambda b, i: (0, 0)),
                  pl.BlockSpec((1, X_DIM), lambda b, i: (0, 0))],
        out_specs=pl.BlockSpec((tq, w), lambda b, i: (b * nq + i, 0)),
        out_shape=jax.ShapeDtypeStruct((n, w), BF16),
        compiler_params=_cparams(("parallel", "parallel"), vmem + 4 * MIB),
        name="cross_attention",
    )(qx, kx, vx, q_g.reshape(1, X_DIM).astype(F32), k_g.reshape(1, X_DIM).astype(F32))


def _even_mixer(xf, h, bsz, seq, w_in, q_lat_g, w_qb, q_g, kv_g, w_uv, w_qi, kidx_g, w_out, rel_bias):
    d = xf.shape[1]
    a_cols = A_Q_RANK + A_KV_RANK + IDX_DIM
    w_a = jnp.concatenate([w_in[:, :a_cols + IDX_HEADS],
                           jnp.zeros((d, LANES - IDX_HEADS), w_in.dtype)], axis=1).astype(BF16)
    w_b = w_in[:, a_cols + IDX_HEADS:].astype(BF16)
    pa = matmul(h, w_a, out_dtype=F32, tn=256)
    qkv = matmul(h, w_b, out_dtype=BF16, tn=512)

    cq = groupnorm(pa, 0, A_Q_RANK, q_lat_g, A_Q_RANK)
    c = groupnorm(pa, A_Q_RANK, A_KV_RANK, kv_g, A_KV_RANK)
    ki = groupnorm(pa, A_Q_RANK + A_KV_RANK, IDX_DIM, kidx_g, IDX_DIM)
    kit = jnp.swapaxes(ki.reshape(bsz, seq, IDX_DIM), 1, 2)
    q3 = matmul(cq, w_qb.reshape(A_Q_RANK, A_HEADS * A_KV_RANK).astype(BF16), out_dtype=BF16,
                tn=A_KV_RANK, norm_gain=q_g, head_dim=A_KV_RANK)
    qi3 = matmul(cq, w_qi.reshape(A_Q_RANK, IDX_HEADS * IDX_DIM).astype(BF16), out_dtype=BF16,
                 tn=512, head_dim=IDX_DIM)
    topk = min(TOPK_MAX, seq // 4)
    scores, thr = dsa_indexer(qi3, kit, pa, a_cols // LANES, bsz, seq, topk)
    o_a = dsa_attention(q3, c, scores, thr, rel_bias[:, BIAS_A_OFF:BIAS_A_OFF + A_HEADS],
                        w_uv.astype(BF16), bsz, seq)
    o_b = stickbreak_attention(qkv, bsz, seq)
    o = jnp.concatenate([o_a, o_b], axis=-1)
    return matmul(o, w_out.astype(BF16), out_dtype=F32, tn=1024, residual=xf)


def _odd_mixer(xf, h, bsz, seq, w_in, c_q_g, c_k_g, sinks, d_q_g, d_k_g, lam_q1, lam_k1, lam_q2, lam_k2,
               sub_g, w_out, rel_bias, lambda_init):
    n = xf.shape[0]
    cw = C_HEADS * C_DIM
    ckw = C_KV_HEADS * C_DIM
    dw = D_HEADS * 2 * D_QK_DIM
    c_end = cw + 2 * ckw
    p1 = matmul(h, w_in[:, :c_end].astype(BF16), out_dtype=F32, tn=512)
    p2 = matmul(h, w_in[:, c_end:c_end + 2 * dw].astype(BF16), out_dtype=F32, tn=512)
    dv = matmul(h, w_in[:, c_end + 2 * dw:].astype(BF16), out_dtype=BF16, tn=512)

    cq = groupnorm(p1, 0, cw, c_q_g, C_DIM)
    ck = groupnorm(p1, cw, ckw, c_k_g, C_DIM)
    cv = p1[:, cw + ckw:].astype(BF16)
    dup = lambda a: jnp.concatenate([a.reshape(n, C_KV_HEADS, C_DIM)] * 2, axis=-1).reshape(n, C_KV_HEADS * LANES)
    o_c = swa_attention(cq, dup(ck), dup(cv), rel_bias[:, BIAS_C_OFF:BIAS_C_OFF + C_HEADS], sinks, bsz, seq)

    dq = groupnorm(p2, 0, dw, d_q_g, D_QK_DIM)
    dk = groupnorm(p2, dw, dw, d_k_g, D_QK_DIM)
    lam_vecs = jnp.stack([lam_q1, lam_k1, lam_q2, lam_k2]).astype(F32)
    o_d = diff_attention(dq, dk, dv, rel_bias[:, BIAS_D_OFF:BIAS_D_OFF + D_HEADS], lam_vecs, sub_g,
                         bsz, seq, lambda_init)
    o = jnp.concatenate([o_c, o_d], axis=-1)
    return matmul(o, w_out.astype(BF16), out_dtype=F32, tn=1024, residual=xf)


def _cross_block(xf, memn, norm_g, wq, wk, wv, q_g, k_g, wo, bsz, seq, mem_len):
    d = xf.shape[1]
    w = X_HEADS * X_DIM
    h = rmsnorm_rows(xf, norm_g)
    qx = matmul(h, wq.reshape(d, w).astype(BF16), out_dtype=F32, tn=512)
    kx = matmul(memn, wk.reshape(d, w).astype(BF16), out_dtype=F32, tn=512)
    vx = matmul(memn, wv.reshape(d, w).astype(BF16), out_dtype=BF16, tn=512)
    ox = cross_attention(qx, kx, vx, q_g, k_g, bsz, seq, mem_len)
    return matmul(ox, wo.astype(BF16), out_dtype=F32, tn=1024, residual=xf)


def _ffn_block(xf, norm_g, w_gate, w_up, conv_w, conv_b, w_down, seq):
    f = w_gate.shape[1]
    fp = -(-f // 1024) * 1024
    pad = fp - f
    h = rmsnorm_rows(xf, norm_g)
    wg = jnp.pad(w_gate, ((0, 0), (0, pad))).astype(BF16)
    wu = jnp.pad(w_up, ((0, 0), (0, pad))).astype(BF16)
    wd = jnp.pad(w_down, ((0, pad), (0, 0))).astype(BF16)
    cw = jnp.pad(conv_w, ((0, 0), (0, pad))).astype(F32)
    cb = jnp.pad(conv_b, ((0, pad),)).astype(F32)
    act = ffn_gate_up(h, wg, wu, cw, cb, seq)
    return matmul(act, wd, out_dtype=F32, tm=1024, tn=1024, tk=fp // 4, residual=xf)


def kernel(x, mem, rel_bias, mem_norm_g, mix_norm_g, xattn_norm_g, ffn_norm_g, ev_w_in, ev_q_lat_g, ev_w_qb, ev_q_g, ev_kv_g, ev_w_uv, ev_w_qi, ev_kidx_g, ev_w_out, od_w_in, od_c_q_g, od_c_k_g, od_sinks, od_d_q_g, od_d_k_g, od_lam_q1, od_lam_k1, od_lam_q2, od_lam_k2, od_sub_g, od_w_out, x_wq, x_wk, x_wv, x_q_g, x_k_g, x_wo, f_w_gate, f_w_up, f_conv_w, f_conv_b, f_w_down):
    bsz, seq, d = x.shape
    mem_len = mem.shape[1]
    depth = mix_norm_g.shape[0]
    xf = x.reshape(bsz * seq, d)
    memn = rmsnorm_rows(mem.reshape(bsz * mem_len, d), mem_norm_g)
    for l in range(depth):
        h = rmsnorm_rows(xf, mix_norm_g[l])
        if l % 2 == 0:
            e = l // 2
            xf = _even_mixer(xf, h, bsz, seq, ev_w_in[e], ev_q_lat_g[e], ev_w_qb[e], ev_q_g[e], ev_kv_g[e],
                             ev_w_uv[e], ev_w_qi[e], ev_kidx_g[e], ev_w_out[e], rel_bias)
        else:
            o = l // 2
            lambda_init = 0.8 - 0.6 * math.exp(-0.3 * l)
            xf = _odd_mixer(xf, h, bsz, seq, od_w_in[o], od_c_q_g[o], od_c_k_g[o], od_sinks[o], od_d_q_g[o],
                            od_d_k_g[o], od_lam_q1[o], od_lam_k1[o], od_lam_q2[o], od_lam_k2[o], od_sub_g[o],
                            od_w_out[o], rel_bias, lambda_init)
        xf = _cross_block(xf, memn, xattn_norm_g[l], x_wq[l], x_wk[l], x_wv[l], x_q_g[l], x_k_g[l], x_wo[l],
                          bsz, seq, mem_len)
        xf = _ffn_block(xf, ffn_norm_g[l], f_w_gate[l], f_w_up[l], f_conv_w[l], f_conv_b[l], f_w_down[l], seq)
    return xf.reshape(bsz, seq, d)
```

```python
import functools
import math

import jax
import jax.numpy as jnp
from jax import lax
from jax.experimental import pallas as pl
from jax.experimental.pallas import tpu as pltpu

F32 = jnp.float32
BF16 = jnp.bfloat16
I32 = jnp.int32

EPS = 1e-6
LANES = 128
MIB = 1024 * 1024
VMEM_CAP = 58 * MIB
M_INIT = -1e30
INT_MIN = -(2 ** 31)

A_HEADS, A_Q_RANK, A_KV_RANK, A_V_DIM = 16, 1024, 512, 128
IDX_HEADS, IDX_DIM, TOPK_MAX = 32, 128, 256
B_HEADS, B_DIM = 16, 128
C_HEADS, C_KV_HEADS, C_DIM, WINDOW = 32, 4, 64, 128
D_HEADS, D_QK_DIM, D_V_DIM = 8, 128, 256
X_HEADS, X_DIM = 4, 128
CONV_W = 3
NUM_BUCKETS, MAX_EXACT, MAX_DISTANCE = 32, 16, 128
FAR_DIST = 113
BIAS_A_OFF, BIAS_C_OFF, BIAS_D_OFF = 0, A_HEADS, A_HEADS + C_HEADS
SB_SKIP = -104.0


def _cparams(sem, vmem_bytes):
    return pltpu.CompilerParams(dimension_semantics=sem,
                                vmem_limit_bytes=int(min(max(vmem_bytes, 16 * MIB), VMEM_CAP)))


def _tile(n, pref):
    t = min(n, pref)
    while n % t:
        t //= 2
    return t


def _rmsnorm_kernel(x_ref, g_ref, o_ref):
    x = x_ref[...]
    ms = jnp.mean(x * x, axis=-1, keepdims=True)
    o_ref[...] = (x * lax.rsqrt(ms + EPS) * g_ref[...]).astype(o_ref.dtype)


def rmsnorm_rows(x, g):
    m, d = x.shape
    tm = _tile(m, 256)
    return pl.pallas_call(
        _rmsnorm_kernel,
        grid=(m // tm,),
        in_specs=[pl.BlockSpec((tm, d), lambda i: (i, 0)),
                  pl.BlockSpec((1, d), lambda i: (0, 0))],
        out_specs=pl.BlockSpec((tm, d), lambda i: (i, 0)),
        out_shape=jax.ShapeDtypeStruct((m, d), BF16),
        compiler_params=_cparams(("parallel",), 4 * tm * d * 6),
        name="rmsnorm_rows",
    )(x, g.reshape(1, d).astype(F32))


def _groupnorm_kernel(x_ref, g_ref, o_ref, *, gd):
    x = x_ref[...].astype(F32)
    width = x.shape[-1]
    g = g_ref[...]
    if gd >= LANES:
        for j in range(width // gd):
            xs = x[:, j * gd:(j + 1) * gd]
            ms = jnp.mean(xs * xs, axis=-1, keepdims=True)
            o_ref[:, j * gd:(j + 1) * gd] = (xs * lax.rsqrt(ms + EPS) * g).astype(o_ref.dtype)
    else:
        left = lax.broadcasted_iota(I32, (x.shape[0], LANES), 1) < gd
        for j in range(width // LANES):
            xs = x[:, j * LANES:(j + 1) * LANES]
            sq = xs * xs
            tot = jnp.sum(sq, axis=-1, keepdims=True)
            lsum = jnp.sum(jnp.where(left, sq, 0.0), axis=-1, keepdims=True)
            ms = jnp.where(left, lsum, tot - lsum) * (1.0 / gd)
            o_ref[:, j * LANES:(j + 1) * LANES] = (xs * lax.rsqrt(ms + EPS) * g).astype(o_ref.dtype)


def groupnorm(x, col0, width, gain, gd):
    m = x.shape[0]
    assert col0 % width == 0 and width % gd == 0 and (gd % LANES == 0 or 2 * gd == LANES)
    tm = _tile(m, 256)
    gw = gd if gd >= LANES else LANES
    g = jnp.tile(gain.astype(F32), gw // gd).reshape(1, gw)
    cb = col0 // width
    return pl.pallas_call(
        functools.partial(_groupnorm_kernel, gd=gd),
        grid=(m // tm,),
        in_specs=[pl.BlockSpec((tm, width), lambda i: (i, cb)),
                  pl.BlockSpec((1, gw), lambda i: (0, 0))],
        out_specs=pl.BlockSpec((tm, width), lambda i: (i, 0)),
        out_shape=jax.ShapeDtypeStruct((m, width), BF16),
        compiler_params=_cparams(("parallel",), 4 * tm * width * 8),
        name="groupnorm",
    )(x, g)


def _mm_kernel(*refs, nk, has_res, has_norm, transpose_out):
    a_ref, w_ref = refs[0], refs[1]
    pos = 2
    res_ref = g_ref = None
    if has_res:
        res_ref = refs[pos]
        pos += 1
    if has_norm:
        g_ref = refs[pos]
        pos += 1
    o_ref = refs[pos]
    acc_ref = refs[pos + 1] if nk > 1 else None

    def epilogue(acc):
        if has_norm:
            ms = jnp.mean(acc * acc, axis=-1, keepdims=True)
            acc = acc * lax.rsqrt(ms + EPS) * g_ref[...]
        if has_res:
            acc = acc + res_ref[...]
        if transpose_out:
            acc = acc.T
        o_ref[...] = acc.astype(o_ref.dtype)

    part = jnp.dot(a_ref[...], w_ref[...], preferred_element_type=F32)
    if nk == 1:
        epilogue(part)
    else:
        k = pl.program_id(2)

        @pl.when(k == 0)
        def _():
            acc_ref[...] = part

        @pl.when(k > 0)
        def _():
            acc_ref[...] += part

        @pl.when(k == nk - 1)
        def _():
            epilogue(acc_ref[...])


def matmul(a, w, *, out_dtype, tm=1024, tn=512, tk=None, residual=None, norm_gain=None, transpose_out=False):
    m, kdim = a.shape
    n = w.shape[1]
    tm = _tile(m, tm)
    tn = _tile(n, tn)
    tk = kdim if tk is None else _tile(kdim, tk)
    nk = kdim // tk
    assert norm_gain is None or norm_gain.shape[0] == tn
    in_specs = [pl.BlockSpec((tm, tk), lambda i, j, k: (i, k)),
                pl.BlockSpec((tk, tn), lambda i, j, k: (k, j))]
    args = [a, w]
    if residual is not None:
        in_specs.append(pl.BlockSpec((tm, tn), lambda i, j, k: (i, j)))
        args.append(residual)
    if norm_gain is not None:
        in_specs.append(pl.BlockSpec((1, tn), lambda i, j, k: (0, 0)))
        args.append(norm_gain.reshape(1, tn).astype(F32))
    if transpose_out:
        out_spec = pl.BlockSpec((tn, tm), lambda i, j, k: (j, i))
        out_shape = jax.ShapeDtypeStruct((n, m), out_dtype)
    else:
        out_spec = pl.BlockSpec((tm, tn), lambda i, j, k: (i, j))
        out_shape = jax.ShapeDtypeStruct((m, n), out_dtype)
    osz = jnp.dtype(out_dtype).itemsize
    vmem = 2 * (tm * tk * 2 + tk * tn * 2 + tm * tn * osz) + tm * tn * 4 * 3
    if residual is not None:
        vmem += 2 * tm * tn * 4
    return pl.pallas_call(
        functools.partial(_mm_kernel, nk=nk, has_res=residual is not None,
                          has_norm=norm_gain is not None, transpose_out=transpose_out),
        grid=(m // tm, n // tn, nk),
        in_specs=in_specs,
        out_specs=out_spec,
        out_shape=out_shape,
        scratch_shapes=[pltpu.VMEM((tm, tn), F32)] if nk > 1 else [],
        compiler_params=_cparams(("parallel", "parallel", "arbitrary"), vmem + 4 * MIB),
        name="matmul",
    )(*args)


HALO = 16


def _ffn_gu_kernel(a_ref, ah_ref, wg_ref, wu_ref, cw_ref, cb_ref, o_ref, *, tm, seq):
    i = pl.program_id(0)
    a = a_ref[...]
    wg = wg_ref[...]
    g = jnp.dot(a, wg, preferred_element_type=F32)
    u = jnp.dot(a, wu_ref[...], preferred_element_type=F32)
    gh = jnp.dot(ah_ref[...], wg, preferred_element_type=F32)
    seq_start = (i * tm) % seq == 0
    gh = jnp.where(seq_start, 0.0, gh)
    rows = lax.broadcasted_iota(I32, g.shape, 0)
    hm1 = gh[HALO - 1:HALO, :]
    hm2 = gh[HALO - 2:HALO - 1, :]
    g1 = jnp.where(rows == 0, hm1, pltpu.roll(g, 1, 0))
    g2 = jnp.where(rows == 0, hm2, jnp.where(rows == 1, hm1, pltpu.roll(g, 2, 0)))
    cw = cw_ref[...]
    c = cw[0:1, :] * g2 + cw[1:2, :] * g1 + cw[2:3, :] * g + cb_ref[...]
    o_ref[...] = (c * jax.nn.sigmoid(c) * u).astype(o_ref.dtype)


def ffn_gate_up(h, wg, wu, conv_w, conv_b, seq):
    m, d = h.shape
    f = wg.shape[1]
    tm = _tile(seq, 1024)
    tn = _tile(f, 256)
    hb = tm // HALO
    vmem = 2 * (tm * d * 2 + HALO * d * 2 + 2 * d * tn * 2 + tm * tn * 2) + 8 * tm * tn * 4
    return pl.pallas_call(
        functools.partial(_ffn_gu_kernel, tm=tm, seq=seq),
        grid=(m // tm, f // tn),
        in_specs=[pl.BlockSpec((tm, d), lambda i, j: (i, 0)),
                  pl.BlockSpec((HALO, d), lambda i, j: (jnp.maximum(i * hb - 1, 0), 0)),
                  pl.BlockSpec((d, tn), lambda i, j: (0, j)),
                  pl.BlockSpec((d, tn), lambda i, j: (0, j)),
                  pl.BlockSpec((CONV_W, tn), lambda i, j: (0, j)),
                  pl.BlockSpec((1, tn), lambda i, j: (0, j))],
        out_specs=pl.BlockSpec((tm, tn), lambda i, j: (i, j)),
        out_shape=jax.ShapeDtypeStruct((m, f), BF16),
        compiler_params=_cparams(("parallel", "parallel"), vmem + 4 * MIB),
        name="ffn_gate_up",
    )(h, h, wg, wu, conv_w, conv_b.reshape(1, f))


def _t5_bucket(dist):
    n = jnp.maximum(dist, 0)
    nf = jnp.maximum(n, 1).astype(F32)
    large = MAX_EXACT + (jnp.log(nf / MAX_EXACT) / math.log(MAX_DISTANCE / MAX_EXACT)
                         * (NUM_BUCKETS - MAX_EXACT)).astype(I32)
    return jnp.where(n < MAX_EXACT, n, jnp.minimum(large, NUM_BUCKETS - 1))


def _near_kinds(tq, tk):
    return -(-(FAR_DIST - 1 + tk) // tq)


def _toeplitz_tiles(table, d0s, tq, tk, transposed):
    n = tq + tk
    j = jnp.arange(n, dtype=I32)
    if transposed:
        rel, rows, cols = jnp.where(j < tq, j, j - n), tk, tq
    else:
        rel, rows, cols = jnp.where(j < tk, -j, n - j), tq, tk
    d0 = jnp.asarray(d0s, I32)[:, None]
    vals = jnp.moveaxis(table.astype(F32)[_t5_bucket(d0 + rel[None, :])], -1, 1)
    x = jnp.tile(vals, (1, 1, rows))[:, :, :rows * (n - 1)]
    return x.reshape(len(d0s), table.shape[1], rows, n - 1)[..., :cols]


def _toeplitz_bias(table, tq, tk, transposed=False):
    kinds = _near_kinds(tq, tk) + 1
    return _toeplitz_tiles(table, [d * tq for d in range(kinds)], tq, tk, transposed)


def _sortable_key(x):
    b = lax.bitcast_convert_type(x, I32)
    return b ^ ((b >> 31) & jnp.int32(0x7FFFFFFF))


def _indexer_kernel(qt_ref, k_ref, wi_ref, sc_ref, thr_ref, key_scr, *, tq, tkc, topk, wscale):
    i = pl.program_id(1)
    nch = (i * tq + tq + tkc - 1) // tkc
    wi = wi_ref[...] * wscale
    sc_ref[...] = jnp.full(sc_ref.shape, -jnp.inf, F32)
    q_idx = i * tq + lax.broadcasted_iota(I32, (tkc, tq), 1)
    k_loc = lax.broadcasted_iota(I32, (tkc, tq), 0)

    def chunk(c, carry):
        off = pl.multiple_of(c * tkc, tkc)
        kc = k_ref[pl.ds(off, tkc), :]
        s = jnp.zeros((tkc, tq), F32)
        for h in range(IDX_HEADS):
            r = jnp.dot(kc, qt_ref[h * IDX_DIM:(h + 1) * IDX_DIM, :], preferred_element_type=F32)
            s = s + jnp.maximum(r, 0.0) * wi[h:h + 1, :]
        s = jnp.where(off + k_loc <= q_idx, s, -jnp.inf)
        sc_ref[pl.ds(off, tkc), :] = s
        key_scr[pl.ds(off, tkc), :] = _sortable_key(s)
        return carry

    lax.fori_loop(0, nch, chunk, 0)

    def bit_step(it, v):
        cand = v + jnp.left_shift(jnp.int32(1), 31 - it)

        def count(c, acc):
            off = pl.multiple_of(c * tkc, tkc)
            hit = (key_scr[pl.ds(off, tkc), :] >= cand).astype(I32)
            return acc + jnp.sum(hit.reshape(tkc // 8, 8, tq), axis=0)

        acc = lax.fori_loop(0, nch, count, jnp.zeros((8, tq), I32))
        cnt = jnp.sum(acc, axis=0, keepdims=True)
        return jnp.where(cnt >= topk, cand, v)

    thr_ref[...] = lax.fori_loop(0, 32, bit_step, jnp.full((1, tq), INT_MIN, I32))


def dsa_indexer(qit, ki, wit, bsz, seq, topk):
    n = bsz * seq
    tq = _tile(seq, 256)
    tkc = _tile(seq, 256)
    nq = seq // tq
    vmem = 2 * (IDX_HEADS * IDX_DIM * tq * 2 + seq * IDX_DIM * 2 + IDX_HEADS * tq * 4 + seq * tq * 4) \
        + seq * tq * 4 + 8 * tkc * tq * 4
    return pl.pallas_call(
        functools.partial(_indexer_kernel, tq=tq, tkc=tkc, topk=topk,
                          wscale=IDX_HEADS ** -0.5 * IDX_DIM ** -0.5),
        grid=(bsz, nq),
        in_specs=[pl.BlockSpec((IDX_HEADS * IDX_DIM, tq), lambda b, i: (0, b * nq + i)),
                  pl.BlockSpec((seq, IDX_DIM), lambda b, i: (b, 0)),
                  pl.BlockSpec((IDX_HEADS, tq), lambda b, i: (0, b * nq + i))],
        out_specs=[pl.BlockSpec((seq, tq), lambda b, i: (b, i)),
                   pl.BlockSpec((1, tq), lambda b, i: (0, b * nq + i))],
        out_shape=[jax.ShapeDtypeStruct((n, seq), F32),
                   jax.ShapeDtypeStruct((1, n), I32)],
        scratch_shapes=[pltpu.VMEM((seq, tq), I32)],
        compiler_params=_cparams(("parallel", "parallel"), vmem + 4 * MIB),
        name="dsa_indexer",
    )(qit, ki, wit)


def _dsa_attn_kernel(qt_ref, c_ref, ct_ref, sc_ref, thr_ref, bias_ref, wuvt_ref, o_ref, m_scr, l_scr, acc_scr,
                     madd_scr, *, tq, tk, nkb, scale):
    i = pl.program_id(1)
    kb = pl.program_id(2)
    kbmax = (i * tq + tq - 1) // tk
    nh = A_HEADS
    rank = A_KV_RANK

    @pl.when(kb == 0)
    def _():
        m_scr[...] = jnp.full(m_scr.shape, M_INIT, F32)
        l_scr[...] = jnp.zeros(l_scr.shape, F32)
        acc_scr[...] = jnp.zeros(acc_scr.shape, F32)

    @pl.when(kb <= kbmax)
    def _():
        key = _sortable_key(sc_ref[...])
        k_idx = kb * tk + lax.broadcasted_iota(I32, (tk, tq), 0)
        q_idx = i * tq + lax.broadcasted_iota(I32, (tk, tq), 1)
        madd_scr[...] = jnp.where(k_idx <= q_idx, jnp.where(key >= thr_ref[...], 0.0, -jnp.inf), -jnp.inf)

        for h in range(nh):
            qh = qt_ref[h * rank:(h + 1) * rank, :]
            s = jnp.dot(c_ref[...], qh, preferred_element_type=F32) * scale + bias_ref[h] + madd_scr[...]
            m_old = m_scr[h]
            m_new = jnp.maximum(m_old, jnp.max(s, axis=0, keepdims=True))
            alpha = jnp.exp(m_old - m_new)
            p = jnp.exp(s - m_new)
            l_scr[h] = alpha * l_scr[h] + jnp.sum(p, axis=0, keepdims=True)
            acc_scr[h] = alpha * acc_scr[h] + jnp.dot(ct_ref[...], p.astype(BF16),
                                                      preferred_element_type=F32)
            m_scr[h] = m_new

    @pl.when(kb == nkb - 1)
    def _():
        for h in range(nh):
            o_lat = (acc_scr[h] / l_scr[h]).astype(BF16)
            out_t = jnp.dot(wuvt_ref[h], o_lat, preferred_element_type=F32)
            o_ref[:, h * A_V_DIM:(h + 1) * A_V_DIM] = out_t.T.astype(o_ref.dtype)


def dsa_attention(qt, c, ct, scores_t, thr, bias_a, w_uvt, bsz, seq):
    n = bsz * seq
    tq = _tile(seq, 256)
    tk = tq
    nq, nkb = seq // tq, seq // tk
    bias = _toeplitz_bias(bias_a, tq, tk, transposed=True)
    kinds = bias.shape[0]
    rank = A_KV_RANK

    def kclamp(i, kb):
        return jnp.minimum(kb, (i * tq + tq - 1) // tk)

    def kind(i, kb):
        return jnp.minimum(i - kclamp(i, kb) * (tk // tq), kinds - 1)

    vmem = 2 * (A_HEADS * rank * tq * 2 + 2 * tk * rank * 2 + tk * tq * 4 + A_HEADS * tk * tq * 4
                + A_HEADS * rank * A_V_DIM * 2 + tq * A_HEADS * A_V_DIM * 2) \
        + A_HEADS * rank * tq * 4 + 10 * tk * tq * 4
    return pl.pallas_call(
        functools.partial(_dsa_attn_kernel, tq=tq, tk=tk, nkb=nkb, scale=rank ** -0.5),
        grid=(bsz, nq, nkb),
        in_specs=[pl.BlockSpec((A_HEADS * rank, tq), lambda b, i, kb: (0, b * nq + i)),
                  pl.BlockSpec((tk, rank), lambda b, i, kb: (b * nkb + kclamp(i, kb), 0)),
                  pl.BlockSpec((rank, tk), lambda b, i, kb: (0, b * nkb + kclamp(i, kb))),
                  pl.BlockSpec((tk, tq), lambda b, i, kb: (b * nkb + kclamp(i, kb), i)),
                  pl.BlockSpec((1, tq), lambda b, i, kb: (0, b * nq + i)),
                  pl.BlockSpec((None, A_HEADS, tk, tq), lambda b, i, kb: (kind(i, kb), 0, 0, 0)),
                  pl.BlockSpec((A_HEADS, A_V_DIM, rank), lambda b, i, kb: (0, 0, 0))],
        out_specs=pl.BlockSpec((tq, A_HEADS * A_V_DIM), lambda b, i, kb: (b * nq + i, 0)),
        out_shape=jax.ShapeDtypeStruct((n, A_HEADS * A_V_DIM), BF16),
        scratch_shapes=[pltpu.VMEM((A_HEADS, 1, tq), F32),
                        pltpu.VMEM((A_HEADS, 1, tq), F32),
                        pltpu.VMEM((A_HEADS, rank, tq), F32),
                        pltpu.VMEM((tk, tq), F32)],
        compiler_params=_cparams(("parallel", "parallel", "arbitrary"), vmem + 4 * MIB),
        name="dsa_attention",
    )(qt, c, ct, scores_t, thr, bias, w_uvt)


def _sb_kernel(q_ref, k_ref, v_ref, u_ref, o_ref, carry_scr, acc_scr, *, tq, scale):
    i = pl.program_id(2)
    q = q_ref[...]
    tri = u_ref[...]
    row = lax.broadcasted_iota(I32, (tq, tq), 0)
    col = lax.broadcasted_iota(I32, (tq, tq), 1)
    strict = col < row

    def step(kb, diag):
        off = pl.multiple_of(kb * tq, tq)
        k = k_ref[pl.ds(off, tq), :]
        v = v_ref[pl.ds(off, tq), :]
        z = lax.dot_general(q, k, (((1,), (1,)), ((), ())), preferred_element_type=F32) * scale
        log1m = -(jnp.maximum(z, 0.0) + jnp.log1p(jnp.exp(-jnp.abs(z))))
        if diag:
            log1m = jnp.where(strict, log1m, 0.0)
        hi = log1m.astype(BF16)
        lo = (log1m - hi.astype(F32)).astype(BF16)
        between = (jnp.dot(hi, tri, preferred_element_type=F32)
                   + jnp.dot(lo, tri, preferred_element_type=F32))
        carry = carry_scr[...]
        w = jnp.exp(z + log1m + between + carry)
        if diag:
            w = jnp.where(strict, w, 0.0)
        acc_scr[...] += jnp.dot(w.astype(BF16), v, preferred_element_type=F32)
        carry = carry + jnp.sum(log1m, axis=-1, keepdims=True)
        carry_scr[...] = carry
        return jnp.max(carry)

    carry_scr[...] = jnp.zeros(carry_scr.shape, F32)
    acc_scr[...] = jnp.zeros(acc_scr.shape, F32)
    cmax = step(i, True)

    def cond(state):
        kb, cmax = state
        return jnp.logical_and(kb >= 0, cmax > SB_SKIP)

    def body(state):
        kb, _ = state
        return kb - 1, step(kb, False)

    lax.while_loop(cond, body, (i - 1, cmax))
    o_ref[...] = acc_scr[...].astype(o_ref.dtype)


def stickbreak_attention(qkv, bsz, seq):
    n = bsz * seq
    tq = _tile(seq, 256)
    nq = seq // tq
    tri = (jnp.arange(tq)[:, None] > jnp.arange(tq)[None, :]).astype(BF16)
    vmem = 2 * (2 * seq * B_DIM * 2 + 2 * tq * B_DIM * 2 + tq * tq * 2) + 12 * tq * tq * 4
    return pl.pallas_call(
        functools.partial(_sb_kernel, tq=tq, scale=B_DIM ** -0.5),
        grid=(bsz, B_HEADS, nq),
        in_specs=[pl.BlockSpec((tq, B_DIM), lambda b, h, i: (b * nq + i, h)),
                  pl.BlockSpec((seq, B_DIM), lambda b, h, i: (b, B_HEADS + h)),
                  pl.BlockSpec((seq, B_DIM), lambda b, h, i: (b, 2 * B_HEADS + h)),
                  pl.BlockSpec((tq, tq), lambda b, h, i: (0, 0))],
        out_specs=pl.BlockSpec((tq, B_DIM), lambda b, h, i: (b * nq + i, h)),
        out_shape=jax.ShapeDtypeStruct((n, B_HEADS * B_DIM), BF16),
        scratch_shapes=[pltpu.VMEM((tq, 1), F32), pltpu.VMEM((tq, B_DIM), F32)],
        compiler_params=_cparams(("parallel", "parallel", "parallel"), vmem + 4 * MIB),
        name="stickbreak_attention",
    )(qkv, qkv, qkv, tri)


def _diff_kernel(qt_ref, k_ref, vt_ref, bias_ref, lam_ref, subg_ref, o_ref, m_scr, l_scr, acc_scr,
                 *, tq, kinds, scale, lambda_init):
    i = pl.program_id(2)
    dq = D_QK_DIM
    lv = lam_ref[...]
    lam = (jnp.exp(jnp.sum(lv[0:1] * lv[1:2], axis=-1, keepdims=True))
           - jnp.exp(jnp.sum(lv[2:3] * lv[3:4], axis=-1, keepdims=True))) + lambda_init
    key_i = lax.broadcasted_iota(I32, (tq, tq), 0)
    qry_i = lax.broadcasted_iota(I32, (tq, tq), 1)
    causal = key_i <= qry_i

    m_scr[...] = jnp.full(m_scr.shape, M_INIT, F32)
    l_scr[...] = jnp.zeros(l_scr.shape, F32)
    acc_scr[...] = jnp.zeros(acc_scr.shape, F32)

    def step(kb, nblk, diag):
        tk = nblk * tq
        off = pl.multiple_of(kb * tq, tq)
        k = k_ref[pl.ds(off, tk), :]
        vt = vt_ref[:, pl.ds(off, tk)]
        bias = jnp.concatenate([bias_ref[jnp.minimum(i - kb - j, kinds - 1)] for j in range(nblk)],
                               axis=0)
        for c in range(2):
            s = jnp.dot(k[:, c * dq:(c + 1) * dq], qt_ref[c * dq:(c + 1) * dq, :],
                        preferred_element_type=F32) * scale + bias
            if diag:
                s = jnp.where(causal, s, -jnp.inf)
            m_old = m_scr[c]
            m_new = jnp.maximum(m_old, jnp.max(s, axis=0, keepdims=True))
            alpha = jnp.exp(m_old - m_new)
            p = jnp.exp(s - m_new)
            l_scr[c] = alpha * l_scr[c] + jnp.sum(p, axis=0, keepdims=True)
            acc_scr[c] = alpha * acc_scr[c] + jnp.dot(vt, p.astype(BF16), preferred_element_type=F32)
            m_scr[c] = m_new

    def far_pair(j, carry):
        step(2 * j, 2, False)
        return carry

    lax.fori_loop(0, i // 2, far_pair, 0)

    @pl.when(i % 2 == 1)
    def _():
        step(i - 1, 1, False)

    step(i, 1, True)
    out = acc_scr[0] / l_scr[0] - lam * (acc_scr[1] / l_scr[1])
    ms = jnp.mean(out * out, axis=0, keepdims=True)
    out = out * lax.rsqrt(ms + EPS) * subg_ref[...] * (1.0 - lambda_init)
    o_ref[...] = out.T.astype(o_ref.dtype)


def diff_attention(dqt, dk, dvt, bias_d, lam_vecs, sub_g, bsz, seq, lambda_init):
    n = bsz * seq
    tq = _tile(seq, 256)
    nq = seq // tq
    bias = _toeplitz_bias(bias_d, tq, tq, transposed=True)
    kinds = bias.shape[0]
    width = 2 * D_QK_DIM
    vmem = 2 * (tq * width * 2 + seq * width * 2 + seq * D_V_DIM * 2 + kinds * tq * tq * 4
                + tq * D_V_DIM * 2) + 2 * tq * D_V_DIM * 4 + 10 * tq * tq * 4
    return pl.pallas_call(
        functools.partial(_diff_kernel, tq=tq, kinds=kinds, scale=D_QK_DIM ** -0.5, lambda_init=lambda_init),
        grid=(bsz, D_HEADS, nq),
        in_specs=[pl.BlockSpec((width, tq), lambda b, h, i: (h, b * nq + i)),
                  pl.BlockSpec((seq, width), lambda b, h, i: (b, h)),
                  pl.BlockSpec((D_V_DIM, seq), lambda b, h, i: (h, b)),
                  pl.BlockSpec((kinds, None, tq, tq), lambda b, h, i: (0, h, 0, 0)),
                  pl.BlockSpec((4, D_QK_DIM), lambda b, h, i: (0, 0)),
                  pl.BlockSpec((D_V_DIM, 1), lambda b, h, i: (0, 0))],
        out_specs=pl.BlockSpec((tq, D_V_DIM), lambda b, h, i: (b * nq + i, h)),
        out_shape=jax.ShapeDtypeStruct((n, D_HEADS * D_V_DIM), BF16),
        scratch_shapes=[pltpu.VMEM((2, 1, tq), F32), pltpu.VMEM((2, 1, tq), F32),
                        pltpu.VMEM((2, D_V_DIM, tq), F32)],
        compiler_params=_cparams(("parallel", "parallel", "parallel"), vmem + 4 * MIB),
        name="diff_attention",
    )(dqt, dk, dvt, bias, lam_vecs, sub_g.reshape(D_V_DIM, 1).astype(F32))


def _swa_kernel(q_ref, kp_ref, ko_ref, vp_ref, vo_ref, bias_ref, sink_ref, o_ref, *, blk, scale):
    i = pl.program_id(1)
    left = lax.broadcasted_iota(I32, (blk, LANES), 1) < C_DIM
    qa = lax.broadcasted_iota(I32, (blk, 2 * blk), 0)
    sa = lax.broadcasted_iota(I32, (blk, 2 * blk), 1)
    rel = qa + blk - sa
    lo = jnp.where(i > 0, 0, blk)
    madd = jnp.where(rel >= 0, jnp.where(rel < WINDOW, jnp.where(sa >= lo, 0.0, -jnp.inf), -jnp.inf), -jnp.inf)
    grp = C_HEADS // C_KV_HEADS
    for g in range(C_KV_HEADS):
        ksl = slice(g * LANES, (g + 1) * LANES)
        kk = jnp.concatenate([kp_ref[:, ksl], ko_ref[:, ksl]], axis=0)
        vv = jnp.concatenate([vp_ref[:, ksl], vo_ref[:, ksl]], axis=0)
        for pr in range(grp // 2):
            cb = g * (grp // 2) + pr
            qp = q_ref[:, cb * LANES:(cb + 1) * LANES]
            outs = []
            for half in range(2):
                h = 2 * cb + half
                qh = jnp.where(left if half == 0 else jnp.logical_not(left), qp, jnp.zeros_like(qp))
                lg = lax.dot_general(qh, kk, (((1,), (1,)), ((), ())), preferred_element_type=F32) * scale
                lg = lg + bias_ref[h] + madd
                sink = sink_ref[h]
                m = jnp.maximum(jnp.max(lg, axis=-1, keepdims=True), sink)
                e = jnp.exp(lg - m)
                p = e / (jnp.sum(e, axis=-1, keepdims=True) + jnp.exp(sink - m))
                outs.append(jnp.dot(p.astype(BF16), vv, preferred_element_type=F32))
            o_ref[:, cb * LANES:(cb + 1) * LANES] = jnp.where(left, outs[0], outs[1]).astype(o_ref.dtype)


def swa_attention(cq, ck2, cv2, bias_c, sinks, bsz, seq):
    n = bsz * seq
    blk = WINDOW
    nb = seq // blk
    bias = _toeplitz_tiles(bias_c, [blk], blk, 2 * blk, False)[0]
    qw = C_HEADS * C_DIM
    kw = C_KV_HEADS * LANES
    own = lambda b, i: (b * nb + i, 0)
    prev = lambda b, i: (b * nb + jnp.maximum(i - 1, 0), 0)
    vmem = 2 * (2 * blk * qw * 2 + 4 * blk * kw * 2 + C_HEADS * blk * 2 * blk * 4) + 16 * blk * 2 * blk * 4
    return pl.pallas_call(
        functools.partial(_swa_kernel, blk=blk, scale=C_DIM ** -0.5),
        grid=(bsz, nb),
        in_specs=[pl.BlockSpec((blk, qw), own),
                  pl.BlockSpec((blk, kw), prev), pl.BlockSpec((blk, kw), own),
                  pl.BlockSpec((blk, kw), prev), pl.BlockSpec((blk, kw), own),
                  pl.BlockSpec((C_HEADS, blk, 2 * blk), lambda b, i: (0, 0, 0)),
                  pl.BlockSpec(memory_space=pltpu.SMEM)],
        out_specs=pl.BlockSpec((blk, qw), own),
        out_shape=jax.ShapeDtypeStruct((n, qw), BF16),
        compiler_params=_cparams(("parallel", "parallel"), vmem + 4 * MIB),
        name="swa_attention",
    )(cq, ck2, ck2, cv2, cv2, bias, sinks.astype(F32))


def _xattn_kernel(q_ref, k_ref, v_ref, qg_ref, kg_ref, o_ref, *, scale):
    for h in range(X_HEADS):
        sl = slice(h * X_DIM, (h + 1) * X_DIM)
        q = q_ref[:, sl]
        k = k_ref[:, sl]
        qn = (q * lax.rsqrt(jnp.mean(q * q, axis=-1, keepdims=True) + EPS) * qg_ref[...]).astype(BF16)
        kn = (k * lax.rsqrt(jnp.mean(k * k, axis=-1, keepdims=True) + EPS) * kg_ref[...]).astype(BF16)
        lg = lax.dot_general(qn, kn, (((1,), (1,)), ((), ())), preferred_element_type=F32) * scale
        m = jnp.max(lg, axis=-1, keepdims=True)
        e = jnp.exp(lg - m)
        p = (e / jnp.sum(e, axis=-1, keepdims=True)).astype(BF16)
        o_ref[:, sl] = jnp.dot(p, v_ref[:, sl], preferred_element_type=F32).astype(o_ref.dtype)


def cross_attention(qx, kx, vx, q_g, k_g, bsz, seq, mem_len):
    n = bsz * seq
    tq = _tile(seq, 512)
    nq = seq // tq
    w = X_HEADS * X_DIM
    vmem = 2 * (tq * w * 4 + mem_len * w * 6 + tq * w * 2) + 8 * tq * mem_len * 4
    return pl.pallas_call(
        functools.partial(_xattn_kernel, scale=X_DIM ** -0.5),
        grid=(bsz, nq),
        in_specs=[pl.BlockSpec((tq, w), lambda b, i: (b * nq + i, 0)),
                  pl.BlockSpec((mem_len, w), lambda b, i: (b, 0)),
                  pl.BlockSpec((mem_len, w), lambda b, i: (b, 0)),
                  pl.BlockSpec((1, X_DIM), lambda b, i: (0, 0)),
                  pl.BlockSpec((1, X_DIM), lambda b, i: (0, 0))],
        out_specs=pl.BlockSpec((tq, w), lambda b, i: (b * nq + i, 0)),
        out_shape=jax.ShapeDtypeStruct((n, w), BF16),
        compiler_params=_cparams(("parallel", "parallel"), vmem + 4 * MIB),
        name="cross_attention",
    )(qx, kx, vx, q_g.reshape(1, X_DIM).astype(F32), k_g.reshape(1, X_DIM).astype(F32))


def _even_mixer(xf, h, bsz, seq, w_in, q_lat_g, w_qb, q_g, kv_g, w_uv, w_qi, kidx_g, w_out, rel_bias):
    d = xf.shape[1]
    a_cols = A_Q_RANK + A_KV_RANK + IDX_DIM
    w_a = jnp.concatenate([w_in[:, :a_cols + IDX_HEADS],
                           jnp.zeros((d, LANES - IDX_HEADS), w_in.dtype)], axis=1).astype(BF16)
    w_b = w_in[:, a_cols + IDX_HEADS:].astype(BF16)
    pa = matmul(h, w_a, out_dtype=F32, tn=256)
    qkv = matmul(h, w_b, out_dtype=BF16, tn=512)

    cq = groupnorm(pa, 0, A_Q_RANK, q_lat_g, A_Q_RANK)
    c = groupnorm(pa, A_Q_RANK, A_KV_RANK, kv_g, A_KV_RANK)
    ki = groupnorm(pa, A_Q_RANK + A_KV_RANK, IDX_DIM, kidx_g, IDX_DIM)
    qt = matmul(cq, w_qb.reshape(A_Q_RANK, A_HEADS * A_KV_RANK).astype(BF16), out_dtype=BF16,
                tn=A_KV_RANK, norm_gain=q_g, transpose_out=True)
    qit = matmul(cq, w_qi.reshape(A_Q_RANK, IDX_HEADS * IDX_DIM).astype(BF16), out_dtype=BF16,
                 tn=512, transpose_out=True)
    wit = pa[:, a_cols:a_cols + IDX_HEADS].T
    topk = min(TOPK_MAX, seq // 4)
    scores_t, thr = dsa_indexer(qit, ki, wit, bsz, seq, topk)
    o_a = dsa_attention(qt, c, c.T, scores_t, thr, rel_bias[:, BIAS_A_OFF:BIAS_A_OFF + A_HEADS],
                        jnp.swapaxes(w_uv, 1, 2).astype(BF16), bsz, seq)
    o_b = stickbreak_attention(qkv, bsz, seq)
    o = jnp.concatenate([o_a, o_b], axis=-1)
    return matmul(o, w_out.astype(BF16), out_dtype=F32, tn=1024, residual=xf)


def _odd_mixer(xf, h, bsz, seq, w_in, c_q_g, c_k_g, sinks, d_q_g, d_k_g, lam_q1, lam_k1, lam_q2, lam_k2,
               sub_g, w_out, rel_bias, lambda_init):
    n = xf.shape[0]
    cw = C_HEADS * C_DIM
    ckw = C_KV_HEADS * C_DIM
    dw = D_HEADS * 2 * D_QK_DIM
    c_end = cw + 2 * ckw
    p1 = matmul(h, w_in[:, :c_end].astype(BF16), out_dtype=F32, tn=512)
    p2 = matmul(h, w_in[:, c_end:c_end + 2 * dw].astype(BF16), out_dtype=F32, tn=512)
    dv = matmul(h, w_in[:, c_end + 2 * dw:].astype(BF16), out_dtype=BF16, tn=512)

    cq = groupnorm(p1, 0, cw, c_q_g, C_DIM)
    ck = groupnorm(p1, cw, ckw, c_k_g, C_DIM)
    cv = p1[:, cw + ckw:].astype(BF16)
    dup = lambda a: jnp.concatenate([a.reshape(n, C_KV_HEADS, C_DIM)] * 2, axis=-1).reshape(n, C_KV_HEADS * LANES)
    o_c = swa_attention(cq, dup(ck), dup(cv), rel_bias[:, BIAS_C_OFF:BIAS_C_OFF + C_HEADS], sinks, bsz, seq)

    dq = groupnorm(p2, 0, dw, d_q_g, D_QK_DIM)
    dk = groupnorm(p2, dw, dw, d_k_g, D_QK_DIM)
    lam_vecs = jnp.stack([lam_q1, lam_k1, lam_q2, lam_k2]).astype(F32)
    o_d = diff_attention(dq.T, dk, dv.T, rel_bias[:, BIAS_D_OFF:BIAS_D_OFF + D_HEADS], lam_vecs, sub_g,
                         bsz, seq, lambda_init)
    o = jnp.concatenate([o_c, o_d], axis=-1)
    return matmul(o, w_out.astype(BF16), out_dtype=F32, tn=1024, residual=xf)


def _cross_block(xf, memn, norm_g, wq, wk, wv, q_g, k_g, wo, bsz, seq, mem_len):
    d = xf.shape[1]
    w = X_HEADS * X_DIM
    h = rmsnorm_rows(xf, norm_g)
    qx = matmul(h, wq.reshape(d, w).astype(BF16), out_dtype=F32, tn=512)
    kx = matmul(memn, wk.reshape(d, w).astype(BF16), out_dtype=F32, tn=512)
    vx = matmul(memn, wv.reshape(d, w).astype(BF16), out_dtype=BF16, tn=512)
    ox = cross_attention(qx, kx, vx, q_g, k_g, bsz, seq, mem_len)
    return matmul(ox, wo.astype(BF16), out_dtype=F32, tn=1024, residual=xf)


def _ffn_block(xf, norm_g, w_gate, w_up, conv_w, conv_b, w_down, seq):
    f = w_gate.shape[1]
    fp = -(-f // 1024) * 1024
    pad = fp - f
    h = rmsnorm_rows(xf, norm_g)
    wg = jnp.pad(w_gate, ((0, 0), (0, pad))).astype(BF16)
    wu = jnp.pad(w_up, ((0, 0), (0, pad))).astype(BF16)
    wd = jnp.pad(w_down, ((0, pad), (0, 0))).astype(BF16)
    cw = jnp.pad(conv_w, ((0, 0), (0, pad))).astype(F32)
    cb = jnp.pad(conv_b, ((0, pad),)).astype(F32)
    act = ffn_gate_up(h, wg, wu, cw, cb, seq)
    return matmul(act, wd, out_dtype=F32, tm=1024, tn=1024, tk=fp // 4, residual=xf)


def kernel(x, mem, rel_bias, mem_norm_g, mix_norm_g, xattn_norm_g, ffn_norm_g, ev_w_in, ev_q_lat_g, ev_w_qb, ev_q_g, ev_kv_g, ev_w_uv, ev_w_qi, ev_kidx_g, ev_w_out, od_w_in, od_c_q_g, od_c_k_g, od_sinks, od_d_q_g, od_d_k_g, od_lam_q1, od_lam_k1, od_lam_q2, od_lam_k2, od_sub_g, od_w_out, x_wq, x_wk, x_wv, x_q_g, x_k_g, x_wo, f_w_gate, f_w_up, f_conv_w, f_conv_b, f_w_down):
    bsz, seq, d = x.shape
    mem_len = mem.shape[1]
    depth = mix_norm_g.shape[0]
    xf = x.reshape(bsz * seq, d)
    memn = rmsnorm_rows(mem.reshape(bsz * mem_len, d), mem_norm_g)
    for l in range(depth):
        h = rmsnorm_rows(xf, mix_norm_g[l])
        if l % 2 == 0:
            e = l // 2
            xf = _even_mixer(xf, h, bsz, seq, ev_w_in[e], ev_q_lat_g[e], ev_w_qb[e], ev_q_g[e], ev_kv_g[e],
                             ev_w_uv[e], ev_w_qi[e], ev_kidx_g[e], ev_w_out[e], rel_bias)
        else:
            o = l // 2
            lambda_init = 0.8 - 0.6 * math.exp(-0.3 * l)
            xf = _odd_mixer(xf, h, bsz, seq, od_w_in[o], od_c_q_g[o], od_c_k_g[o], od_sinks[o], od_d_q_g[o],
                            od_d_k_g[o], od_lam_q1[o], od_lam_k1[o], od_lam_q2[o], od_lam_k2[o], od_sub_g[o],
                            od_w_out[o], rel_bias, lambda_init)
        xf = _cross_block(xf, memn, xattn_norm_g[l], x_wq[l], x_wk[l], x_wv[l], x_q_g[l], x_k_g[l], x_wo[l],
                          bsz, seq, mem_len)
        xf = _ffn_block(xf, ffn_norm_g[l], f_w_gate[l], f_w_up[l], f_conv_w[l], f_conv_b[l], f_w_down[l], seq)
    return xf.reshape(bsz, seq, d)
```

```python
import functools
import math

import jax
import jax.numpy as jnp
from jax import lax
from jax.experimental import pallas as pl
from jax.experimental.pallas import tpu as pltpu

F32 = jnp.float32
BF16 = jnp.bfloat16
I32 = jnp.int32

EPS = 1e-6
LANES = 128
MIB = 1024 * 1024
VMEM_CAP = 58 * MIB
M_INIT = -1e30
INT_MIN = -(2 ** 31)

A_HEADS, A_Q_RANK, A_KV_RANK, A_V_DIM = 16, 1024, 512, 128
IDX_HEADS, IDX_DIM, TOPK_MAX = 32, 128, 256
B_HEADS, B_DIM = 16, 128
C_HEADS, C_KV_HEADS, C_DIM, WINDOW = 32, 4, 64, 128
D_HEADS, D_QK_DIM, D_V_DIM = 8, 128, 256
X_HEADS, X_DIM = 4, 128
CONV_W = 3
NUM_BUCKETS, MAX_EXACT, MAX_DISTANCE = 32, 16, 128
FAR_DIST = 113
BIAS_A_OFF, BIAS_C_OFF, BIAS_D_OFF = 0, A_HEADS, A_HEADS + C_HEADS
SB_SKIP = -104.0


def _cparams(sem, vmem_bytes):
    return pltpu.CompilerParams(dimension_semantics=sem,
                                vmem_limit_bytes=int(min(max(vmem_bytes, 16 * MIB), VMEM_CAP)))


def _tile(n, pref):
    t = min(n, pref)
    while n % t:
        t //= 2
    return t


def _rmsnorm_kernel(x_ref, g_ref, o_ref):
    x = x_ref[...]
    ms = jnp.mean(x * x, axis=-1, keepdims=True)
    o_ref[...] = (x * lax.rsqrt(ms + EPS) * g_ref[...]).astype(o_ref.dtype)


def rmsnorm_rows(x, g):
    m, d = x.shape
    tm = _tile(m, 256)
    return pl.pallas_call(
        _rmsnorm_kernel,
        grid=(m // tm,),
        in_specs=[pl.BlockSpec((tm, d), lambda i: (i, 0)),
                  pl.BlockSpec((1, d), lambda i: (0, 0))],
        out_specs=pl.BlockSpec((tm, d), lambda i: (i, 0)),
        out_shape=jax.ShapeDtypeStruct((m, d), BF16),
        compiler_params=_cparams(("parallel",), 4 * tm * d * 6),
        name="rmsnorm_rows",
    )(x, g.reshape(1, d).astype(F32))


def _groupnorm_kernel(x_ref, g_ref, *o_refs, gd, layouts):
    x = x_ref[...].astype(F32)
    width = x.shape[-1]
    g = g_ref[...]
    parts = []
    if gd >= LANES:
        for j in range(width // gd):
            xs = x[:, j * gd:(j + 1) * gd]
            ms = jnp.mean(xs * xs, axis=-1, keepdims=True)
            parts.append(xs * lax.rsqrt(ms + EPS) * g)
    else:
        left = lax.broadcasted_iota(I32, (x.shape[0], LANES), 1) < gd
        for j in range(width // LANES):
            xs = x[:, j * LANES:(j + 1) * LANES]
            sq = xs * xs
            tot = jnp.sum(sq, axis=-1, keepdims=True)
            lsum = jnp.sum(jnp.where(left, sq, 0.0), axis=-1, keepdims=True)
            ms = jnp.where(left, lsum, tot - lsum) * (1.0 / gd)
            parts.append(xs * lax.rsqrt(ms + EPS) * g)
    pw = parts[0].shape[-1]
    for o_ref, transposed in zip(o_refs, layouts):
        for j, y in enumerate(parts):
            if transposed:
                o_ref[j * pw:(j + 1) * pw, :] = y.T.astype(o_ref.dtype)
            else:
                o_ref[:, j * pw:(j + 1) * pw] = y.astype(o_ref.dtype)


def groupnorm(x, col0, width, gain, gd, layouts=(False,)):
    m = x.shape[0]
    assert col0 % width == 0 and width % gd == 0 and (gd % LANES == 0 or 2 * gd == LANES)
    tm = _tile(m, 256)
    gw = gd if gd >= LANES else LANES
    g = jnp.tile(gain.astype(F32), gw // gd).reshape(1, gw)
    cb = col0 // width
    out_specs = [pl.BlockSpec((width, tm), lambda i: (0, i)) if t else pl.BlockSpec((tm, width), lambda i: (i, 0))
                 for t in layouts]
    out_shape = [jax.ShapeDtypeStruct((width, m) if t else (m, width), BF16) for t in layouts]
    outs = pl.pallas_call(
        functools.partial(_groupnorm_kernel, gd=gd, layouts=tuple(layouts)),
        grid=(m // tm,),
        in_specs=[pl.BlockSpec((tm, width), lambda i: (i, cb)),
                  pl.BlockSpec((1, gw), lambda i: (0, 0))],
        out_specs=out_specs,
        out_shape=out_shape,
        compiler_params=_cparams(("parallel",), 4 * tm * width * (8 + 2 * len(layouts))),
        name="groupnorm",
    )(x, g)
    return outs[0] if len(layouts) == 1 else outs


def _mm_kernel(*refs, nk, has_res, has_norm, transpose_out):
    a_ref, w_ref = refs[0], refs[1]
    pos = 2
    res_ref = g_ref = None
    if has_res:
        res_ref = refs[pos]
        pos += 1
    if has_norm:
        g_ref = refs[pos]
        pos += 1
    o_ref = refs[pos]
    acc_ref = refs[pos + 1] if nk > 1 else None

    def epilogue(acc):
        if has_norm:
            ms = jnp.mean(acc * acc, axis=-1, keepdims=True)
            acc = acc * lax.rsqrt(ms + EPS) * g_ref[...]
        if has_res:
            acc = acc + res_ref[...]
        if transpose_out:
            acc = acc.T
        o_ref[...] = acc.astype(o_ref.dtype)

    part = jnp.dot(a_ref[...], w_ref[...], preferred_element_type=F32)
    if nk == 1:
        epilogue(part)
    else:
        k = pl.program_id(2)

        @pl.when(k == 0)
        def _():
            acc_ref[...] = part

        @pl.when(k > 0)
        def _():
            acc_ref[...] += part

        @pl.when(k == nk - 1)
        def _():
            epilogue(acc_ref[...])


def matmul(a, w, *, out_dtype, tm=1024, tn=512, tk=None, residual=None, norm_gain=None, transpose_out=False):
    m, kdim = a.shape
    n = w.shape[1]
    tm = _tile(m, tm)
    tn = _tile(n, tn)
    tk = kdim if tk is None else _tile(kdim, tk)
    nk = kdim // tk
    assert norm_gain is None or norm_gain.shape[0] == tn
    in_specs = [pl.BlockSpec((tm, tk), lambda i, j, k: (i, k)),
                pl.BlockSpec((tk, tn), lambda i, j, k: (k, j))]
    args = [a, w]
    if residual is not None:
        in_specs.append(pl.BlockSpec((tm, tn), lambda i, j, k: (i, j)))
        args.append(residual)
    if norm_gain is not None:
        in_specs.append(pl.BlockSpec((1, tn), lambda i, j, k: (0, 0)))
        args.append(norm_gain.reshape(1, tn).astype(F32))
    if transpose_out:
        out_spec = pl.BlockSpec((tn, tm), lambda i, j, k: (j, i))
        out_shape = jax.ShapeDtypeStruct((n, m), out_dtype)
    else:
        out_spec = pl.BlockSpec((tm, tn), lambda i, j, k: (i, j))
        out_shape = jax.ShapeDtypeStruct((m, n), out_dtype)
    osz = jnp.dtype(out_dtype).itemsize
    vmem = 2 * (tm * tk * 2 + tk * tn * 2 + tm * tn * osz) + tm * tn * 4 * 3
    if residual is not None:
        vmem += 2 * tm * tn * 4
    return pl.pallas_call(
        functools.partial(_mm_kernel, nk=nk, has_res=residual is not None,
                          has_norm=norm_gain is not None, transpose_out=transpose_out),
        grid=(m // tm, n // tn, nk),
        in_specs=in_specs,
        out_specs=out_spec,
        out_shape=out_shape,
        scratch_shapes=[pltpu.VMEM((tm, tn), F32)] if nk > 1 else [],
        compiler_params=_cparams(("parallel", "parallel", "arbitrary"), vmem + 4 * MIB),
        name="matmul",
    )(*args)


HALO = 16


def _ffn_gu_kernel(a_ref, ah_ref, wg_ref, wu_ref, cw_ref, cb_ref, o_ref, *, tm, seq):
    i = pl.program_id(0)
    a = a_ref[...]
    wg = wg_ref[...].astype(BF16)
    g = jnp.dot(a, wg, preferred_element_type=F32)
    u = jnp.dot(a, wu_ref[...].astype(BF16), preferred_element_type=F32)
    gh = jnp.dot(ah_ref[...], wg, preferred_element_type=F32)
    seq_start = (i * tm) % seq == 0
    gh = jnp.where(seq_start, 0.0, gh)
    rows = lax.broadcasted_iota(I32, g.shape, 0)
    hm1 = gh[HALO - 1:HALO, :]
    hm2 = gh[HALO - 2:HALO - 1, :]
    g1 = jnp.where(rows == 0, hm1, pltpu.roll(g, 1, 0))
    g2 = jnp.where(rows == 0, hm2, jnp.where(rows == 1, hm1, pltpu.roll(g, 2, 0)))
    cw = cw_ref[...]
    c = cw[0:1, :] * g2 + cw[1:2, :] * g1 + cw[2:3, :] * g + cb_ref[...]
    o_ref[...] = (c * jax.nn.sigmoid(c) * u).astype(o_ref.dtype)


def ffn_gate_up(h, wg, wu, conv_w, conv_b, seq):
    m, d = h.shape
    f = wg.shape[1]
    tm = _tile(seq, 1024)
    tn = _tile(f, 256)
    hb = tm // HALO
    vmem = 2 * (tm * d * 2 + HALO * d * 2 + 2 * d * tn * 4 + tm * tn * 2) + 2 * d * tn * 2 + 8 * tm * tn * 4
    return pl.pallas_call(
        functools.partial(_ffn_gu_kernel, tm=tm, seq=seq),
        grid=(m // tm, f // tn),
        in_specs=[pl.BlockSpec((tm, d), lambda i, j: (i, 0)),
                  pl.BlockSpec((HALO, d), lambda i, j: (jnp.maximum(i * hb - 1, 0), 0)),
                  pl.BlockSpec((d, tn), lambda i, j: (0, j)),
                  pl.BlockSpec((d, tn), lambda i, j: (0, j)),
                  pl.BlockSpec((CONV_W, tn), lambda i, j: (0, j)),
                  pl.BlockSpec((1, tn), lambda i, j: (0, j))],
        out_specs=pl.BlockSpec((tm, tn), lambda i, j: (i, j)),
        out_shape=jax.ShapeDtypeStruct((m, f), BF16),
        compiler_params=_cparams(("parallel", "parallel"), vmem + 4 * MIB),
        name="ffn_gate_up",
    )(h, h, wg, wu, conv_w.astype(F32), conv_b.reshape(1, f).astype(F32))


def _t5_bucket(dist):
    n = jnp.maximum(dist, 0)
    nf = jnp.maximum(n, 1).astype(F32)
    large = MAX_EXACT + (jnp.log(nf / MAX_EXACT) / math.log(MAX_DISTANCE / MAX_EXACT)
                         * (NUM_BUCKETS - MAX_EXACT)).astype(I32)
    return jnp.where(n < MAX_EXACT, n, jnp.minimum(large, NUM_BUCKETS - 1))


def _near_kinds(tq, tk):
    return -(-(FAR_DIST - 1 + tk) // tq)


def _toeplitz_tiles(table, d0s, tq, tk, transposed):
    n = tq + tk
    j = jnp.arange(n, dtype=I32)
    if transposed:
        rel, rows, cols = jnp.where(j < tq, j, j - n), tk, tq
    else:
        rel, rows, cols = jnp.where(j < tk, -j, n - j), tq, tk
    d0 = jnp.asarray(d0s, I32)[:, None]
    vals = jnp.moveaxis(table.astype(F32)[_t5_bucket(d0 + rel[None, :])], -1, 1)
    x = jnp.tile(vals, (1, 1, rows))[:, :, :rows * (n - 1)]
    return x.reshape(len(d0s), table.shape[1], rows, n - 1)[..., :cols]


def _toeplitz_bias(table, tq, tk, transposed=False):
    kinds = _near_kinds(tq, tk) + 1
    return _toeplitz_tiles(table, [d * tq for d in range(kinds)], tq, tk, transposed)


def _sortable_key(x):
    b = lax.bitcast_convert_type(x, I32)
    return b ^ ((b >> 31) & jnp.int32(0x7FFFFFFF))


def _indexer_kernel(qt_ref, k_ref, wi_ref, sc_ref, thr_ref, key_scr, *, tq, tkc, topk, wscale):
    i = pl.program_id(1)
    nch = (i * tq + tq + tkc - 1) // tkc
    wi = wi_ref[...] * wscale
    sc_ref[...] = jnp.full(sc_ref.shape, -jnp.inf, F32)
    q_idx = i * tq + lax.broadcasted_iota(I32, (tkc, tq), 1)
    k_loc = lax.broadcasted_iota(I32, (tkc, tq), 0)

    def chunk(c, carry):
        off = pl.multiple_of(c * tkc, tkc)
        kc = k_ref[pl.ds(off, tkc), :]
        s = jnp.zeros((tkc, tq), F32)
        for h in range(IDX_HEADS):
            r = jnp.dot(kc, qt_ref[h * IDX_DIM:(h + 1) * IDX_DIM, :], preferred_element_type=F32)
            s = s + jnp.maximum(r, 0.0) * wi[h:h + 1, :]
        s = jnp.where(off + k_loc <= q_idx, s, -jnp.inf)
        sc_ref[pl.ds(off, tkc), :] = s
        key_scr[pl.ds(off, tkc), :] = _sortable_key(s)
        return carry

    lax.fori_loop(0, nch, chunk, 0)

    def bit_step(it, v):
        cand = v + jnp.left_shift(jnp.int32(1), 31 - it)

        def count(c, acc):
            off = pl.multiple_of(c * tkc, tkc)
            hit = (key_scr[pl.ds(off, tkc), :] >= cand).astype(I32)
            return acc + jnp.sum(hit.reshape(tkc // 8, 8, tq), axis=0)

        acc = lax.fori_loop(0, nch, count, jnp.zeros((8, tq), I32))
        cnt = jnp.sum(acc, axis=0, keepdims=True)
        return jnp.where(cnt >= topk, cand, v)

    thr_ref[...] = lax.fori_loop(0, 32, bit_step, jnp.full((1, tq), INT_MIN, I32))


def dsa_indexer(qit, ki, wit, bsz, seq, topk):
    n = bsz * seq
    tq = _tile(seq, 256)
    tkc = _tile(seq, 256)
    nq = seq // tq
    vmem = 2 * (IDX_HEADS * IDX_DIM * tq * 2 + seq * IDX_DIM * 2 + IDX_HEADS * tq * 4 + seq * tq * 4) \
        + seq * tq * 4 + 8 * tkc * tq * 4
    return pl.pallas_call(
        functools.partial(_indexer_kernel, tq=tq, tkc=tkc, topk=topk,
                          wscale=IDX_HEADS ** -0.5 * IDX_DIM ** -0.5),
        grid=(bsz, nq),
        in_specs=[pl.BlockSpec((IDX_HEADS * IDX_DIM, tq), lambda b, i: (0, b * nq + i)),
                  pl.BlockSpec((seq, IDX_DIM), lambda b, i: (b, 0)),
                  pl.BlockSpec((IDX_HEADS, tq), lambda b, i: (0, b * nq + i))],
        out_specs=[pl.BlockSpec((seq, tq), lambda b, i: (b, i)),
                   pl.BlockSpec((1, tq), lambda b, i: (0, b * nq + i))],
        out_shape=[jax.ShapeDtypeStruct((n, seq), F32),
                   jax.ShapeDtypeStruct((1, n), I32)],
        scratch_shapes=[pltpu.VMEM((seq, tq), I32)],
        compiler_params=_cparams(("parallel", "parallel"), vmem + 4 * MIB),
        name="dsa_indexer",
    )(qit, ki, wit)


def _dsa_attn_kernel(qt_ref, c_ref, ct_ref, sc_ref, thr_ref, bias_ref, wuvt_ref, o_ref, m_scr, l_scr, acc_scr,
                     madd_scr, *, tq, tk, nkb, scale):
    i = pl.program_id(1)
    kb = pl.program_id(2)
    kbmax = (i * tq + tq - 1) // tk
    nh = A_HEADS
    rank = A_KV_RANK

    @pl.when(kb == 0)
    def _():
        m_scr[...] = jnp.full(m_scr.shape, M_INIT, F32)
        l_scr[...] = jnp.zeros(l_scr.shape, F32)
        acc_scr[...] = jnp.zeros(acc_scr.shape, F32)

    @pl.when(kb <= kbmax)
    def _():
        key = _sortable_key(sc_ref[...])
        k_idx = kb * tk + lax.broadcasted_iota(I32, (tk, tq), 0)
        q_idx = i * tq + lax.broadcasted_iota(I32, (tk, tq), 1)
        madd_scr[...] = jnp.where(k_idx <= q_idx, jnp.where(key >= thr_ref[...], 0.0, -jnp.inf), -jnp.inf)

        for h in range(nh):
            qh = qt_ref[h * rank:(h + 1) * rank, :]
            s = jnp.dot(c_ref[...], qh, preferred_element_type=F32) * scale + bias_ref[h] + madd_scr[...]
            m_old = m_scr[h]
            m_new = jnp.maximum(m_old, jnp.max(s, axis=0, keepdims=True))
            alpha = jnp.exp(m_old - m_new)
            p = jnp.exp(s - m_new)
            l_scr[h] = alpha * l_scr[h] + jnp.sum(p, axis=0, keepdims=True)
            acc_scr[h] = alpha * acc_scr[h] + jnp.dot(ct_ref[...], p.astype(BF16),
                                                      preferred_element_type=F32)
            m_scr[h] = m_new

    @pl.when(kb == nkb - 1)
    def _():
        for h in range(nh):
            o_lat = (acc_scr[h] / l_scr[h]).astype(BF16)
            out_t = jnp.dot(wuvt_ref[h], o_lat, preferred_element_type=F32)
            o_ref[:, h * A_V_DIM:(h + 1) * A_V_DIM] = out_t.T.astype(o_ref.dtype)


def dsa_attention(qt, c, ct, scores_t, thr, bias_a, w_uvt, bsz, seq):
    n = bsz * seq
    tq = _tile(seq, 256)
    tk = tq
    nq, nkb = seq // tq, seq // tk
    bias = _toeplitz_bias(bias_a, tq, tk, transposed=True)
    kinds = bias.shape[0]
    rank = A_KV_RANK

    def kclamp(i, kb):
        return jnp.minimum(kb, (i * tq + tq - 1) // tk)

    def kind(i, kb):
        return jnp.minimum(i - kclamp(i, kb) * (tk // tq), kinds - 1)

    vmem = 2 * (A_HEADS * rank * tq * 2 + 2 * tk * rank * 2 + tk * tq * 4 + A_HEADS * tk * tq * 4
                + A_HEADS * rank * A_V_DIM * 2 + tq * A_HEADS * A_V_DIM * 2) \
        + A_HEADS * rank * tq * 4 + 10 * tk * tq * 4
    return pl.pallas_call(
        functools.partial(_dsa_attn_kernel, tq=tq, tk=tk, nkb=nkb, scale=rank ** -0.5),
        grid=(bsz, nq, nkb),
        in_specs=[pl.BlockSpec((A_HEADS * rank, tq), lambda b, i, kb: (0, b * nq + i)),
                  pl.BlockSpec((tk, rank), lambda b, i, kb: (b * nkb + kclamp(i, kb), 0)),
                  pl.BlockSpec((rank, tk), lambda b, i, kb: (0, b * nkb + kclamp(i, kb))),
                  pl.BlockSpec((tk, tq), lambda b, i, kb: (b * nkb + kclamp(i, kb), i)),
                  pl.BlockSpec((1, tq), lambda b, i, kb: (0, b * nq + i)),
                  pl.BlockSpec((None, A_HEADS, tk, tq), lambda b, i, kb: (kind(i, kb), 0, 0, 0)),
                  pl.BlockSpec((A_HEADS, A_V_DIM, rank), lambda b, i, kb: (0, 0, 0))],
        out_specs=pl.BlockSpec((tq, A_HEADS * A_V_DIM), lambda b, i, kb: (b * nq + i, 0)),
        out_shape=jax.ShapeDtypeStruct((n, A_HEADS * A_V_DIM), BF16),
        scratch_shapes=[pltpu.VMEM((A_HEADS, 1, tq), F32),
                        pltpu.VMEM((A_HEADS, 1, tq), F32),
                        pltpu.VMEM((A_HEADS, rank, tq), F32),
                        pltpu.VMEM((tk, tq), F32)],
        compiler_params=_cparams(("parallel", "parallel", "arbitrary"), vmem + 4 * MIB),
        name="dsa_attention",
    )(qt, c, ct, scores_t, thr, bias, w_uvt)


def _sb_kernel(q_ref, k_ref, v_ref, u_ref, o_ref, carry_scr, acc_scr, *, tq, scale):
    i = pl.program_id(2)
    q = q_ref[...]
    tri = u_ref[...]
    row = lax.broadcasted_iota(I32, (tq, tq), 0)
    col = lax.broadcasted_iota(I32, (tq, tq), 1)
    strict = col < row

    def step(kb, diag):
        off = pl.multiple_of(kb * tq, tq)
        k = k_ref[pl.ds(off, tq), :]
        v = v_ref[pl.ds(off, tq), :]
        z = lax.dot_general(q, k, (((1,), (1,)), ((), ())), preferred_element_type=F32) * scale
        log1m = -(jnp.maximum(z, 0.0) + jnp.log1p(jnp.exp(-jnp.abs(z))))
        if diag:
            log1m = jnp.where(strict, log1m, 0.0)
        hi = log1m.astype(BF16)
        lo = (log1m - hi.astype(F32)).astype(BF16)
        between = (jnp.dot(hi, tri, preferred_element_type=F32)
                   + jnp.dot(lo, tri, preferred_element_type=F32))
        carry = carry_scr[...]
        w = jnp.exp(z + log1m + between + carry)
        if diag:
            w = jnp.where(strict, w, 0.0)
        acc_scr[...] += jnp.dot(w.astype(BF16), v, preferred_element_type=F32)
        carry = carry + jnp.sum(log1m, axis=-1, keepdims=True)
        carry_scr[...] = carry
        return jnp.max(carry)

    carry_scr[...] = jnp.zeros(carry_scr.shape, F32)
    acc_scr[...] = jnp.zeros(acc_scr.shape, F32)
    cmax = step(i, True)

    def cond(state):
        kb, cmax = state
        return jnp.logical_and(kb >= 0, cmax > SB_SKIP)

    def body(state):
        kb, _ = state
        return kb - 1, step(kb, False)

    lax.while_loop(cond, body, (i - 1, cmax))
    o_ref[...] = acc_scr[...].astype(o_ref.dtype)


def stickbreak_attention(qkv, bsz, seq):
    n = bsz * seq
    tq = _tile(seq, 256)
    nq = seq // tq
    tri = (jnp.arange(tq)[:, None] > jnp.arange(tq)[None, :]).astype(BF16)
    vmem = 2 * (2 * seq * B_DIM * 2 + 2 * tq * B_DIM * 2 + tq * tq * 2) + 12 * tq * tq * 4
    return pl.pallas_call(
        functools.partial(_sb_kernel, tq=tq, scale=B_DIM ** -0.5),
        grid=(bsz, B_HEADS, nq),
        in_specs=[pl.BlockSpec((tq, B_DIM), lambda b, h, i: (b * nq + i, h)),
                  pl.BlockSpec((seq, B_DIM), lambda b, h, i: (b, B_HEADS + h)),
                  pl.BlockSpec((seq, B_DIM), lambda b, h, i: (b, 2 * B_HEADS + h)),
                  pl.BlockSpec((tq, tq), lambda b, h, i: (0, 0))],
        out_specs=pl.BlockSpec((tq, B_DIM), lambda b, h, i: (b * nq + i, h)),
        out_shape=jax.ShapeDtypeStruct((n, B_HEADS * B_DIM), BF16),
        scratch_shapes=[pltpu.VMEM((tq, 1), F32), pltpu.VMEM((tq, B_DIM), F32)],
        compiler_params=_cparams(("parallel", "parallel", "parallel"), vmem + 4 * MIB),
        name="stickbreak_attention",
    )(qkv, qkv, qkv, tri)


def _diff_kernel(qt_ref, k_ref, vt_ref, bias_ref, lam_ref, subg_ref, o_ref, m_scr, l_scr, acc_scr,
                 *, tq, kinds, scale, lambda_init):
    i = pl.program_id(2)
    dq = D_QK_DIM
    lv = lam_ref[...]
    lam = (jnp.exp(jnp.sum(lv[0:1] * lv[1:2], axis=-1, keepdims=True))
           - jnp.exp(jnp.sum(lv[2:3] * lv[3:4], axis=-1, keepdims=True))) + lambda_init
    key_i = lax.broadcasted_iota(I32, (tq, tq), 0)
    qry_i = lax.broadcasted_iota(I32, (tq, tq), 1)
    causal = key_i <= qry_i

    m_scr[...] = jnp.full(m_scr.shape, M_INIT, F32)
    l_scr[...] = jnp.zeros(l_scr.shape, F32)
    acc_scr[...] = jnp.zeros(acc_scr.shape, F32)

    def step(kb, nblk, diag):
        tk = nblk * tq
        off = pl.multiple_of(kb * tq, tq)
        k = k_ref[pl.ds(off, tk), :]
        vt = vt_ref[:, pl.ds(off, tk)]
        bias = jnp.concatenate([bias_ref[jnp.minimum(i - kb - j, kinds - 1)] for j in range(nblk)],
                               axis=0)
        for c in range(2):
            s = jnp.dot(k[:, c * dq:(c + 1) * dq], qt_ref[c * dq:(c + 1) * dq, :],
                        preferred_element_type=F32) * scale + bias
            if diag:
                s = jnp.where(causal, s, -jnp.inf)
            m_old = m_scr[c]
            m_new = jnp.maximum(m_old, jnp.max(s, axis=0, keepdims=True))
            alpha = jnp.exp(m_old - m_new)
            p = jnp.exp(s - m_new)
            l_scr[c] = alpha * l_scr[c] + jnp.sum(p, axis=0, keepdims=True)
            acc_scr[c] = alpha * acc_scr[c] + jnp.dot(vt, p.astype(BF16), preferred_element_type=F32)
            m_scr[c] = m_new

    def far_quad(j, carry):
        step(4 * j, 2, False)
        step(4 * j + 2, 2, False)
        return carry

    lax.fori_loop(0, i // 4, far_quad, 0)

    @pl.when(i % 4 >= 2)
    def _():
        step((i // 4) * 4, 2, False)

    @pl.when(i % 2 == 1)
    def _():
        step(i - 1, 1, False)

    step(i, 1, True)
    out = acc_scr[0] / l_scr[0] - lam * (acc_scr[1] / l_scr[1])
    ms = jnp.mean(out * out, axis=0, keepdims=True)
    out = out * lax.rsqrt(ms + EPS) * subg_ref[...] * (1.0 - lambda_init)
    o_ref[...] = out.T.astype(o_ref.dtype)


def diff_attention(dqt, dk, dvt, bias_d, lam_vecs, sub_g, bsz, seq, lambda_init):
    n = bsz * seq
    tq = _tile(seq, 256)
    nq = seq // tq
    bias = _toeplitz_bias(bias_d, tq, tq, transposed=True)
    kinds = bias.shape[0]
    width = 2 * D_QK_DIM
    vmem = 2 * (tq * width * 2 + seq * width * 2 + seq * D_V_DIM * 2 + kinds * tq * tq * 4
                + tq * D_V_DIM * 2) + 2 * tq * D_V_DIM * 4 + 10 * tq * tq * 4
    return pl.pallas_call(
        functools.partial(_diff_kernel, tq=tq, kinds=kinds, scale=D_QK_DIM ** -0.5, lambda_init=lambda_init),
        grid=(bsz, D_HEADS, nq),
        in_specs=[pl.BlockSpec((width, tq), lambda b, h, i: (h, b * nq + i)),
                  pl.BlockSpec((seq, width), lambda b, h, i: (b, h)),
                  pl.BlockSpec((D_V_DIM, seq), lambda b, h, i: (h, b)),
                  pl.BlockSpec((kinds, None, tq, tq), lambda b, h, i: (0, h, 0, 0)),
                  pl.BlockSpec((4, D_QK_DIM), lambda b, h, i: (0, 0)),
                  pl.BlockSpec((D_V_DIM, 1), lambda b, h, i: (0, 0))],
        out_specs=pl.BlockSpec((tq, D_V_DIM), lambda b, h, i: (b * nq + i, h)),
        out_shape=jax.ShapeDtypeStruct((n, D_HEADS * D_V_DIM), BF16),
        scratch_shapes=[pltpu.VMEM((2, 1, tq), F32), pltpu.VMEM((2, 1, tq), F32),
                        pltpu.VMEM((2, D_V_DIM, tq), F32)],
        compiler_params=_cparams(("parallel", "parallel", "parallel"), vmem + 4 * MIB),
        name="diff_attention",
    )(dqt, dk, dvt, bias, lam_vecs, sub_g.reshape(D_V_DIM, 1).astype(F32))


def _swa_kernel(q_ref, kp_ref, ko_ref, vp_ref, vo_ref, bias_ref, sink_ref, o_ref, *, blk, scale):
    i = pl.program_id(1)
    left = lax.broadcasted_iota(I32, (blk, LANES), 1) < C_DIM
    qa = lax.broadcasted_iota(I32, (blk, 2 * blk), 0)
    sa = lax.broadcasted_iota(I32, (blk, 2 * blk), 1)
    rel = qa + blk - sa
    lo = jnp.where(i > 0, 0, blk)
    madd = jnp.where(rel >= 0, jnp.where(rel < WINDOW, jnp.where(sa >= lo, 0.0, -jnp.inf), -jnp.inf), -jnp.inf)
    grp = C_HEADS // C_KV_HEADS
    for g in range(C_KV_HEADS):
        ksl = slice(g * LANES, (g + 1) * LANES)
        kk = jnp.concatenate([kp_ref[:, ksl], ko_ref[:, ksl]], axis=0)
        vv = jnp.concatenate([vp_ref[:, ksl], vo_ref[:, ksl]], axis=0)
        for pr in range(grp // 2):
            cb = g * (grp // 2) + pr
            qp = q_ref[:, cb * LANES:(cb + 1) * LANES]
            outs = []
            for half in range(2):
                h = 2 * cb + half
                qh = jnp.where(left if half == 0 else jnp.logical_not(left), qp, jnp.zeros_like(qp))
                lg = lax.dot_general(qh, kk, (((1,), (1,)), ((), ())), preferred_element_type=F32) * scale
                lg = lg + bias_ref[h] + madd
                sink = sink_ref[h]
                m = jnp.maximum(jnp.max(lg, axis=-1, keepdims=True), sink)
                e = jnp.exp(lg - m)
                p = e / (jnp.sum(e, axis=-1, keepdims=True) + jnp.exp(sink - m))
                outs.append(jnp.dot(p.astype(BF16), vv, preferred_element_type=F32))
            o_ref[:, cb * LANES:(cb + 1) * LANES] = jnp.where(left, outs[0], outs[1]).astype(o_ref.dtype)


def swa_attention(cq, ck2, cv2, bias_c, sinks, bsz, seq):
    n = bsz * seq
    blk = WINDOW
    nb = seq // blk
    bias = _toeplitz_tiles(bias_c, [blk], blk, 2 * blk, False)[0]
    qw = C_HEADS * C_DIM
    kw = C_KV_HEADS * LANES
    own = lambda b, i: (b * nb + i, 0)
    prev = lambda b, i: (b * nb + jnp.maximum(i - 1, 0), 0)
    vmem = 2 * (2 * blk * qw * 2 + 4 * blk * kw * 2 + C_HEADS * blk * 2 * blk * 4) + 16 * blk * 2 * blk * 4
    return pl.pallas_call(
        functools.partial(_swa_kernel, blk=blk, scale=C_DIM ** -0.5),
        grid=(bsz, nb),
        in_specs=[pl.BlockSpec((blk, qw), own),
                  pl.BlockSpec((blk, kw), prev), pl.BlockSpec((blk, kw), own),
                  pl.BlockSpec((blk, kw), prev), pl.BlockSpec((blk, kw), own),
                  pl.BlockSpec((C_HEADS, blk, 2 * blk), lambda b, i: (0, 0, 0)),
                  pl.BlockSpec(memory_space=pltpu.SMEM)],
        out_specs=pl.BlockSpec((blk, qw), own),
        out_shape=jax.ShapeDtypeStruct((n, qw), BF16),
        compiler_params=_cparams(("parallel", "parallel"), vmem + 4 * MIB),
        name="swa_attention",
    )(cq, ck2, ck2, cv2, cv2, bias, sinks.astype(F32))


def _xattn_kernel(q_ref, k_ref, v_ref, qg_ref, kg_ref, o_ref, *, scale):
    for h in range(X_HEADS):
        sl = slice(h * X_DIM, (h + 1) * X_DIM)
        q = q_ref[:, sl]
        k = k_ref[:, sl]
        qn = (q * lax.rsqrt(jnp.mean(q * q, axis=-1, keepdims=True) + EPS) * qg_ref[...]).astype(BF16)
        kn = (k * lax.rsqrt(jnp.mean(k * k, axis=-1, keepdims=True) + EPS) * kg_ref[...]).astype(BF16)
        lg = lax.dot_general(qn, kn, (((1,), (1,)), ((), ())), preferred_element_type=F32) * scale
        m = jnp.max(lg, axis=-1, keepdims=True)
        e = jnp.exp(lg - m)
        p = (e / jnp.sum(e, axis=-1, keepdims=True)).astype(BF16)
        o_ref[:, sl] = jnp.dot(p, v_ref[:, sl], preferred_element_type=F32).astype(o_ref.dtype)


def cross_attention(qx, kx, vx, q_g, k_g, bsz, seq, mem_len):
    n = bsz * seq
    tq = _tile(seq, 512)
    nq = seq // tq
    w = X_HEADS * X_DIM
    vmem = 2 * (tq * w * 4 + mem_len * w * 6 + tq * w * 2) + 8 * tq * mem_len * 4
    return pl.pallas_call(
        functools.partial(_xattn_kernel, scale=X_DIM ** -0.5),
        grid=(bsz, nq),
        in_specs=[pl.BlockSpec((tq, w), lambda b, i: (b * nq + i, 0)),
                  pl.BlockSpec((mem_len, w), lambda b, i: (b, 0)),
                  pl.BlockSpec((mem_len, w), lambda b, i: (b, 0)),
                  pl.BlockSpec((1, X_DIM), lambda b, i: (0, 0)),
                  pl.BlockSpec((1, X_DIM), lambda b, i: (0, 0))],
        out_specs=pl.BlockSpec((tq, w), lambda b, i: (b * nq + i, 0)),
        out_shape=jax.ShapeDtypeStruct((n, w), BF16),
        compiler_params=_cparams(("parallel", "parallel"), vmem + 4 * MIB),
        name="cross_attention",
    )(qx, kx, vx, q_g.reshape(1, X_DIM).astype(F32), k_g.reshape(1, X_DIM).astype(F32))


def _even_mixer(xf, h, bsz, seq, w_in, q_lat_g, w_qb, q_g, kv_g, w_uv, w_qi, kidx_g, w_out, rel_bias):
    d = xf.shape[1]
    a_cols = A_Q_RANK + A_KV_RANK + IDX_DIM
    w_a = jnp.concatenate([w_in[:, :a_cols + IDX_HEADS],
                           jnp.zeros((d, LANES - IDX_HEADS), w_in.dtype)], axis=1).astype(BF16)
    w_b = w_in[:, a_cols + IDX_HEADS:].astype(BF16)
    pa = matmul(h, w_a, out_dtype=F32, tn=256)
    qkv = matmul(h, w_b, out_dtype=BF16, tn=512)

    cq = groupnorm(pa, 0, A_Q_RANK, q_lat_g, A_Q_RANK)
    c, ct = groupnorm(pa, A_Q_RANK, A_KV_RANK, kv_g, A_KV_RANK, layouts=(False, True))
    ki = groupnorm(pa, A_Q_RANK + A_KV_RANK, IDX_DIM, kidx_g, IDX_DIM)
    qt = matmul(cq, w_qb.reshape(A_Q_RANK, A_HEADS * A_KV_RANK).astype(BF16), out_dtype=BF16,
                tn=A_KV_RANK, norm_gain=q_g, transpose_out=True)
    qit = matmul(cq, w_qi.reshape(A_Q_RANK, IDX_HEADS * IDX_DIM).astype(BF16), out_dtype=BF16,
                 tn=512, transpose_out=True)
    wit = pa[:, a_cols:a_cols + IDX_HEADS].T
    topk = min(TOPK_MAX, seq // 4)
    scores_t, thr = dsa_indexer(qit, ki, wit, bsz, seq, topk)
    o_a = dsa_attention(qt, c, ct, scores_t, thr, rel_bias[:, BIAS_A_OFF:BIAS_A_OFF + A_HEADS],
                        jnp.swapaxes(w_uv, 1, 2).astype(BF16), bsz, seq)
    o_b = stickbreak_attention(qkv, bsz, seq)
    o = jnp.concatenate([o_a, o_b], axis=-1)
    return matmul(o, w_out.astype(BF16), out_dtype=F32, tn=1024, residual=xf)


def _odd_mixer(xf, h, bsz, seq, w_in, c_q_g, c_k_g, sinks, d_q_g, d_k_g, lam_q1, lam_k1, lam_q2, lam_k2,
               sub_g, w_out, rel_bias, lambda_init):
    n = xf.shape[0]
    cw = C_HEADS * C_DIM
    ckw = C_KV_HEADS * C_DIM
    dw = D_HEADS * 2 * D_QK_DIM
    c_end = cw + 2 * ckw
    p1 = matmul(h, w_in[:, :c_end].astype(BF16), out_dtype=F32, tn=512)
    p2 = matmul(h, w_in[:, c_end:c_end + 2 * dw].astype(BF16), out_dtype=F32, tn=512)
    dvt = matmul(h, w_in[:, c_end + 2 * dw:].astype(BF16), out_dtype=BF16, tn=512, transpose_out=True)

    cq = groupnorm(p1, 0, cw, c_q_g, C_DIM)
    ck = groupnorm(p1, cw, ckw, c_k_g, C_DIM)
    cv = p1[:, cw + ckw:].astype(BF16)
    dup = lambda a: jnp.concatenate([a.reshape(n, C_KV_HEADS, C_DIM)] * 2, axis=-1).reshape(n, C_KV_HEADS * LANES)
    o_c = swa_attention(cq, dup(ck), dup(cv), rel_bias[:, BIAS_C_OFF:BIAS_C_OFF + C_HEADS], sinks, bsz, seq)

    dqt = groupnorm(p2, 0, dw, d_q_g, D_QK_DIM, layouts=(True,))
    dk = groupnorm(p2, dw, dw, d_k_g, D_QK_DIM)
    lam_vecs = jnp.stack([lam_q1, lam_k1, lam_q2, lam_k2]).astype(F32)
    o_d = diff_attention(dqt, dk, dvt, rel_bias[:, BIAS_D_OFF:BIAS_D_OFF + D_HEADS], lam_vecs, sub_g,
                         bsz, seq, lambda_init)
    o = jnp.concatenate([o_c, o_d], axis=-1)
    return matmul(o, w_out.astype(BF16), out_dtype=F32, tn=1024, residual=xf)


def _cross_block(xf, memn, norm_g, wq, wk, wv, q_g, k_g, wo, bsz, seq, mem_len):
    d = xf.shape[1]
    w = X_HEADS * X_DIM
    h = rmsnorm_rows(xf, norm_g)
    qx = matmul(h, wq.reshape(d, w).astype(BF16), out_dtype=F32, tn=512)
    kx = matmul(memn, wk.reshape(d, w).astype(BF16), out_dtype=F32, tn=512)
    vx = matmul(memn, wv.reshape(d, w).astype(BF16), out_dtype=BF16, tn=512)
    ox = cross_attention(qx, kx, vx, q_g, k_g, bsz, seq, mem_len)
    return matmul(ox, wo.astype(BF16), out_dtype=F32, tn=1024, residual=xf)


def _ffn_block(xf, norm_g, w_gate, w_up, conv_w, conv_b, w_down, seq):
    h = rmsnorm_rows(xf, norm_g)
    act = ffn_gate_up(h, w_gate, w_up, conv_w, conv_b, seq)
    return matmul(act, w_down.astype(BF16), out_dtype=F32, tm=512, tn=512, residual=xf)


def kernel(x, mem, rel_bias, mem_norm_g, mix_norm_g, xattn_norm_g, ffn_norm_g, ev_w_in, ev_q_lat_g, ev_w_qb, ev_q_g, ev_kv_g, ev_w_uv, ev_w_qi, ev_kidx_g, ev_w_out, od_w_in, od_c_q_g, od_c_k_g, od_sinks, od_d_q_g, od_d_k_g, od_lam_q1, od_lam_k1, od_lam_q2, od_lam_k2, od_sub_g, od_w_out, x_wq, x_wk, x_wv, x_q_g, x_k_g, x_wo, f_w_gate, f_w_up, f_conv_w, f_conv_b, f_w_down):
    bsz, seq, d = x.shape
    mem_len = mem.shape[1]
    depth = mix_norm_g.shape[0]
    xf = x.reshape(bsz * seq, d)
    memn = rmsnorm_rows(mem.reshape(bsz * mem_len, d), mem_norm_g)
    for l in range(depth):
        h = rmsnorm_rows(xf, mix_norm_g[l])
        if l % 2 == 0:
            e = l // 2
            xf = _even_mixer(xf, h, bsz, seq, ev_w_in[e], ev_q_lat_g[e], ev_w_qb[e], ev_q_g[e], ev_kv_g[e],
                             ev_w_uv[e], ev_w_qi[e], ev_kidx_g[e], ev_w_out[e], rel_bias)
        else:
            o = l // 2
            lambda_init = 0.8 - 0.6 * math.exp(-0.3 * l)
            xf = _odd_mixer(xf, h, bsz, seq, od_w_in[o], od_c_q_g[o], od_c_k_g[o], od_sinks[o], od_d_q_g[o],
                            od_d_k_g[o], od_lam_q1[o], od_lam_k1[o], od_lam_q2[o], od_lam_k2[o], od_sub_g[o],
                            od_w_out[o], rel_bias, lambda_init)
        xf = _cross_block(xf, memn, xattn_norm_g[l], x_wq[l], x_wk[l], x_wv[l], x_q_g[l], x_k_g[l], x_wo[l],
                          bsz, seq, mem_len)
        xf = _ffn_block(xf, ffn_norm_g[l], f_w_gate[l], f_w_up[l], f_conv_w[l], f_conv_b[l], f_w_down[l], seq)
    return xf.reshape(bsz, seq, d)
```

```python
import functools
import math

import jax
import jax.numpy as jnp
from jax import lax
from jax.experimental import pallas as pl
from jax.experimental.pallas import tpu as pltpu

F32 = jnp.float32
BF16 = jnp.bfloat16
I32 = jnp.int32

EPS = 1e-6
LANES = 128
MIB = 1024 * 1024
VMEM_CAP = 58 * MIB
M_INIT = -1e30
INT_MIN = -(2 ** 31)

A_HEADS, A_Q_RANK, A_KV_RANK, A_V_DIM = 16, 1024, 512, 128
IDX_HEADS, IDX_DIM, TOPK_MAX = 32, 128, 256
B_HEADS, B_DIM = 16, 128
C_HEADS, C_KV_HEADS, C_DIM, WINDOW = 32, 4, 64, 128
D_HEADS, D_QK_DIM, D_V_DIM = 8, 128, 256
X_HEADS, X_DIM = 4, 128
CONV_W = 3
NUM_BUCKETS, MAX_EXACT, MAX_DISTANCE = 32, 16, 128
FAR_DIST = 113
BIAS_A_OFF, BIAS_C_OFF, BIAS_D_OFF = 0, A_HEADS, A_HEADS + C_HEADS
SB_SKIP = -104.0


def _cparams(sem, vmem_bytes):
    return pltpu.CompilerParams(dimension_semantics=sem,
                                vmem_limit_bytes=int(min(max(vmem_bytes, 16 * MIB), VMEM_CAP)))


def _tile(n, pref):
    t = min(n, pref)
    while n % t:
        t //= 2
    return t


def _rmsnorm_kernel(x_ref, g_ref, o_ref):
    x = x_ref[...]
    ms = jnp.mean(x * x, axis=-1, keepdims=True)
    o_ref[...] = (x * lax.rsqrt(ms + EPS) * g_ref[...]).astype(o_ref.dtype)


def rmsnorm_rows(x, g):
    m, d = x.shape
    tm = _tile(m, 256)
    return pl.pallas_call(
        _rmsnorm_kernel,
        grid=(m // tm,),
        in_specs=[pl.BlockSpec((tm, d), lambda i: (i, 0)),
                  pl.BlockSpec((1, d), lambda i: (0, 0))],
        out_specs=pl.BlockSpec((tm, d), lambda i: (i, 0)),
        out_shape=jax.ShapeDtypeStruct((m, d), BF16),
        compiler_params=_cparams(("parallel",), 4 * tm * d * 6),
        name="rmsnorm_rows",
    )(x, g.reshape(1, d).astype(F32))


def _groupnorm_kernel(x_ref, g_ref, *o_refs, gd, layouts):
    x = x_ref[...].astype(F32)
    width = x.shape[-1]
    g = g_ref[...]
    parts = []
    if gd >= LANES:
        for j in range(width // gd):
            xs = x[:, j * gd:(j + 1) * gd]
            ms = jnp.mean(xs * xs, axis=-1, keepdims=True)
            parts.append(xs * lax.rsqrt(ms + EPS) * g)
    else:
        left = lax.broadcasted_iota(I32, (x.shape[0], LANES), 1) < gd
        for j in range(width // LANES):
            xs = x[:, j * LANES:(j + 1) * LANES]
            sq = xs * xs
            tot = jnp.sum(sq, axis=-1, keepdims=True)
            lsum = jnp.sum(jnp.where(left, sq, 0.0), axis=-1, keepdims=True)
            ms = jnp.where(left, lsum, tot - lsum) * (1.0 / gd)
            parts.append(xs * lax.rsqrt(ms + EPS) * g)
    pw = parts[0].shape[-1]
    for o_ref, transposed in zip(o_refs, layouts):
        for j, y in enumerate(parts):
            if transposed:
                o_ref[j * pw:(j + 1) * pw, :] = y.T.astype(o_ref.dtype)
            else:
                o_ref[:, j * pw:(j + 1) * pw] = y.astype(o_ref.dtype)


def groupnorm(x, col0, width, gain, gd, layouts=(False,)):
    m = x.shape[0]
    assert col0 % width == 0 and width % gd == 0 and (gd % LANES == 0 or 2 * gd == LANES)
    tm = _tile(m, 256)
    gw = gd if gd >= LANES else LANES
    g = jnp.tile(gain.astype(F32), gw // gd).reshape(1, gw)
    cb = col0 // width
    out_specs = [pl.BlockSpec((width, tm), lambda i: (0, i)) if t else pl.BlockSpec((tm, width), lambda i: (i, 0))
                 for t in layouts]
    out_shape = [jax.ShapeDtypeStruct((width, m) if t else (m, width), BF16) for t in layouts]
    outs = pl.pallas_call(
        functools.partial(_groupnorm_kernel, gd=gd, layouts=tuple(layouts)),
        grid=(m // tm,),
        in_specs=[pl.BlockSpec((tm, width), lambda i: (i, cb)),
                  pl.BlockSpec((1, gw), lambda i: (0, 0))],
        out_specs=out_specs,
        out_shape=out_shape,
        compiler_params=_cparams(("parallel",), 4 * tm * width * (8 + 2 * len(layouts))),
        name="groupnorm",
    )(x, g)
    return outs[0] if len(layouts) == 1 else outs


def _mm_kernel(*refs, nk, has_a2, has_res, has_norm, transpose_out):
    a_ref, w_ref = refs[0], refs[1]
    pos = 2
    a2_ref = w2_ref = res_ref = g_ref = None
    if has_a2:
        a2_ref, w2_ref = refs[pos], refs[pos + 1]
        pos += 2
    if has_res:
        res_ref = refs[pos]
        pos += 1
    if has_norm:
        g_ref = refs[pos]
        pos += 1
    o_ref = refs[pos]
    acc_ref = refs[pos + 1] if nk > 1 else None

    def epilogue(acc):
        if has_norm:
            ms = jnp.mean(acc * acc, axis=-1, keepdims=True)
            acc = acc * lax.rsqrt(ms + EPS) * g_ref[...]
        if has_res:
            acc = acc + res_ref[...]
        if transpose_out:
            acc = acc.T
        o_ref[...] = acc.astype(o_ref.dtype)

    part = jnp.dot(a_ref[...], w_ref[...], preferred_element_type=F32)
    if has_a2:
        part = part + jnp.dot(a2_ref[...], w2_ref[...], preferred_element_type=F32)
    if nk == 1:
        epilogue(part)
    else:
        k = pl.program_id(2)

        @pl.when(k == 0)
        def _():
            acc_ref[...] = part

        @pl.when(k > 0)
        def _():
            acc_ref[...] += part

        @pl.when(k == nk - 1)
        def _():
            epilogue(acc_ref[...])


def matmul(a, w, *, out_dtype, tm=1024, tn=512, tk=None, col0=0, n=None, a2=None, residual=None,
           norm_gain=None, transpose_out=False):
    m, kdim = a.shape
    n = w.shape[1] if n is None else n
    tm = _tile(m, tm)
    tn = _tile(n, tn)
    tk = kdim if tk is None else _tile(kdim, tk)
    nk = kdim // tk
    assert col0 % tn == 0 and (norm_gain is None or norm_gain.shape[0] == tn)
    assert a2 is None or (nk == 1 and a2.shape == a.shape and w.shape[0] == 2 * kdim)
    cb = col0 // tn
    in_specs = [pl.BlockSpec((tm, tk), lambda i, j, k: (i, k)),
                pl.BlockSpec((tk, tn), lambda i, j, k: (k, j + cb))]
    args = [a, w]
    if a2 is not None:
        in_specs += [pl.BlockSpec((tm, tk), lambda i, j, k: (i, 0)),
                     pl.BlockSpec((tk, tn), lambda i, j, k: (1, j + cb))]
        args += [a2, w]
    if residual is not None:
        in_specs.append(pl.BlockSpec((tm, tn), lambda i, j, k: (i, j)))
        args.append(residual)
    if norm_gain is not None:
        in_specs.append(pl.BlockSpec((1, tn), lambda i, j, k: (0, 0)))
        args.append(norm_gain.reshape(1, tn).astype(F32))
    if transpose_out:
        out_spec = pl.BlockSpec((tn, tm), lambda i, j, k: (j, i))
        out_shape = jax.ShapeDtypeStruct((n, m), out_dtype)
    else:
        out_spec = pl.BlockSpec((tm, tn), lambda i, j, k: (i, j))
        out_shape = jax.ShapeDtypeStruct((m, n), out_dtype)
    osz = jnp.dtype(out_dtype).itemsize
    nops = 2 if a2 is not None else 1
    vmem = 2 * (nops * (tm * tk * 2 + tk * tn * 2) + tm * tn * osz) + tm * tn * 4 * 3
    if residual is not None:
        vmem += 2 * tm * tn * 4
    return pl.pallas_call(
        functools.partial(_mm_kernel, nk=nk, has_a2=a2 is not None, has_res=residual is not None,
                          has_norm=norm_gain is not None, transpose_out=transpose_out),
        grid=(m // tm, n // tn, nk),
        in_specs=in_specs,
        out_specs=out_spec,
        out_shape=out_shape,
        scratch_shapes=[pltpu.VMEM((tm, tn), F32)] if nk > 1 else [],
        compiler_params=_cparams(("parallel", "parallel", "arbitrary"), vmem + 4 * MIB),
        name="matmul",
    )(*args)


HALO = 16


def _ffn_gu_kernel(a_ref, ah_ref, wg_ref, wu_ref, cw_ref, cb_ref, o_ref, *, tm, seq):
    i = pl.program_id(0)
    a = a_ref[...]
    wg = wg_ref[...].astype(BF16)
    g = jnp.dot(a, wg, preferred_element_type=F32)
    u = jnp.dot(a, wu_ref[...].astype(BF16), preferred_element_type=F32)
    gh = jnp.dot(ah_ref[...], wg, preferred_element_type=F32)
    seq_start = (i * tm) % seq == 0
    gh = jnp.where(seq_start, 0.0, gh)
    rows = lax.broadcasted_iota(I32, g.shape, 0)
    hm1 = gh[HALO - 1:HALO, :]
    hm2 = gh[HALO - 2:HALO - 1, :]
    g1 = jnp.where(rows == 0, hm1, pltpu.roll(g, 1, 0))
    g2 = jnp.where(rows == 0, hm2, jnp.where(rows == 1, hm1, pltpu.roll(g, 2, 0)))
    cw = cw_ref[...]
    c = cw[0:1, :] * g2 + cw[1:2, :] * g1 + cw[2:3, :] * g + cb_ref[...]
    o_ref[...] = (c * jax.nn.sigmoid(c) * u).astype(o_ref.dtype)


def ffn_gate_up(h, wg, wu, conv_w, conv_b, seq):
    m, d = h.shape
    f = wg.shape[1]
    tm = _tile(seq, 1024)
    tn = _tile(f, 256)
    hb = tm // HALO
    vmem = 2 * (tm * d * 2 + HALO * d * 2 + 2 * d * tn * 4 + tm * tn * 2) + 2 * d * tn * 2 + 8 * tm * tn * 4
    return pl.pallas_call(
        functools.partial(_ffn_gu_kernel, tm=tm, seq=seq),
        grid=(m // tm, f // tn),
        in_specs=[pl.BlockSpec((tm, d), lambda i, j: (i, 0)),
                  pl.BlockSpec((HALO, d), lambda i, j: (jnp.maximum(i * hb - 1, 0), 0)),
                  pl.BlockSpec((d, tn), lambda i, j: (0, j)),
                  pl.BlockSpec((d, tn), lambda i, j: (0, j)),
                  pl.BlockSpec((CONV_W, tn), lambda i, j: (0, j)),
                  pl.BlockSpec((1, tn), lambda i, j: (0, j))],
        out_specs=pl.BlockSpec((tm, tn), lambda i, j: (i, j)),
        out_shape=jax.ShapeDtypeStruct((m, f), BF16),
        compiler_params=_cparams(("parallel", "parallel"), vmem + 4 * MIB),
        name="ffn_gate_up",
    )(h, h, wg, wu, conv_w.astype(F32), conv_b.reshape(1, f).astype(F32))


def _t5_bucket(dist):
    n = jnp.maximum(dist, 0)
    nf = jnp.maximum(n, 1).astype(F32)
    large = MAX_EXACT + (jnp.log(nf / MAX_EXACT) / math.log(MAX_DISTANCE / MAX_EXACT)
                         * (NUM_BUCKETS - MAX_EXACT)).astype(I32)
    return jnp.where(n < MAX_EXACT, n, jnp.minimum(large, NUM_BUCKETS - 1))


def _near_kinds(tq, tk):
    return -(-(FAR_DIST - 1 + tk) // tq)


def _toeplitz_tiles(table, d0s, tq, tk, transposed):
    n = tq + tk
    j = jnp.arange(n, dtype=I32)
    if transposed:
        rel, rows, cols = jnp.where(j < tq, j, j - n), tk, tq
    else:
        rel, rows, cols = jnp.where(j < tk, -j, n - j), tq, tk
    d0 = jnp.asarray(d0s, I32)[:, None]
    vals = jnp.moveaxis(table.astype(F32)[_t5_bucket(d0 + rel[None, :])], -1, 1)
    x = jnp.tile(vals, (1, 1, rows))[:, :, :rows * (n - 1)]
    return x.reshape(len(d0s), table.shape[1], rows, n - 1)[..., :cols]


def _toeplitz_bias(table, tq, tk, transposed=False):
    kinds = _near_kinds(tq, tk) + 1
    return _toeplitz_tiles(table, [d * tq for d in range(kinds)], tq, tk, transposed)


def _sortable_key(x):
    b = lax.bitcast_convert_type(x, I32)
    return b ^ ((b >> 31) & jnp.int32(0x7FFFFFFF))


def _indexer_kernel(qt_ref, k_ref, wi_ref, sc_ref, thr_ref, key_scr, *, tq, tkc, topk, wscale):
    i = pl.program_id(1)
    nch = (i * tq + tq + tkc - 1) // tkc
    wi = wi_ref[...] * wscale
    sc_ref[...] = jnp.full(sc_ref.shape, -jnp.inf, F32)
    q_idx = i * tq + lax.broadcasted_iota(I32, (tkc, tq), 1)
    k_loc = lax.broadcasted_iota(I32, (tkc, tq), 0)

    def chunk(c, carry):
        off = pl.multiple_of(c * tkc, tkc)
        kc = k_ref[pl.ds(off, tkc), :]
        s = jnp.zeros((tkc, tq), F32)
        for h in range(IDX_HEADS):
            r = jnp.dot(kc, qt_ref[h * IDX_DIM:(h + 1) * IDX_DIM, :], preferred_element_type=F32)
            s = s + jnp.maximum(r, 0.0) * wi[h:h + 1, :]
        s = jnp.where(off + k_loc <= q_idx, s, -jnp.inf)
        sc_ref[pl.ds(off, tkc), :] = s
        key_scr[pl.ds(off, tkc), :] = _sortable_key(s)
        return carry

    lax.fori_loop(0, nch, chunk, 0)

    def bit_step(it, v):
        cand = v + jnp.left_shift(jnp.int32(1), 31 - it)

        def count(c, acc):
            off = pl.multiple_of(c * tkc, tkc)
            hit = (key_scr[pl.ds(off, tkc), :] >= cand).astype(I32)
            return acc + jnp.sum(hit.reshape(tkc // 8, 8, tq), axis=0)

        acc = lax.fori_loop(0, nch, count, jnp.zeros((8, tq), I32))
        cnt = jnp.sum(acc, axis=0, keepdims=True)
        return jnp.where(cnt >= topk, cand, v)

    thr_ref[...] = lax.fori_loop(0, 32, bit_step, jnp.full((1, tq), INT_MIN, I32))


def dsa_indexer(qit, ki, wit, bsz, seq, topk):
    n = bsz * seq
    tq = _tile(seq, 256)
    tkc = _tile(seq, 256)
    nq = seq // tq
    vmem = 2 * (IDX_HEADS * IDX_DIM * tq * 2 + seq * IDX_DIM * 2 + IDX_HEADS * tq * 4 + seq * tq * 4) \
        + seq * tq * 4 + 8 * tkc * tq * 4
    return pl.pallas_call(
        functools.partial(_indexer_kernel, tq=tq, tkc=tkc, topk=topk,
                          wscale=IDX_HEADS ** -0.5 * IDX_DIM ** -0.5),
        grid=(bsz, nq),
        in_specs=[pl.BlockSpec((IDX_HEADS * IDX_DIM, tq), lambda b, i: (0, b * nq + i)),
                  pl.BlockSpec((seq, IDX_DIM), lambda b, i: (b, 0)),
                  pl.BlockSpec((IDX_HEADS, tq), lambda b, i: (0, b * nq + i))],
        out_specs=[pl.BlockSpec((seq, tq), lambda b, i: (b, i)),
                   pl.BlockSpec((1, tq), lambda b, i: (0, b * nq + i))],
        out_shape=[jax.ShapeDtypeStruct((n, seq), F32),
                   jax.ShapeDtypeStruct((1, n), I32)],
        scratch_shapes=[pltpu.VMEM((seq, tq), I32)],
        compiler_params=_cparams(("parallel", "parallel"), vmem + 4 * MIB),
        name="dsa_indexer",
    )(qit, ki, wit)


def _dsa_attn_kernel(qt_ref, c_ref, ct_ref, sc_ref, thr_ref, bias_ref, wuvt_ref, o_ref, m_scr, l_scr, acc_scr,
                     madd_scr, *, tq, tk, nkb, scale):
    i = pl.program_id(1)
    kb = pl.program_id(2)
    kbmax = (i * tq + tq - 1) // tk
    nh = A_HEADS
    rank = A_KV_RANK

    @pl.when(kb == 0)
    def _():
        m_scr[...] = jnp.full(m_scr.shape, M_INIT, F32)
        l_scr[...] = jnp.zeros(l_scr.shape, F32)
        acc_scr[...] = jnp.zeros(acc_scr.shape, F32)

    @pl.when(kb <= kbmax)
    def _():
        key = _sortable_key(sc_ref[...])
        k_idx = kb * tk + lax.broadcasted_iota(I32, (tk, tq), 0)
        q_idx = i * tq + lax.broadcasted_iota(I32, (tk, tq), 1)
        madd_scr[...] = jnp.where(k_idx <= q_idx, jnp.where(key >= thr_ref[...], 0.0, -jnp.inf), -jnp.inf)

        for h in range(nh):
            qh = qt_ref[h * rank:(h + 1) * rank, :]
            s = jnp.dot(c_ref[...], qh, preferred_element_type=F32) * scale + bias_ref[h] + madd_scr[...]
            m_old = m_scr[h]
            m_new = jnp.maximum(m_old, jnp.max(s, axis=0, keepdims=True))
            alpha = jnp.exp(m_old - m_new)
            p = jnp.exp(s - m_new)
            l_scr[h] = alpha * l_scr[h] + jnp.sum(p, axis=0, keepdims=True)
            acc_scr[h] = alpha * acc_scr[h] + jnp.dot(ct_ref[...], p.astype(BF16),
                                                      preferred_element_type=F32)
            m_scr[h] = m_new

    @pl.when(kb == nkb - 1)
    def _():
        for h in range(nh):
            o_lat = (acc_scr[h] / l_scr[h]).astype(BF16)
            out_t = jnp.dot(wuvt_ref[h], o_lat, preferred_element_type=F32)
            o_ref[:, h * A_V_DIM:(h + 1) * A_V_DIM] = out_t.T.astype(o_ref.dtype)


def dsa_attention(qt, c, ct, scores_t, thr, bias_a, w_uvt, bsz, seq):
    n = bsz * seq
    tq = _tile(seq, 256)
    tk = tq
    nq, nkb = seq // tq, seq // tk
    bias = _toeplitz_bias(bias_a, tq, tk, transposed=True)
    kinds = bias.shape[0]
    rank = A_KV_RANK

    def kclamp(i, kb):
        return jnp.minimum(kb, (i * tq + tq - 1) // tk)

    def kind(i, kb):
        return jnp.minimum(i - kclamp(i, kb) * (tk // tq), kinds - 1)

    vmem = 2 * (A_HEADS * rank * tq * 2 + 2 * tk * rank * 2 + tk * tq * 4 + A_HEADS * tk * tq * 4
                + A_HEADS * rank * A_V_DIM * 2 + tq * A_HEADS * A_V_DIM * 2) \
        + A_HEADS * rank * tq * 4 + 10 * tk * tq * 4
    return pl.pallas_call(
        functools.partial(_dsa_attn_kernel, tq=tq, tk=tk, nkb=nkb, scale=rank ** -0.5),
        grid=(bsz, nq, nkb),
        in_specs=[pl.BlockSpec((A_HEADS * rank, tq), lambda b, i, kb: (0, b * nq + i)),
                  pl.BlockSpec((tk, rank), lambda b, i, kb: (b * nkb + kclamp(i, kb), 0)),
                  pl.BlockSpec((rank, tk), lambda b, i, kb: (0, b * nkb + kclamp(i, kb))),
                  pl.BlockSpec((tk, tq), lambda b, i, kb: (b * nkb + kclamp(i, kb), i)),
                  pl.BlockSpec((1, tq), lambda b, i, kb: (0, b * nq + i)),
                  pl.BlockSpec((None, A_HEADS, tk, tq), lambda b, i, kb: (kind(i, kb), 0, 0, 0)),
                  pl.BlockSpec((A_HEADS, A_V_DIM, rank), lambda b, i, kb: (0, 0, 0))],
        out_specs=pl.BlockSpec((tq, A_HEADS * A_V_DIM), lambda b, i, kb: (b * nq + i, 0)),
        out_shape=jax.ShapeDtypeStruct((n, A_HEADS * A_V_DIM), BF16),
        scratch_shapes=[pltpu.VMEM((A_HEADS, 1, tq), F32),
                        pltpu.VMEM((A_HEADS, 1, tq), F32),
                        pltpu.VMEM((A_HEADS, rank, tq), F32),
                        pltpu.VMEM((tk, tq), F32)],
        compiler_params=_cparams(("parallel", "parallel", "arbitrary"), vmem + 4 * MIB),
        name="dsa_attention",
    )(qt, c, ct, scores_t, thr, bias, w_uvt)


def _sb_kernel(q_ref, k_ref, v_ref, u_ref, o_ref, carry_scr, acc_scr, *, tq, scale):
    i = pl.program_id(2)
    q = q_ref[...]
    tri = u_ref[...]
    row = lax.broadcasted_iota(I32, (tq, tq), 0)
    col = lax.broadcasted_iota(I32, (tq, tq), 1)
    strict = col < row

    def block(kb, diag):
        off = pl.multiple_of(kb * tq, tq)
        k = k_ref[pl.ds(off, tq), :]
        z = lax.dot_general(q, k, (((1,), (1,)), ((), ())), preferred_element_type=F32) * scale
        log1m = -(jnp.maximum(z, 0.0) + jnp.log1p(jnp.exp(-jnp.abs(z))))
        if diag:
            log1m = jnp.where(strict, log1m, 0.0)
        hi = log1m.astype(BF16)
        lo = (log1m - hi.astype(F32)).astype(BF16)
        between = (jnp.dot(hi, tri, preferred_element_type=F32)
                   + jnp.dot(lo, tri, preferred_element_type=F32))
        logw = z + log1m + between
        return logw, jnp.sum(log1m, axis=-1, keepdims=True), v_ref[pl.ds(off, tq), :]

    has_prev = i > 0
    lw_d, rs_d, v_d = block(i, True)
    lw_p, rs_p, v_p = block(jnp.maximum(i - 1, 0), False)
    w_d = jnp.where(strict, jnp.exp(lw_d), 0.0)
    w_p = jnp.where(has_prev, jnp.exp(lw_p + rs_d), 0.0)
    acc_scr[...] = (jnp.dot(w_d.astype(BF16), v_d, preferred_element_type=F32)
                    + jnp.dot(w_p.astype(BF16), v_p, preferred_element_type=F32))
    carry = rs_d + jnp.where(has_prev, rs_p, 0.0)
    carry_scr[...] = carry

    def cond(state):
        kb, cmax = state
        return jnp.logical_and(kb >= 0, cmax > SB_SKIP)

    def body(state):
        kb, _ = state
        logw, rs, v = block(kb, False)
        carry = carry_scr[...]
        w = jnp.exp(logw + carry)
        acc_scr[...] += jnp.dot(w.astype(BF16), v, preferred_element_type=F32)
        carry = carry + rs
        carry_scr[...] = carry
        return kb - 1, jnp.max(carry)

    lax.while_loop(cond, body, (i - 2, jnp.max(carry)))
    o_ref[...] = acc_scr[...].astype(o_ref.dtype)


def stickbreak_attention(qkv, bsz, seq):
    n = bsz * seq
    tq = _tile(seq, 256)
    nq = seq // tq
    tri = (jnp.arange(tq)[:, None] > jnp.arange(tq)[None, :]).astype(BF16)
    vmem = 2 * (2 * seq * B_DIM * 2 + 2 * tq * B_DIM * 2 + tq * tq * 2) + 12 * tq * tq * 4
    return pl.pallas_call(
        functools.partial(_sb_kernel, tq=tq, scale=B_DIM ** -0.5),
        grid=(bsz, B_HEADS, nq),
        in_specs=[pl.BlockSpec((tq, B_DIM), lambda b, h, i: (b * nq + i, h)),
                  pl.BlockSpec((seq, B_DIM), lambda b, h, i: (b, B_HEADS + h)),
                  pl.BlockSpec((seq, B_DIM), lambda b, h, i: (b, 2 * B_HEADS + h)),
                  pl.BlockSpec((tq, tq), lambda b, h, i: (0, 0))],
        out_specs=pl.BlockSpec((tq, B_DIM), lambda b, h, i: (b * nq + i, h)),
        out_shape=jax.ShapeDtypeStruct((n, B_HEADS * B_DIM), BF16),
        scratch_shapes=[pltpu.VMEM((tq, 1), F32), pltpu.VMEM((tq, B_DIM), F32)],
        compiler_params=_cparams(("parallel", "parallel", "parallel"), vmem + 4 * MIB),
        name="stickbreak_attention",
    )(qkv, qkv, qkv, tri)


def _diff_kernel(qt_ref, k_ref, vt_ref, bias_ref, lam_ref, subg_ref, o_ref, m_scr, l_scr, acc_scr,
                 *, tq, kinds, scale, lambda_init):
    i = pl.program_id(2)
    dq = D_QK_DIM
    lv = lam_ref[...]
    lam = (jnp.exp(jnp.sum(lv[0:1] * lv[1:2], axis=-1, keepdims=True))
           - jnp.exp(jnp.sum(lv[2:3] * lv[3:4], axis=-1, keepdims=True))) + lambda_init
    key_i = lax.broadcasted_iota(I32, (tq, tq), 0)
    qry_i = lax.broadcasted_iota(I32, (tq, tq), 1)
    causal = key_i <= qry_i

    m_scr[...] = jnp.full(m_scr.shape, M_INIT, F32)
    l_scr[...] = jnp.zeros(l_scr.shape, F32)
    acc_scr[...] = jnp.zeros(acc_scr.shape, F32)

    def step(kb, nblk, diag):
        tk = nblk * tq
        off = pl.multiple_of(kb * tq, tq)
        k = k_ref[pl.ds(off, tk), :]
        vt = vt_ref[:, pl.ds(off, tk)]
        bias = jnp.concatenate([bias_ref[jnp.minimum(i - kb - j, kinds - 1)] for j in range(nblk)],
                               axis=0)
        for c in range(2):
            s = jnp.dot(k[:, c * dq:(c + 1) * dq], qt_ref[c * dq:(c + 1) * dq, :],
                        preferred_element_type=F32) * scale + bias
            if diag:
                s = jnp.where(causal, s, -jnp.inf)
            m_old = m_scr[c]
            m_new = jnp.maximum(m_old, jnp.max(s, axis=0, keepdims=True))
            alpha = jnp.exp(m_old - m_new)
            p = jnp.exp(s - m_new)
            l_scr[c] = alpha * l_scr[c] + jnp.sum(p, axis=0, keepdims=True)
            acc_scr[c] = alpha * acc_scr[c] + jnp.dot(vt, p.astype(BF16), preferred_element_type=F32)
            m_scr[c] = m_new

    def far_oct(j, carry):
        step(8 * j, 2, False)
        step(8 * j + 2, 2, False)
        step(8 * j + 4, 2, False)
        step(8 * j + 6, 2, False)
        return carry

    lax.fori_loop(0, i // 8, far_oct, 0)

    @pl.when(i % 8 >= 4)
    def _():
        step((i // 8) * 8, 2, False)
        step((i // 8) * 8 + 2, 2, False)

    @pl.when(i % 4 >= 2)
    def _():
        step((i // 4) * 4, 2, False)

    @pl.when(i % 2 == 1)
    def _():
        step(i - 1, 1, False)

    step(i, 1, True)
    out = acc_scr[0] / l_scr[0] - lam * (acc_scr[1] / l_scr[1])
    ms = jnp.mean(out * out, axis=0, keepdims=True)
    out = out * lax.rsqrt(ms + EPS) * subg_ref[...] * (1.0 - lambda_init)
    o_ref[...] = out.T.astype(o_ref.dtype)


def diff_attention(dqt, dk, dvt, bias_d, lam_vecs, sub_g, bsz, seq, lambda_init):
    n = bsz * seq
    tq = _tile(seq, 256)
    nq = seq // tq
    bias = _toeplitz_bias(bias_d, tq, tq, transposed=True)
    kinds = bias.shape[0]
    width = 2 * D_QK_DIM
    vmem = 2 * (tq * width * 2 + seq * width * 2 + seq * D_V_DIM * 2 + kinds * tq * tq * 4
                + tq * D_V_DIM * 2) + 2 * tq * D_V_DIM * 4 + 10 * tq * tq * 4
    return pl.pallas_call(
        functools.partial(_diff_kernel, tq=tq, kinds=kinds, scale=D_QK_DIM ** -0.5, lambda_init=lambda_init),
        grid=(bsz, D_HEADS, nq),
        in_specs=[pl.BlockSpec((width, tq), lambda b, h, i: (h, b * nq + i)),
                  pl.BlockSpec((seq, width), lambda b, h, i: (b, h)),
                  pl.BlockSpec((D_V_DIM, seq), lambda b, h, i: (h, b)),
                  pl.BlockSpec((kinds, None, tq, tq), lambda b, h, i: (0, h, 0, 0)),
                  pl.BlockSpec((4, D_QK_DIM), lambda b, h, i: (0, 0)),
                  pl.BlockSpec((D_V_DIM, 1), lambda b, h, i: (0, 0))],
        out_specs=pl.BlockSpec((tq, D_V_DIM), lambda b, h, i: (b * nq + i, h)),
        out_shape=jax.ShapeDtypeStruct((n, D_HEADS * D_V_DIM), BF16),
        scratch_shapes=[pltpu.VMEM((2, 1, tq), F32), pltpu.VMEM((2, 1, tq), F32),
                        pltpu.VMEM((2, D_V_DIM, tq), F32)],
        compiler_params=_cparams(("parallel", "parallel", "parallel"), vmem + 4 * MIB),
        name="diff_attention",
    )(dqt, dk, dvt, bias, lam_vecs, sub_g.reshape(D_V_DIM, 1).astype(F32))


def _swa_kernel(q_ref, kp_ref, ko_ref, vp_ref, vo_ref, bias_ref, sink_ref, o_ref, *, blk, scale):
    i = pl.program_id(1)
    left = lax.broadcasted_iota(I32, (blk, LANES), 1) < C_DIM
    qa = lax.broadcasted_iota(I32, (blk, 2 * blk), 0)
    sa = lax.broadcasted_iota(I32, (blk, 2 * blk), 1)
    rel = qa + blk - sa
    lo = jnp.where(i > 0, 0, blk)
    madd = jnp.where(rel >= 0, jnp.where(rel < WINDOW, jnp.where(sa >= lo, 0.0, -jnp.inf), -jnp.inf), -jnp.inf)
    grp = C_HEADS // C_KV_HEADS
    for g in range(C_KV_HEADS):
        ksl = slice(g * LANES, (g + 1) * LANES)
        kk = jnp.concatenate([kp_ref[:, ksl], ko_ref[:, ksl]], axis=0)
        vv = jnp.concatenate([vp_ref[:, ksl], vo_ref[:, ksl]], axis=0)
        for pr in range(grp // 2):
            cb = g * (grp // 2) + pr
            qp = q_ref[:, cb * LANES:(cb + 1) * LANES]
            outs = []
            for half in range(2):
                h = 2 * cb + half
                qh = jnp.where(left if half == 0 else jnp.logical_not(left), qp, jnp.zeros_like(qp))
                lg = lax.dot_general(qh, kk, (((1,), (1,)), ((), ())), preferred_element_type=F32) * scale
                lg = lg + bias_ref[h] + madd
                sink = sink_ref[h]
                m = jnp.maximum(jnp.max(lg, axis=-1, keepdims=True), sink)
                e = jnp.exp(lg - m)
                p = e / (jnp.sum(e, axis=-1, keepdims=True) + jnp.exp(sink - m))
                outs.append(jnp.dot(p.astype(BF16), vv, preferred_element_type=F32))
            o_ref[:, cb * LANES:(cb + 1) * LANES] = jnp.where(left, outs[0], outs[1]).astype(o_ref.dtype)


def swa_attention(cq, ck2, cv2, bias_c, sinks, bsz, seq):
    n = bsz * seq
    blk = WINDOW
    nb = seq // blk
    bias = _toeplitz_tiles(bias_c, [blk], blk, 2 * blk, False)[0]
    qw = C_HEADS * C_DIM
    kw = C_KV_HEADS * LANES
    own = lambda b, i: (b * nb + i, 0)
    prev = lambda b, i: (b * nb + jnp.maximum(i - 1, 0), 0)
    vmem = 2 * (2 * blk * qw * 2 + 4 * blk * kw * 2 + C_HEADS * blk * 2 * blk * 4) + 16 * blk * 2 * blk * 4
    return pl.pallas_call(
        functools.partial(_swa_kernel, blk=blk, scale=C_DIM ** -0.5),
        grid=(bsz, nb),
        in_specs=[pl.BlockSpec((blk, qw), own),
                  pl.BlockSpec((blk, kw), prev), pl.BlockSpec((blk, kw), own),
                  pl.BlockSpec((blk, kw), prev), pl.BlockSpec((blk, kw), own),
                  pl.BlockSpec((C_HEADS, blk, 2 * blk), lambda b, i: (0, 0, 0)),
                  pl.BlockSpec(memory_space=pltpu.SMEM)],
        out_specs=pl.BlockSpec((blk, qw), own),
        out_shape=jax.ShapeDtypeStruct((n, qw), BF16),
        compiler_params=_cparams(("parallel", "parallel"), vmem + 4 * MIB),
        name="swa_attention",
    )(cq, ck2, ck2, cv2, cv2, bias, sinks.astype(F32))


def _xattn_kernel(q_ref, k_ref, v_ref, qg_ref, kg_ref, o_ref, *, scale):
    for h in range(X_HEADS):
        sl = slice(h * X_DIM, (h + 1) * X_DIM)
        q = q_ref[:, sl]
        k = k_ref[:, sl]
        qn = (q * lax.rsqrt(jnp.mean(q * q, axis=-1, keepdims=True) + EPS) * qg_ref[...]).astype(BF16)
        kn = (k * lax.rsqrt(jnp.mean(k * k, axis=-1, keepdims=True) + EPS) * kg_ref[...]).astype(BF16)
        lg = lax.dot_general(qn, kn, (((1,), (1,)), ((), ())), preferred_element_type=F32) * scale
        m = jnp.max(lg, axis=-1, keepdims=True)
        e = jnp.exp(lg - m)
        p = (e / jnp.sum(e, axis=-1, keepdims=True)).astype(BF16)
        o_ref[:, sl] = jnp.dot(p, v_ref[:, sl], preferred_element_type=F32).astype(o_ref.dtype)


def cross_attention(qx, kx, vx, q_g, k_g, bsz, seq, mem_len):
    n = bsz * seq
    tq = _tile(seq, 512)
    nq = seq // tq
    w = X_HEADS * X_DIM
    vmem = 2 * (tq * w * 4 + mem_len * w * 6 + tq * w * 2) + 8 * tq * mem_len * 4
    return pl.pallas_call(
        functools.partial(_xattn_kernel, scale=X_DIM ** -0.5),
        grid=(bsz, nq),
        in_specs=[pl.BlockSpec((tq, w), lambda b, i: (b * nq + i, 0)),
                  pl.BlockSpec((mem_len, w), lambda b, i: (b, 0)),
                  pl.BlockSpec((mem_len, w), lambda b, i: (b, 0)),
                  pl.BlockSpec((1, X_DIM), lambda b, i: (0, 0)),
                  pl.BlockSpec((1, X_DIM), lambda b, i: (0, 0))],
        out_specs=pl.BlockSpec((tq, w), lambda b, i: (b * nq + i, 0)),
        out_shape=jax.ShapeDtypeStruct((n, w), BF16),
        compiler_params=_cparams(("parallel", "parallel"), vmem + 4 * MIB),
        name="cross_attention",
    )(qx, kx, vx, q_g.reshape(1, X_DIM).astype(F32), k_g.reshape(1, X_DIM).astype(F32))


def _even_mixer(xf, h, bsz, seq, w_in, q_lat_g, w_qb, q_g, kv_g, w_uv, w_qi, kidx_g, w_out, rel_bias):
    a_cols = A_Q_RANK + A_KV_RANK + IDX_DIM
    a_width = -(-(a_cols + IDX_HEADS) // 256) * 256
    w_a = w_in[:, :a_width].astype(BF16)
    w_b = w_in[:, a_cols + IDX_HEADS:].astype(BF16)
    pa = matmul(h, w_a, out_dtype=F32, tn=256)
    qkv = matmul(h, w_b, out_dtype=BF16, tn=512)

    cq = groupnorm(pa, 0, A_Q_RANK, q_lat_g, A_Q_RANK)
    c, ct = groupnorm(pa, A_Q_RANK, A_KV_RANK, kv_g, A_KV_RANK, layouts=(False, True))
    ki = groupnorm(pa, A_Q_RANK + A_KV_RANK, IDX_DIM, kidx_g, IDX_DIM)
    qt = matmul(cq, w_qb.reshape(A_Q_RANK, A_HEADS * A_KV_RANK).astype(BF16), out_dtype=BF16,
                tn=A_KV_RANK, norm_gain=q_g, transpose_out=True)
    qit = matmul(cq, w_qi.reshape(A_Q_RANK, IDX_HEADS * IDX_DIM).astype(BF16), out_dtype=BF16,
                 tn=512, transpose_out=True)
    wit = pa[:, a_cols:a_cols + IDX_HEADS].T
    topk = min(TOPK_MAX, seq // 4)
    scores_t, thr = dsa_indexer(qit, ki, wit, bsz, seq, topk)
    o_a = dsa_attention(qt, c, ct, scores_t, thr, rel_bias[:, BIAS_A_OFF:BIAS_A_OFF + A_HEADS],
                        jnp.swapaxes(w_uv, 1, 2).astype(BF16), bsz, seq)
    o_b = stickbreak_attention(qkv, bsz, seq)
    return matmul(o_a, w_out.astype(BF16), a2=o_b, out_dtype=F32, tn=1024, residual=xf)


def _odd_mixer(xf, h, bsz, seq, w_in, c_q_g, c_k_g, sinks, d_q_g, d_k_g, lam_q1, lam_k1, lam_q2, lam_k2,
               sub_g, w_out, rel_bias, lambda_init):
    n = xf.shape[0]
    cw = C_HEADS * C_DIM
    ckw = C_KV_HEADS * C_DIM
    dw = D_HEADS * 2 * D_QK_DIM
    c_end = cw + 2 * ckw
    w_bf = w_in.astype(BF16)
    p1 = matmul(h, w_bf, n=c_end, out_dtype=F32, tn=512)
    p2 = matmul(h, w_bf, col0=c_end, n=2 * dw, out_dtype=F32, tn=512)
    dvt = matmul(h, w_bf, col0=c_end + 2 * dw, n=D_HEADS * D_V_DIM, out_dtype=BF16, tn=512,
                 transpose_out=True)

    cq = groupnorm(p1, 0, cw, c_q_g, C_DIM)
    ck = groupnorm(p1, cw, ckw, c_k_g, C_DIM)
    cv = p1[:, cw + ckw:].astype(BF16)
    dup = lambda a: jnp.concatenate([a.reshape(n, C_KV_HEADS, C_DIM)] * 2, axis=-1).reshape(n, C_KV_HEADS * LANES)
    o_c = swa_attention(cq, dup(ck), dup(cv), rel_bias[:, BIAS_C_OFF:BIAS_C_OFF + C_HEADS], sinks, bsz, seq)

    dqt = groupnorm(p2, 0, dw, d_q_g, D_QK_DIM, layouts=(True,))
    dk = groupnorm(p2, dw, dw, d_k_g, D_QK_DIM)
    lam_vecs = jnp.stack([lam_q1, lam_k1, lam_q2, lam_k2]).astype(F32)
    o_d = diff_attention(dqt, dk, dvt, rel_bias[:, BIAS_D_OFF:BIAS_D_OFF + D_HEADS], lam_vecs, sub_g,
                         bsz, seq, lambda_init)
    return matmul(o_c, w_out.astype(BF16), a2=o_d, out_dtype=F32, tn=1024, residual=xf)


def _cross_block(xf, memn, norm_g, wq, wk, wv, q_g, k_g, wo, bsz, seq, mem_len):
    d = xf.shape[1]
    w = X_HEADS * X_DIM
    h = rmsnorm_rows(xf, norm_g)
    qx = matmul(h, wq.reshape(d, w).astype(BF16), out_dtype=F32, tn=512)
    kx = matmul(memn, wk.reshape(d, w).astype(BF16), out_dtype=F32, tn=512)
    vx = matmul(memn, wv.reshape(d, w).astype(BF16), out_dtype=BF16, tn=512)
    ox = cross_attention(qx, kx, vx, q_g, k_g, bsz, seq, mem_len)
    return matmul(ox, wo.astype(BF16), out_dtype=F32, tn=1024, residual=xf)


def _ffn_block(xf, norm_g, w_gate, w_up, conv_w, conv_b, w_down, seq):
    h = rmsnorm_rows(xf, norm_g)
    act = ffn_gate_up(h, w_gate, w_up, conv_w, conv_b, seq)
    return matmul(act, w_down.astype(BF16), out_dtype=F32, tm=512, tn=512, residual=xf)


def kernel(x, mem, rel_bias, mem_norm_g, mix_norm_g, xattn_norm_g, ffn_norm_g, ev_w_in, ev_q_lat_g, ev_w_qb, ev_q_g, ev_kv_g, ev_w_uv, ev_w_qi, ev_kidx_g, ev_w_out, od_w_in, od_c_q_g, od_c_k_g, od_sinks, od_d_q_g, od_d_k_g, od_lam_q1, od_lam_k1, od_lam_q2, od_lam_k2, od_sub_g, od_w_out, x_wq, x_wk, x_wv, x_q_g, x_k_g, x_wo, f_w_gate, f_w_up, f_conv_w, f_conv_b, f_w_down):
    bsz, seq, d = x.shape
    mem_len = mem.shape[1]
    depth = mix_norm_g.shape[0]
    xf = x.reshape(bsz * seq, d)
    memn = rmsnorm_rows(mem.reshape(bsz * mem_len, d), mem_norm_g)
    for l in range(depth):
        h = rmsnorm_rows(xf, mix_norm_g[l])
        if l % 2 == 0:
            e = l // 2
            xf = _even_mixer(xf, h, bsz, seq, ev_w_in[e], ev_q_lat_g[e], ev_w_qb[e], ev_q_g[e], ev_kv_g[e],
                             ev_w_uv[e], ev_w_qi[e], ev_kidx_g[e], ev_w_out[e], rel_bias)
        else:
            o = l // 2
            lambda_init = 0.8 - 0.6 * math.exp(-0.3 * l)
            xf = _odd_mixer(xf, h, bsz, seq, od_w_in[o], od_c_q_g[o], od_c_k_g[o], od_sinks[o], od_d_q_g[o],
                            od_d_k_g[o], od_lam_q1[o], od_lam_k1[o], od_lam_q2[o], od_lam_k2[o], od_sub_g[o],
                            od_w_out[o], rel_bias, lambda_init)
        xf = _cross_block(xf, memn, xattn_norm_g[l], x_wq[l], x_wk[l], x_wv[l], x_q_g[l], x_k_g[l], x_wo[l],
                          bsz, seq, mem_len)
        xf = _ffn_block(xf, ffn_norm_g[l], f_w_gate[l], f_w_up[l], f_conv_w[l], f_conv_b[l], f_w_down[l], seq)
    return xf.reshape(bsz, seq, d)
```

```python
import functools
import math

import jax
import jax.numpy as jnp
from jax import lax
from jax.experimental import pallas as pl
from jax.experimental.pallas import tpu as pltpu

F32 = jnp.float32
BF16 = jnp.bfloat16
I32 = jnp.int32

EPS = 1e-6
LANES = 128
MIB = 1024 * 1024
VMEM_CAP = 58 * MIB
M_INIT = -1e30
INT_MIN = -(2 ** 31)

A_HEADS, A_Q_RANK, A_KV_RANK, A_V_DIM = 16, 1024, 512, 128
IDX_HEADS, IDX_DIM, TOPK_MAX = 32, 128, 256
B_HEADS, B_DIM = 16, 128
C_HEADS, C_KV_HEADS, C_DIM, WINDOW = 32, 4, 64, 128
D_HEADS, D_QK_DIM, D_V_DIM = 8, 128, 256
X_HEADS, X_DIM = 4, 128
CONV_W = 3
NUM_BUCKETS, MAX_EXACT, MAX_DISTANCE = 32, 16, 128
FAR_DIST = 113
BIAS_A_OFF, BIAS_C_OFF, BIAS_D_OFF = 0, A_HEADS, A_HEADS + C_HEADS
SB_SKIP = -104.0


def _cparams(sem, vmem_bytes):
    return pltpu.CompilerParams(dimension_semantics=sem,
                                vmem_limit_bytes=int(min(max(vmem_bytes, 16 * MIB), VMEM_CAP)))


def _tile(n, pref):
    t = min(n, pref)
    while n % t:
        t //= 2
    return t


def _rmsnorm_kernel(x_ref, g_ref, o_ref):
    x = x_ref[...]
    ms = jnp.mean(x * x, axis=-1, keepdims=True)
    o_ref[...] = (x * lax.rsqrt(ms + EPS) * g_ref[...]).astype(o_ref.dtype)


def rmsnorm_rows(x, g):
    m, d = x.shape
    tm = _tile(m, 256)
    return pl.pallas_call(
        _rmsnorm_kernel,
        grid=(m // tm,),
        in_specs=[pl.BlockSpec((tm, d), lambda i: (i, 0)),
                  pl.BlockSpec((1, d), lambda i: (0, 0))],
        out_specs=pl.BlockSpec((tm, d), lambda i: (i, 0)),
        out_shape=jax.ShapeDtypeStruct((m, d), BF16),
        compiler_params=_cparams(("parallel",), 4 * tm * d * 6),
        name="rmsnorm_rows",
    )(x, g.reshape(1, d).astype(F32))


def _rms_groups(x, g, gd):
    width = x.shape[-1]
    parts = []
    if gd >= LANES:
        for j in range(width // gd):
            xs = x[:, j * gd:(j + 1) * gd]
            ms = jnp.mean(xs * xs, axis=-1, keepdims=True)
            parts.append(xs * lax.rsqrt(ms + EPS) * g)
    else:
        left = lax.broadcasted_iota(I32, (x.shape[0], LANES), 1) < gd
        for j in range(width // LANES):
            xs = x[:, j * LANES:(j + 1) * LANES]
            sq = xs * xs
            tot = jnp.sum(sq, axis=-1, keepdims=True)
            lsum = jnp.sum(jnp.where(left, sq, 0.0), axis=-1, keepdims=True)
            ms = jnp.where(left, lsum, tot - lsum) * (1.0 / gd)
            parts.append(xs * lax.rsqrt(ms + EPS) * g)
    return parts


def _group_gain(gain, gd):
    gw = max(gd, LANES)
    return jnp.tile(gain.astype(F32), gw // gd).reshape(1, gw)


def _groupnorm_kernel(x_ref, g_ref, *o_refs, gd, layouts):
    parts = _rms_groups(x_ref[...].astype(F32), g_ref[...], gd)
    pw = parts[0].shape[-1]
    for o_ref, transposed in zip(o_refs, layouts):
        for j, y in enumerate(parts):
            if transposed:
                o_ref[j * pw:(j + 1) * pw, :] = y.T.astype(o_ref.dtype)
            else:
                o_ref[:, j * pw:(j + 1) * pw] = y.astype(o_ref.dtype)


def groupnorm(x, col0, width, gain, gd, layouts=(False,)):
    m = x.shape[0]
    assert col0 % width == 0 and width % gd == 0 and (gd % LANES == 0 or 2 * gd == LANES)
    tm = _tile(m, 256)
    g = _group_gain(gain, gd)
    gw = g.shape[1]
    cb = col0 // width
    out_specs = [pl.BlockSpec((width, tm), lambda i: (0, i)) if t else pl.BlockSpec((tm, width), lambda i: (i, 0))
                 for t in layouts]
    out_shape = [jax.ShapeDtypeStruct((width, m) if t else (m, width), BF16) for t in layouts]
    outs = pl.pallas_call(
        functools.partial(_groupnorm_kernel, gd=gd, layouts=tuple(layouts)),
        grid=(m // tm,),
        in_specs=[pl.BlockSpec((tm, width), lambda i: (i, cb)),
                  pl.BlockSpec((1, gw), lambda i: (0, 0))],
        out_specs=out_specs,
        out_shape=out_shape,
        compiler_params=_cparams(("parallel",), 4 * tm * width * (8 + 2 * len(layouts))),
        name="groupnorm",
    )(x, g)
    return outs[0] if len(layouts) == 1 else outs


def _mm_kernel(*refs, nk, has_a2, has_res, norm_gd, transpose_out):
    a_ref, w_ref = refs[0], refs[1]
    pos = 2
    a2_ref = w2_ref = res_ref = g_ref = None
    if has_a2:
        a2_ref, w2_ref = refs[pos], refs[pos + 1]
        pos += 2
    if has_res:
        res_ref = refs[pos]
        pos += 1
    if norm_gd:
        g_ref = refs[pos]
        pos += 1
    o_ref = refs[pos]
    acc_ref = refs[pos + 1] if nk > 1 else None

    def epilogue(acc):
        if norm_gd:
            parts = _rms_groups(acc, g_ref[...], norm_gd)
            acc = parts[0] if len(parts) == 1 else jnp.concatenate(parts, axis=-1)
        if has_res:
            acc = acc + res_ref[...]
        acc = acc.astype(o_ref.dtype)
        o_ref[...] = acc.T if transpose_out else acc

    part = jnp.dot(a_ref[...], w_ref[...], preferred_element_type=F32)
    if has_a2:
        part = part + jnp.dot(a2_ref[...], w2_ref[...], preferred_element_type=F32)
    if nk == 1:
        epilogue(part)
    else:
        k = pl.program_id(2)

        @pl.when(k == 0)
        def _():
            acc_ref[...] = part

        @pl.when(k > 0)
        def _():
            acc_ref[...] += part

        @pl.when(k == nk - 1)
        def _():
            epilogue(acc_ref[...])


def matmul(a, w, *, out_dtype, tm=1024, tn=512, tk=None, col0=0, n=None, a2=None, residual=None,
           norm=None, transpose_out=False):
    m, kdim = a.shape
    n = w.shape[1] if n is None else n
    tm = _tile(m, tm)
    tn = _tile(n, tn)
    tk = kdim if tk is None else _tile(kdim, tk)
    nk = kdim // tk
    assert col0 % tn == 0 and (norm is None or tn % max(norm[0], LANES) == 0)
    assert a2 is None or (nk == 1 and a2.shape == a.shape and w.shape[0] == 2 * kdim)
    cb = col0 // tn
    in_specs = [pl.BlockSpec((tm, tk), lambda i, j, k: (i, k)),
                pl.BlockSpec((tk, tn), lambda i, j, k: (k, j + cb))]
    args = [a, w]
    if a2 is not None:
        in_specs += [pl.BlockSpec((tm, tk), lambda i, j, k: (i, 0)),
                     pl.BlockSpec((tk, tn), lambda i, j, k: (1, j + cb))]
        args += [a2, w]
    if residual is not None:
        in_specs.append(pl.BlockSpec((tm, tn), lambda i, j, k: (i, j)))
        args.append(residual)
    if norm is not None:
        gain = _group_gain(norm[1], norm[0])
        in_specs.append(pl.BlockSpec(gain.shape, lambda i, j, k: (0, 0)))
        args.append(gain)
    if transpose_out:
        out_spec = pl.BlockSpec((tn, tm), lambda i, j, k: (j, i))
        out_shape = jax.ShapeDtypeStruct((n, m), out_dtype)
    else:
        out_spec = pl.BlockSpec((tm, tn), lambda i, j, k: (i, j))
        out_shape = jax.ShapeDtypeStruct((m, n), out_dtype)
    osz = jnp.dtype(out_dtype).itemsize
    nops = 2 if a2 is not None else 1
    vmem = 2 * (nops * (tm * tk * 2 + tk * tn * 2) + tm * tn * osz) + tm * tn * 4 * 3
    if residual is not None:
        vmem += 2 * tm * tn * 4
    return pl.pallas_call(
        functools.partial(_mm_kernel, nk=nk, has_a2=a2 is not None, has_res=residual is not None,
                          norm_gd=None if norm is None else norm[0], transpose_out=transpose_out),
        grid=(m // tm, n // tn, nk),
        in_specs=in_specs,
        out_specs=out_spec,
        out_shape=out_shape,
        scratch_shapes=[pltpu.VMEM((tm, tn), F32)] if nk > 1 else [],
        compiler_params=_cparams(("parallel", "parallel", "arbitrary"), vmem + 4 * MIB),
        name="matmul",
    )(*args)


HALO = 16


def _ffn_gu_kernel(a_ref, ah_ref, wg_ref, wu_ref, cw_ref, cb_ref, o_ref, *, tm, seq):
    i = pl.program_id(0)
    a = a_ref[...]
    wg = wg_ref[...].astype(BF16)
    g = jnp.dot(a, wg, preferred_element_type=F32)
    u = jnp.dot(a, wu_ref[...].astype(BF16), preferred_element_type=F32)
    gh = jnp.dot(ah_ref[...], wg, preferred_element_type=F32)
    seq_start = (i * tm) % seq == 0
    gh = jnp.where(seq_start, 0.0, gh)
    rows = lax.broadcasted_iota(I32, g.shape, 0)
    hm1 = gh[HALO - 1:HALO, :]
    hm2 = gh[HALO - 2:HALO - 1, :]
    g1 = jnp.where(rows == 0, hm1, pltpu.roll(g, 1, 0))
    g2 = jnp.where(rows == 0, hm2, jnp.where(rows == 1, hm1, pltpu.roll(g, 2, 0)))
    cw = cw_ref[...]
    c = cw[0:1, :] * g2 + cw[1:2, :] * g1 + cw[2:3, :] * g + cb_ref[...]
    o_ref[...] = (c * jax.nn.sigmoid(c) * u).astype(o_ref.dtype)


def ffn_gate_up(h, wg, wu, layer, conv_w, conv_b, seq):
    m, d = h.shape
    f = wg.shape[2]
    tm = _tile(seq, 1024)
    tn = _tile(f, 256)
    hb = tm // HALO
    vmem = 2 * (tm * d * 2 + HALO * d * 2 + 2 * d * tn * 4 + tm * tn * 2) + 2 * d * tn * 2 + 8 * tm * tn * 4
    return pl.pallas_call(
        functools.partial(_ffn_gu_kernel, tm=tm, seq=seq),
        grid=(m // tm, f // tn),
        in_specs=[pl.BlockSpec((tm, d), lambda i, j: (i, 0)),
                  pl.BlockSpec((HALO, d), lambda i, j: (jnp.maximum(i * hb - 1, 0), 0)),
                  pl.BlockSpec((None, d, tn), lambda i, j: (layer, 0, j)),
                  pl.BlockSpec((None, d, tn), lambda i, j: (layer, 0, j)),
                  pl.BlockSpec((CONV_W, tn), lambda i, j: (0, j)),
                  pl.BlockSpec((1, tn), lambda i, j: (0, j))],
        out_specs=pl.BlockSpec((tm, tn), lambda i, j: (i, j)),
        out_shape=jax.ShapeDtypeStruct((m, f), BF16),
        compiler_params=_cparams(("parallel", "parallel"), vmem + 4 * MIB),
        name="ffn_gate_up",
    )(h, h, wg, wu, conv_w.astype(F32), conv_b.reshape(1, f).astype(F32))


def _t5_bucket(dist):
    n = jnp.maximum(dist, 0)
    nf = jnp.maximum(n, 1).astype(F32)
    large = MAX_EXACT + (jnp.log(nf / MAX_EXACT) / math.log(MAX_DISTANCE / MAX_EXACT)
                         * (NUM_BUCKETS - MAX_EXACT)).astype(I32)
    return jnp.where(n < MAX_EXACT, n, jnp.minimum(large, NUM_BUCKETS - 1))


def _near_kinds(tq, tk):
    return -(-(FAR_DIST - 1 + tk) // tq)


def _toeplitz_tiles(table, d0s, tq, tk, transposed):
    n = tq + tk
    j = jnp.arange(n, dtype=I32)
    if transposed:
        rel, rows, cols = jnp.where(j < tq, j, j - n), tk, tq
    else:
        rel, rows, cols = jnp.where(j < tk, -j, n - j), tq, tk
    d0 = jnp.asarray(d0s, I32)[:, None]
    vals = jnp.moveaxis(table.astype(F32)[_t5_bucket(d0 + rel[None, :])], -1, 1)
    x = jnp.tile(vals, (1, 1, rows))[:, :, :rows * (n - 1)]
    return x.reshape(len(d0s), table.shape[1], rows, n - 1)[..., :cols]


def _toeplitz_bias(table, tq, tk, transposed=False):
    kinds = _near_kinds(tq, tk) + 1
    return _toeplitz_tiles(table, [d * tq for d in range(kinds)], tq, tk, transposed)


def _sortable_key(x):
    b = lax.bitcast_convert_type(x, I32)
    return b ^ ((b >> 31) & jnp.int32(0x7FFFFFFF))


def _indexer_kernel(qt_ref, k_ref, wi_ref, sc_ref, thr_ref, key_scr, *, tq, tkc, topk, wscale):
    i = pl.program_id(1)
    nch = (i * tq + tq + tkc - 1) // tkc
    wi = wi_ref[...] * wscale
    sc_ref[...] = jnp.full(sc_ref.shape, -jnp.inf, F32)
    q_idx = i * tq + lax.broadcasted_iota(I32, (tkc, tq), 1)
    k_loc = lax.broadcasted_iota(I32, (tkc, tq), 0)

    def chunk(c, carry):
        off = pl.multiple_of(c * tkc, tkc)
        kc = k_ref[pl.ds(off, tkc), :]
        s = jnp.zeros((tkc, tq), F32)
        for h in range(IDX_HEADS):
            r = jnp.dot(kc, qt_ref[h * IDX_DIM:(h + 1) * IDX_DIM, :], preferred_element_type=F32)
            s = s + jnp.maximum(r, 0.0) * wi[h:h + 1, :]
        s = jnp.where(off + k_loc <= q_idx, s, -jnp.inf)
        sc_ref[pl.ds(off, tkc), :] = s
        key_scr[pl.ds(off, tkc), :] = _sortable_key(s)
        return carry

    lax.fori_loop(0, nch, chunk, 0)

    def count_ge(cand):
        def count(c, acc):
            off = pl.multiple_of(c * tkc, tkc)
            hit = (key_scr[pl.ds(off, tkc), :] >= cand).astype(I32)
            return acc + jnp.sum(hit.reshape(tkc // 8, 8, tq), axis=0)

        acc = lax.fori_loop(0, nch, count, jnp.zeros((8, tq), I32))
        return jnp.sum(acc, axis=0, keepdims=True)

    def cond(state):
        it, _, _, settled = state
        return jnp.logical_and(it < 32, settled == 0)

    def bit_step(state):
        it, v, cnt_v, _ = state
        cand = v + jnp.left_shift(jnp.int32(1), 31 - it)
        cnt = count_ge(cand)
        take = cnt >= topk
        v = jnp.where(take, cand, v)
        cnt_v = jnp.where(take, cnt, cnt_v)
        settled = jnp.min(jnp.where(cnt_v == topk, 1, 0))
        return it + 1, v, cnt_v, settled

    v0 = jnp.full((1, tq), INT_MIN, I32)
    cnt0 = jnp.full((1, tq), nch * tkc, I32)
    thr_ref[...] = lax.while_loop(cond, bit_step, (jnp.int32(0), v0, cnt0, jnp.int32(0)))[1]


def dsa_indexer(qit, ki, wit, bsz, seq, topk):
    n = bsz * seq
    tq = _tile(seq, 256)
    tkc = _tile(seq, 256)
    nq = seq // tq
    vmem = 2 * (IDX_HEADS * IDX_DIM * tq * 2 + seq * IDX_DIM * 2 + IDX_HEADS * tq * 4 + seq * tq * 4) \
        + seq * tq * 4 + 8 * tkc * tq * 4
    return pl.pallas_call(
        functools.partial(_indexer_kernel, tq=tq, tkc=tkc, topk=topk,
                          wscale=IDX_HEADS ** -0.5 * IDX_DIM ** -0.5),
        grid=(bsz, nq),
        in_specs=[pl.BlockSpec((IDX_HEADS * IDX_DIM, tq), lambda b, i: (0, b * nq + i)),
                  pl.BlockSpec((seq, IDX_DIM), lambda b, i: (b, 0)),
                  pl.BlockSpec((IDX_HEADS, tq), lambda b, i: (0, b * nq + i))],
        out_specs=[pl.BlockSpec((seq, tq), lambda b, i: (b, i)),
                   pl.BlockSpec((1, tq), lambda b, i: (0, b * nq + i))],
        out_shape=[jax.ShapeDtypeStruct((n, seq), F32),
                   jax.ShapeDtypeStruct((1, n), I32)],
        scratch_shapes=[pltpu.VMEM((seq, tq), I32)],
        compiler_params=_cparams(("parallel", "parallel"), vmem + 4 * MIB),
        name="dsa_indexer",
    )(qit, ki, wit)


def _dsa_attn_kernel(qi_tab, kb_tab, qt_ref, c_ref, ct_ref, sc_ref, thr_ref, bias_ref, wuvt_ref, o_ref,
                     m_scr, l_scr, acc_scr, madd_scr, *, tq, tk, scale):
    s_id = pl.program_id(1)
    i = qi_tab[s_id]
    kb = kb_tab[s_id]
    nh = A_HEADS
    rank = A_KV_RANK

    @pl.when(kb == 0)
    def _():
        m_scr[...] = jnp.full(m_scr.shape, M_INIT, F32)
        l_scr[...] = jnp.zeros(l_scr.shape, F32)
        acc_scr[...] = jnp.zeros(acc_scr.shape, F32)

    key = _sortable_key(sc_ref[...])
    k_idx = kb * tk + lax.broadcasted_iota(I32, (tk, tq), 0)
    q_idx = i * tq + lax.broadcasted_iota(I32, (tk, tq), 1)
    madd_scr[...] = jnp.where(k_idx <= q_idx, jnp.where(key >= thr_ref[...], 0.0, -jnp.inf), -jnp.inf)

    for h in range(nh):
        qh = qt_ref[h * rank:(h + 1) * rank, :]
        s = jnp.dot(c_ref[...], qh, preferred_element_type=F32) * scale + bias_ref[h] + madd_scr[...]
        m_old = m_scr[h]
        m_new = jnp.maximum(m_old, jnp.max(s, axis=0, keepdims=True))
        alpha = jnp.exp(m_old - m_new)
        p = jnp.exp(s - m_new)
        l_scr[h] = alpha * l_scr[h] + jnp.sum(p, axis=0, keepdims=True)
        acc_scr[h] = alpha * acc_scr[h] + jnp.dot(ct_ref[...], p.astype(BF16),
                                                  preferred_element_type=F32)
        m_scr[h] = m_new

    @pl.when(kb == (i * tq + tq - 1) // tk)
    def _():
        for h in range(nh):
            o_lat = (acc_scr[h] / l_scr[h]).astype(BF16)
            out_t = jnp.dot(wuvt_ref[h], o_lat, preferred_element_type=F32)
            o_ref[:, h * A_V_DIM:(h + 1) * A_V_DIM] = out_t.T.astype(o_ref.dtype)


def dsa_attention(qt, c, ct, scores_t, thr, bias_a, w_uvt, bsz, seq):
    n = bsz * seq
    tq = _tile(seq, 256)
    tk = tq
    nq, nkb = seq // tq, seq // tk
    bias = _toeplitz_bias(bias_a, tq, tk, transposed=True)
    kinds = bias.shape[0]
    rank = A_KV_RANK
    pairs = [(i, kb) for i in range(nq) for kb in range((i * tq + tq - 1) // tk + 1)]
    qi_tab = jnp.asarray([p[0] for p in pairs], I32)
    kb_tab = jnp.asarray([p[1] for p in pairs], I32)

    def kind(i, kb):
        return jnp.minimum(i - kb * (tk // tq), kinds - 1)

    vmem = 2 * (A_HEADS * rank * tq * 2 + 2 * tk * rank * 2 + tk * tq * 4 + A_HEADS * tk * tq * 4
                + A_HEADS * rank * A_V_DIM * 2 + tq * A_HEADS * A_V_DIM * 2) \
        + A_HEADS * rank * tq * 4 + 10 * tk * tq * 4
    grid_spec = pltpu.PrefetchScalarGridSpec(
        num_scalar_prefetch=2,
        grid=(bsz, len(pairs)),
        in_specs=[pl.BlockSpec((A_HEADS * rank, tq), lambda b, s, qi, kb: (0, b * nq + qi[s])),
                  pl.BlockSpec((tk, rank), lambda b, s, qi, kb: (b * nkb + kb[s], 0)),
                  pl.BlockSpec((rank, tk), lambda b, s, qi, kb: (0, b * nkb + kb[s])),
                  pl.BlockSpec((tk, tq), lambda b, s, qi, kb: (b * nkb + kb[s], qi[s])),
                  pl.BlockSpec((1, tq), lambda b, s, qi, kb: (0, b * nq + qi[s])),
                  pl.BlockSpec((None, A_HEADS, tk, tq), lambda b, s, qi, kb: (kind(qi[s], kb[s]), 0, 0, 0)),
                  pl.BlockSpec((A_HEADS, A_V_DIM, rank), lambda b, s, qi, kb: (0, 0, 0))],
        out_specs=pl.BlockSpec((tq, A_HEADS * A_V_DIM), lambda b, s, qi, kb: (b * nq + qi[s], 0)),
        scratch_shapes=[pltpu.VMEM((A_HEADS, 1, tq), F32),
                        pltpu.VMEM((A_HEADS, 1, tq), F32),
                        pltpu.VMEM((A_HEADS, rank, tq), F32),
                        pltpu.VMEM((tk, tq), F32)])
    return pl.pallas_call(
        functools.partial(_dsa_attn_kernel, tq=tq, tk=tk, scale=rank ** -0.5),
        grid_spec=grid_spec,
        out_shape=jax.ShapeDtypeStruct((n, A_HEADS * A_V_DIM), BF16),
        compiler_params=_cparams(("parallel", "arbitrary"), vmem + 4 * MIB),
        name="dsa_attention",
    )(qi_tab, kb_tab, qt, c, ct, scores_t, thr, bias, w_uvt)


def _sb_kernel(q_ref, k_ref, v_ref, u_ref, o_ref, carry_scr, acc_scr, *, tq, scale):
    i = pl.program_id(2)
    q = q_ref[...]
    tri = u_ref[...]
    row = lax.broadcasted_iota(I32, (tq, tq), 0)
    col = lax.broadcasted_iota(I32, (tq, tq), 1)
    strict = col < row

    def block(kb, diag):
        off = pl.multiple_of(kb * tq, tq)
        k = k_ref[pl.ds(off, tq), :]
        z = lax.dot_general(q, k, (((1,), (1,)), ((), ())), preferred_element_type=F32) * scale
        log1m = -(jnp.maximum(z, 0.0) + jnp.log(1.0 + jnp.exp(-jnp.abs(z))))
        if diag:
            log1m = jnp.where(strict, log1m, 0.0)
        hi = log1m.astype(BF16)
        lo = (log1m - hi.astype(F32)).astype(BF16)
        between = (jnp.dot(hi, tri, preferred_element_type=F32)
                   + jnp.dot(lo, tri, preferred_element_type=F32))
        logw = z + log1m + between
        return logw, jnp.sum(log1m, axis=-1, keepdims=True), v_ref[pl.ds(off, tq), :]

    has_prev = i > 0
    lw_d, rs_d, v_d = block(i, True)
    lw_p, rs_p, v_p = block(jnp.maximum(i - 1, 0), False)
    w_d = jnp.where(strict, jnp.exp(lw_d), 0.0)
    w_p = jnp.where(has_prev, jnp.exp(lw_p + rs_d), 0.0)
    acc_scr[...] = (jnp.dot(w_d.astype(BF16), v_d, preferred_element_type=F32)
                    + jnp.dot(w_p.astype(BF16), v_p, preferred_element_type=F32))
    carry = rs_d + jnp.where(has_prev, rs_p, 0.0)
    carry_scr[...] = carry

    def cond(state):
        kb, cmax = state
        return jnp.logical_and(kb >= 0, cmax > SB_SKIP)

    def body(state):
        kb, _ = state
        logw, rs, v = block(kb, False)
        carry = carry_scr[...]
        w = jnp.exp(logw + carry)
        acc_scr[...] += jnp.dot(w.astype(BF16), v, preferred_element_type=F32)
        carry = carry + rs
        carry_scr[...] = carry
        return kb - 1, jnp.max(carry)

    lax.while_loop(cond, body, (i - 2, jnp.max(carry)))
    o_ref[...] = acc_scr[...].astype(o_ref.dtype)


def stickbreak_attention(qkv, bsz, seq):
    n = bsz * seq
    tq = _tile(seq, 256)
    nq = seq // tq
    tri = (jnp.arange(tq)[:, None] > jnp.arange(tq)[None, :]).astype(BF16)
    vmem = 2 * (2 * seq * B_DIM * 2 + 2 * tq * B_DIM * 2 + tq * tq * 2) + 12 * tq * tq * 4
    return pl.pallas_call(
        functools.partial(_sb_kernel, tq=tq, scale=B_DIM ** -0.5),
        grid=(bsz, B_HEADS, nq),
        in_specs=[pl.BlockSpec((tq, B_DIM), lambda b, h, i: (b * nq + i, h)),
                  pl.BlockSpec((seq, B_DIM), lambda b, h, i: (b, B_HEADS + h)),
                  pl.BlockSpec((seq, B_DIM), lambda b, h, i: (b, 2 * B_HEADS + h)),
                  pl.BlockSpec((tq, tq), lambda b, h, i: (0, 0))],
        out_specs=pl.BlockSpec((tq, B_DIM), lambda b, h, i: (b * nq + i, h)),
        out_shape=jax.ShapeDtypeStruct((n, B_HEADS * B_DIM), BF16),
        scratch_shapes=[pltpu.VMEM((tq, 1), F32), pltpu.VMEM((tq, B_DIM), F32)],
        compiler_params=_cparams(("parallel", "parallel", "parallel"), vmem + 4 * MIB),
        name="stickbreak_attention",
    )(qkv, qkv, qkv, tri)


def _diff_kernel(qt_ref, k_ref, vt_ref, bias_ref, lam_ref, subg_ref, o_ref, m_scr, l_scr, acc_scr,
                 *, tq, kinds, scale, lambda_init):
    i = pl.program_id(2)
    dq = D_QK_DIM
    lv = lam_ref[...]
    lam = (jnp.exp(jnp.sum(lv[0:1] * lv[1:2], axis=-1, keepdims=True))
           - jnp.exp(jnp.sum(lv[2:3] * lv[3:4], axis=-1, keepdims=True))) + lambda_init
    key_i = lax.broadcasted_iota(I32, (tq, tq), 0)
    qry_i = lax.broadcasted_iota(I32, (tq, tq), 1)
    causal = key_i <= qry_i

    m_scr[...] = jnp.full(m_scr.shape, M_INIT, F32)
    l_scr[...] = jnp.zeros(l_scr.shape, F32)
    acc_scr[...] = jnp.zeros(acc_scr.shape, F32)

    def step(kb, nblk, diag):
        tk = nblk * tq
        off = pl.multiple_of(kb * tq, tq)
        k = k_ref[pl.ds(off, tk), :]
        vt = vt_ref[:, pl.ds(off, tk)]
        bias = jnp.concatenate([bias_ref[jnp.minimum(i - kb - j, kinds - 1)] for j in range(nblk)],
                               axis=0)
        for c in range(2):
            s = jnp.dot(k[:, c * dq:(c + 1) * dq], qt_ref[c * dq:(c + 1) * dq, :],
                        preferred_element_type=F32) * scale + bias
            if diag:
                s = jnp.where(causal, s, -jnp.inf)
            m_old = m_scr[c]
            m_new = jnp.maximum(m_old, jnp.max(s, axis=0, keepdims=True))
            alpha = jnp.exp(m_old - m_new)
            p = jnp.exp(s - m_new)
            l_scr[c] = alpha * l_scr[c] + jnp.sum(p, axis=0, keepdims=True)
            acc_scr[c] = alpha * acc_scr[c] + jnp.dot(vt, p.astype(BF16), preferred_element_type=F32)
            m_scr[c] = m_new

    def far_oct(j, carry):
        step(8 * j, 2, False)
        step(8 * j + 2, 2, False)
        step(8 * j + 4, 2, False)
        step(8 * j + 6, 2, False)
        return carry

    lax.fori_loop(0, i // 8, far_oct, 0)

    @pl.when(i % 8 >= 4)
    def _():
        step((i // 8) * 8, 2, False)
        step((i // 8) * 8 + 2, 2, False)

    @pl.when(i % 4 >= 2)
    def _():
        step((i // 4) * 4, 2, False)

    @pl.when(i % 2 == 1)
    def _():
        step(i - 1, 1, False)

    step(i, 1, True)
    out = acc_scr[0] / l_scr[0] - lam * (acc_scr[1] / l_scr[1])
    ms = jnp.mean(out * out, axis=0, keepdims=True)
    out = out * lax.rsqrt(ms + EPS) * subg_ref[...] * (1.0 - lambda_init)
    o_ref[...] = out.T.astype(o_ref.dtype)


def diff_attention(dqt, dk, dvt, bias_d, lam_vecs, sub_g, bsz, seq, lambda_init):
    n = bsz * seq
    tq = _tile(seq, 256)
    nq = seq // tq
    bias = _toeplitz_bias(bias_d, tq, tq, transposed=True)
    kinds = bias.shape[0]
    width = 2 * D_QK_DIM
    vmem = 2 * (tq * width * 2 + seq * width * 2 + seq * D_V_DIM * 2 + kinds * tq * tq * 4
                + tq * D_V_DIM * 2) + 2 * tq * D_V_DIM * 4 + 10 * tq * tq * 4
    return pl.pallas_call(
        functools.partial(_diff_kernel, tq=tq, kinds=kinds, scale=D_QK_DIM ** -0.5, lambda_init=lambda_init),
        grid=(bsz, D_HEADS, nq),
        in_specs=[pl.BlockSpec((width, tq), lambda b, h, i: (h, b * nq + i)),
                  pl.BlockSpec((seq, width), lambda b, h, i: (b, h)),
                  pl.BlockSpec((D_V_DIM, seq), lambda b, h, i: (h, b)),
                  pl.BlockSpec((kinds, None, tq, tq), lambda b, h, i: (0, h, 0, 0)),
                  pl.BlockSpec((4, D_QK_DIM), lambda b, h, i: (0, 0)),
                  pl.BlockSpec((D_V_DIM, 1), lambda b, h, i: (0, 0))],
        out_specs=pl.BlockSpec((tq, D_V_DIM), lambda b, h, i: (b * nq + i, h)),
        out_shape=jax.ShapeDtypeStruct((n, D_HEADS * D_V_DIM), BF16),
        scratch_shapes=[pltpu.VMEM((2, 1, tq), F32), pltpu.VMEM((2, 1, tq), F32),
                        pltpu.VMEM((2, D_V_DIM, tq), F32)],
        compiler_params=_cparams(("parallel", "parallel", "parallel"), vmem + 4 * MIB),
        name="diff_attention",
    )(dqt, dk, dvt, bias, lam_vecs, sub_g.reshape(D_V_DIM, 1).astype(F32))


def _swa_kernel(q_ref, kp_ref, ko_ref, vp_ref, vo_ref, bias_ref, sink_ref, o_ref, *, blk, scale):
    i = pl.program_id(1)
    left = lax.broadcasted_iota(I32, (blk, LANES), 1) < C_DIM
    qa = lax.broadcasted_iota(I32, (blk, 2 * blk), 0)
    sa = lax.broadcasted_iota(I32, (blk, 2 * blk), 1)
    rel = qa + blk - sa
    lo = jnp.where(i > 0, 0, blk)
    madd = jnp.where(rel >= 0, jnp.where(rel < WINDOW, jnp.where(sa >= lo, 0.0, -jnp.inf), -jnp.inf), -jnp.inf)
    grp = C_HEADS // C_KV_HEADS
    for g in range(C_KV_HEADS):
        ksl = slice(g * LANES, (g + 1) * LANES)
        kk = jnp.concatenate([kp_ref[:, ksl], ko_ref[:, ksl]], axis=0)
        vv = jnp.concatenate([vp_ref[:, ksl], vo_ref[:, ksl]], axis=0)
        for pr in range(grp // 2):
            cb = g * (grp // 2) + pr
            qp = q_ref[:, cb * LANES:(cb + 1) * LANES]
            outs = []
            for half in range(2):
                h = 2 * cb + half
                qh = jnp.where(left if half == 0 else jnp.logical_not(left), qp, jnp.zeros_like(qp))
                lg = lax.dot_general(qh, kk, (((1,), (1,)), ((), ())), preferred_element_type=F32) * scale
                lg = lg + bias_ref[h] + madd
                sink = sink_ref[h]
                m = jnp.maximum(jnp.max(lg, axis=-1, keepdims=True), sink)
                e = jnp.exp(lg - m)
                p = e / (jnp.sum(e, axis=-1, keepdims=True) + jnp.exp(sink - m))
                outs.append(jnp.dot(p.astype(BF16), vv, preferred_element_type=F32))
            o_ref[:, cb * LANES:(cb + 1) * LANES] = jnp.where(left, outs[0], outs[1]).astype(o_ref.dtype)


def swa_attention(cq, ck2, cv2, bias_c, sinks, bsz, seq):
    n = bsz * seq
    blk = WINDOW
    nb = seq // blk
    bias = _toeplitz_tiles(bias_c, [blk], blk, 2 * blk, False)[0]
    qw = C_HEADS * C_DIM
    kw = C_KV_HEADS * LANES
    own = lambda b, i: (b * nb + i, 0)
    prev = lambda b, i: (b * nb + jnp.maximum(i - 1, 0), 0)
    vmem = 2 * (2 * blk * qw * 2 + 4 * blk * kw * 2 + C_HEADS * blk * 2 * blk * 4) + 16 * blk * 2 * blk * 4
    return pl.pallas_call(
        functools.partial(_swa_kernel, blk=blk, scale=C_DIM ** -0.5),
        grid=(bsz, nb),
        in_specs=[pl.BlockSpec((blk, qw), own),
                  pl.BlockSpec((blk, kw), prev), pl.BlockSpec((blk, kw), own),
                  pl.BlockSpec((blk, kw), prev), pl.BlockSpec((blk, kw), own),
                  pl.BlockSpec((C_HEADS, blk, 2 * blk), lambda b, i: (0, 0, 0)),
                  pl.BlockSpec(memory_space=pltpu.SMEM)],
        out_specs=pl.BlockSpec((blk, qw), own),
        out_shape=jax.ShapeDtypeStruct((n, qw), BF16),
        compiler_params=_cparams(("parallel", "parallel"), vmem + 4 * MIB),
        name="swa_attention",
    )(cq, ck2, ck2, cv2, cv2, bias, sinks.astype(F32))


def _xattn_kernel(q_ref, k_ref, v_ref, qg_ref, kg_ref, o_ref, *, scale):
    for h in range(X_HEADS):
        sl = slice(h * X_DIM, (h + 1) * X_DIM)
        q = q_ref[:, sl]
        k = k_ref[:, sl]
        qn = (q * lax.rsqrt(jnp.mean(q * q, axis=-1, keepdims=True) + EPS) * qg_ref[...]).astype(BF16)
        kn = (k * lax.rsqrt(jnp.mean(k * k, axis=-1, keepdims=True) + EPS) * kg_ref[...]).astype(BF16)
        lg = lax.dot_general(qn, kn, (((1,), (1,)), ((), ())), preferred_element_type=F32) * scale
        m = jnp.max(lg, axis=-1, keepdims=True)
        e = jnp.exp(lg - m)
        p = (e / jnp.sum(e, axis=-1, keepdims=True)).astype(BF16)
        o_ref[:, sl] = jnp.dot(p, v_ref[:, sl], preferred_element_type=F32).astype(o_ref.dtype)


def cross_attention(qx, kx, vx, q_g, k_g, bsz, seq, mem_len):
    n = bsz * seq
    tq = _tile(seq, 512)
    nq = seq // tq
    w = X_HEADS * X_DIM
    vmem = 2 * (tq * w * 4 + mem_len * w * 6 + tq * w * 2) + 8 * tq * mem_len * 4
    return pl.pallas_call(
        functools.partial(_xattn_kernel, scale=X_DIM ** -0.5),
        grid=(bsz, nq),
        in_specs=[pl.BlockSpec((tq, w), lambda b, i: (b * nq + i, 0)),
                  pl.BlockSpec((mem_len, w), lambda b, i: (b, 0)),
                  pl.BlockSpec((mem_len, w), lambda b, i: (b, 0)),
                  pl.BlockSpec((1, X_DIM), lambda b, i: (0, 0)),
                  pl.BlockSpec((1, X_DIM), lambda b, i: (0, 0))],
        out_specs=pl.BlockSpec((tq, w), lambda b, i: (b * nq + i, 0)),
        out_shape=jax.ShapeDtypeStruct((n, w), BF16),
        compiler_params=_cparams(("parallel", "parallel"), vmem + 4 * MIB),
        name="cross_attention",
    )(qx, kx, vx, q_g.reshape(1, X_DIM).astype(F32), k_g.reshape(1, X_DIM).astype(F32))


def _even_mixer(xf, h, bsz, seq, w_in, q_lat_g, w_qb, q_g, kv_g, w_uv, w_qi, kidx_g, w_out, rel_bias):
    a_cols = A_Q_RANK + A_KV_RANK + IDX_DIM
    a_width = -(-(a_cols + IDX_HEADS) // 256) * 256
    w_a = w_in[:, :a_width].astype(BF16)
    w_b = w_in[:, a_cols + IDX_HEADS:].astype(BF16)
    pa = matmul(h, w_a, out_dtype=F32, tn=256)
    qkv = matmul(h, w_b, out_dtype=BF16, tn=512)

    cq = groupnorm(pa, 0, A_Q_RANK, q_lat_g, A_Q_RANK)
    c, ct = groupnorm(pa, A_Q_RANK, A_KV_RANK, kv_g, A_KV_RANK, layouts=(False, True))
    ki = groupnorm(pa, A_Q_RANK + A_KV_RANK, IDX_DIM, kidx_g, IDX_DIM)
    qt = matmul(cq, w_qb.reshape(A_Q_RANK, A_HEADS * A_KV_RANK).astype(BF16), out_dtype=BF16,
                tn=A_KV_RANK, norm=(A_KV_RANK, q_g), transpose_out=True)
    qit = matmul(cq, w_qi.reshape(A_Q_RANK, IDX_HEADS * IDX_DIM).astype(BF16), out_dtype=BF16,
                 tn=512, transpose_out=True)
    wit = pa[:, a_cols:a_cols + IDX_HEADS].T
    topk = min(TOPK_MAX, seq // 4)
    scores_t, thr = dsa_indexer(qit, ki, wit, bsz, seq, topk)
    o_a = dsa_attention(qt, c, ct, scores_t, thr, rel_bias[:, BIAS_A_OFF:BIAS_A_OFF + A_HEADS],
                        jnp.swapaxes(w_uv, 1, 2).astype(BF16), bsz, seq)
    o_b = stickbreak_attention(qkv, bsz, seq)
    return matmul(o_a, w_out.astype(BF16), a2=o_b, out_dtype=F32, tn=1024, residual=xf)


def _odd_mixer(xf, h, bsz, seq, w_in, c_q_g, c_k_g, sinks, d_q_g, d_k_g, lam_q1, lam_k1, lam_q2, lam_k2,
               sub_g, w_out, rel_bias, lambda_init):
    n = xf.shape[0]
    cw = C_HEADS * C_DIM
    ckw = C_KV_HEADS * C_DIM
    dw = D_HEADS * 2 * D_QK_DIM
    c_end = cw + 2 * ckw
    w_bf = w_in.astype(BF16)
    proj = functools.partial(matmul, h, w_bf, out_dtype=BF16)
    cq = proj(n=cw, norm=(C_DIM, c_q_g), tn=512)
    ck = proj(col0=cw, n=ckw, norm=(C_DIM, c_k_g), tn=256)
    cv = proj(col0=cw + ckw, n=ckw, tn=256)
    dqt = proj(col0=c_end, n=dw, norm=(D_QK_DIM, d_q_g), tn=512, transpose_out=True)
    dk = proj(col0=c_end + dw, n=dw, norm=(D_QK_DIM, d_k_g), tn=512)
    dvt = proj(col0=c_end + 2 * dw, n=D_HEADS * D_V_DIM, tn=512, transpose_out=True)

    dup = lambda a: jnp.concatenate([a.reshape(n, C_KV_HEADS, C_DIM)] * 2, axis=-1).reshape(n, C_KV_HEADS * LANES)
    o_c = swa_attention(cq, dup(ck), dup(cv), rel_bias[:, BIAS_C_OFF:BIAS_C_OFF + C_HEADS], sinks, bsz, seq)

    lam_vecs = jnp.stack([lam_q1, lam_k1, lam_q2, lam_k2]).astype(F32)
    o_d = diff_attention(dqt, dk, dvt, rel_bias[:, BIAS_D_OFF:BIAS_D_OFF + D_HEADS], lam_vecs, sub_g,
                         bsz, seq, lambda_init)
    return matmul(o_c, w_out.astype(BF16), a2=o_d, out_dtype=F32, tn=1024, residual=xf)


def _cross_block(xf, memn, norm_g, wq, wk, wv, q_g, k_g, wo, bsz, seq, mem_len):
    d = xf.shape[1]
    w = X_HEADS * X_DIM
    h = rmsnorm_rows(xf, norm_g)
    qx = matmul(h, wq.reshape(d, w).astype(BF16), out_dtype=F32, tn=512)
    kx = matmul(memn, wk.reshape(d, w).astype(BF16), out_dtype=F32, tn=512)
    vx = matmul(memn, wv.reshape(d, w).astype(BF16), out_dtype=BF16, tn=512)
    ox = cross_attention(qx, kx, vx, q_g, k_g, bsz, seq, mem_len)
    return matmul(ox, wo.astype(BF16), out_dtype=F32, tn=1024, residual=xf)


def _ffn_block(xf, norm_g, w_gate, w_up, layer, conv_w, conv_b, w_down, seq):
    h = rmsnorm_rows(xf, norm_g)
    act = ffn_gate_up(h, w_gate, w_up, layer, conv_w, conv_b, seq)
    return matmul(act, w_down.astype(BF16), out_dtype=F32, tm=512, tn=512, residual=xf)


def kernel(x, mem, rel_bias, mem_norm_g, mix_norm_g, xattn_norm_g, ffn_norm_g, ev_w_in, ev_q_lat_g, ev_w_qb, ev_q_g, ev_kv_g, ev_w_uv, ev_w_qi, ev_kidx_g, ev_w_out, od_w_in, od_c_q_g, od_c_k_g, od_sinks, od_d_q_g, od_d_k_g, od_lam_q1, od_lam_k1, od_lam_q2, od_lam_k2, od_sub_g, od_w_out, x_wq, x_wk, x_wv, x_q_g, x_k_g, x_wo, f_w_gate, f_w_up, f_conv_w, f_conv_b, f_w_down):
    bsz, seq, d = x.shape
    mem_len = mem.shape[1]
    depth = mix_norm_g.shape[0]
    xf = x.reshape(bsz * seq, d)
    memn = rmsnorm_rows(mem.reshape(bsz * mem_len, d), mem_norm_g)
    for l in range(depth):
        h = rmsnorm_rows(xf, mix_norm_g[l])
        if l % 2 == 0:
            e = l // 2
            xf = _even_mixer(xf, h, bsz, seq, ev_w_in[e], ev_q_lat_g[e], ev_w_qb[e], ev_q_g[e], ev_kv_g[e],
                             ev_w_uv[e], ev_w_qi[e], ev_kidx_g[e], ev_w_out[e], rel_bias)
        else:
            o = l // 2
            lambda_init = 0.8 - 0.6 * math.exp(-0.3 * l)
            xf = _odd_mixer(xf, h, bsz, seq, od_w_in[o], od_c_q_g[o], od_c_k_g[o], od_sinks[o], od_d_q_g[o],
                            od_d_k_g[o], od_lam_q1[o], od_lam_k1[o], od_lam_q2[o], od_lam_k2[o], od_sub_g[o],
                            od_w_out[o], rel_bias, lambda_init)
        xf = _cross_block(xf, memn, xattn_norm_g[l], x_wq[l], x_wk[l], x_wv[l], x_q_g[l], x_k_g[l], x_wo[l],
                          bsz, seq, mem_len)
        xf = _ffn_block(xf, ffn_norm_g[l], f_w_gate, f_w_up, l, f_conv_w[l], f_conv_b[l], f_w_down[l], seq)
    return xf.reshape(bsz, seq, d)
```

```python
import functools
import math

import jax
import jax.numpy as jnp
from jax import lax
from jax.experimental import pallas as pl
from jax.experimental.pallas import tpu as pltpu

F32 = jnp.float32
BF16 = jnp.bfloat16
I32 = jnp.int32

EPS = 1e-6
LANES = 128
MIB = 1024 * 1024
VMEM_CAP = 58 * MIB
M_INIT = -1e30
INT_MIN = -(2 ** 31)

A_HEADS, A_Q_RANK, A_KV_RANK, A_V_DIM = 16, 1024, 512, 128
IDX_HEADS, IDX_DIM, TOPK_MAX = 32, 128, 256
B_HEADS, B_DIM = 16, 128
C_HEADS, C_KV_HEADS, C_DIM, WINDOW = 32, 4, 64, 128
D_HEADS, D_QK_DIM, D_V_DIM = 8, 128, 256
X_HEADS, X_DIM = 4, 128
CONV_W = 3
NUM_BUCKETS, MAX_EXACT, MAX_DISTANCE = 32, 16, 128
FAR_DIST = 113
BIAS_A_OFF, BIAS_C_OFF, BIAS_D_OFF = 0, A_HEADS, A_HEADS + C_HEADS
LOG2E = math.log2(math.e)
SB_SKIP = -104.0


def _cparams(sem, vmem_bytes):
    return pltpu.CompilerParams(dimension_semantics=sem,
                                vmem_limit_bytes=int(min(max(vmem_bytes, 16 * MIB), VMEM_CAP)))


def _tile(n, pref):
    t = min(n, pref)
    while n % t:
        t //= 2
    return t


def _rmsnorm_kernel(x_ref, g_ref, o_ref):
    x = x_ref[...]
    ms = jnp.mean(x * x, axis=-1, keepdims=True)
    o_ref[...] = (x * lax.rsqrt(ms + EPS) * g_ref[...]).astype(o_ref.dtype)


def rmsnorm_rows(x, g):
    m, d = x.shape
    tm = _tile(m, 256)
    return pl.pallas_call(
        _rmsnorm_kernel,
        grid=(m // tm,),
        in_specs=[pl.BlockSpec((tm, d), lambda i: (i, 0)),
                  pl.BlockSpec((1, d), lambda i: (0, 0))],
        out_specs=pl.BlockSpec((tm, d), lambda i: (i, 0)),
        out_shape=jax.ShapeDtypeStruct((m, d), BF16),
        compiler_params=_cparams(("parallel",), 4 * tm * d * 6),
        name="rmsnorm_rows",
    )(x, g.reshape(1, d).astype(F32))


def _rms_groups(x, g, gd):
    width = x.shape[-1]
    parts = []
    if gd >= LANES:
        for j in range(width // gd):
            xs = x[:, j * gd:(j + 1) * gd]
            ms = jnp.mean(xs * xs, axis=-1, keepdims=True)
            parts.append(xs * lax.rsqrt(ms + EPS) * g)
    else:
        left = lax.broadcasted_iota(I32, (x.shape[0], LANES), 1) < gd
        for j in range(width // LANES):
            xs = x[:, j * LANES:(j + 1) * LANES]
            sq = xs * xs
            tot = jnp.sum(sq, axis=-1, keepdims=True)
            lsum = jnp.sum(jnp.where(left, sq, 0.0), axis=-1, keepdims=True)
            ms = jnp.where(left, lsum, tot - lsum) * (1.0 / gd)
            parts.append(xs * lax.rsqrt(ms + EPS) * g)
    return parts


def _group_gain(gain, gd):
    gw = max(gd, LANES)
    return jnp.tile(gain.astype(F32), gw // gd).reshape(1, gw)


def _groupnorm_kernel(x_ref, g_ref, *o_refs, gd, layouts):
    parts = _rms_groups(x_ref[...].astype(F32), g_ref[...], gd)
    pw = parts[0].shape[-1]
    for o_ref, transposed in zip(o_refs, layouts):
        for j, y in enumerate(parts):
            if transposed:
                o_ref[j * pw:(j + 1) * pw, :] = y.T.astype(o_ref.dtype)
            else:
                o_ref[:, j * pw:(j + 1) * pw] = y.astype(o_ref.dtype)


def groupnorm(x, col0, width, gain, gd, layouts=(False,)):
    m = x.shape[0]
    assert col0 % width == 0 and width % gd == 0 and (gd % LANES == 0 or 2 * gd == LANES)
    tm = _tile(m, 256)
    g = _group_gain(gain, gd)
    gw = g.shape[1]
    cb = col0 // width
    out_specs = [pl.BlockSpec((width, tm), lambda i: (0, i)) if t else pl.BlockSpec((tm, width), lambda i: (i, 0))
                 for t in layouts]
    out_shape = [jax.ShapeDtypeStruct((width, m) if t else (m, width), BF16) for t in layouts]
    outs = pl.pallas_call(
        functools.partial(_groupnorm_kernel, gd=gd, layouts=tuple(layouts)),
        grid=(m // tm,),
        in_specs=[pl.BlockSpec((tm, width), lambda i: (i, cb)),
                  pl.BlockSpec((1, gw), lambda i: (0, 0))],
        out_specs=out_specs,
        out_shape=out_shape,
        compiler_params=_cparams(("parallel",), 4 * tm * width * (8 + 2 * len(layouts))),
        name="groupnorm",
    )(x, g)
    return outs[0] if len(layouts) == 1 else outs


def _mm_kernel(*refs, nk, has_a2, has_res, norm_gd, transpose_out):
    a_ref, w_ref = refs[0], refs[1]
    pos = 2
    a2_ref = w2_ref = res_ref = g_ref = None
    if has_a2:
        a2_ref, w2_ref = refs[pos], refs[pos + 1]
        pos += 2
    if has_res:
        res_ref = refs[pos]
        pos += 1
    if norm_gd:
        g_ref = refs[pos]
        pos += 1
    o_ref = refs[pos]
    acc_ref = refs[pos + 1] if nk > 1 else None

    def epilogue(acc):
        if norm_gd:
            parts = _rms_groups(acc, g_ref[...], norm_gd)
            acc = parts[0] if len(parts) == 1 else jnp.concatenate(parts, axis=-1)
        if has_res:
            acc = acc + res_ref[...]
        acc = acc.astype(o_ref.dtype)
        o_ref[...] = acc.T if transpose_out else acc

    part = jnp.dot(a_ref[...], w_ref[...], preferred_element_type=F32)
    if has_a2:
        part = part + jnp.dot(a2_ref[...], w2_ref[...], preferred_element_type=F32)
    if nk == 1:
        epilogue(part)
    else:
        k = pl.program_id(2)

        @pl.when(k == 0)
        def _():
            acc_ref[...] = part

        @pl.when(k > 0)
        def _():
            acc_ref[...] += part

        @pl.when(k == nk - 1)
        def _():
            epilogue(acc_ref[...])


def matmul(a, w, *, out_dtype, tm=1024, tn=512, tk=None, col0=0, n=None, a2=None, residual=None,
           norm=None, transpose_out=False):
    m, kdim = a.shape
    n = w.shape[1] if n is None else n
    tm = _tile(m, tm)
    tn = _tile(n, tn)
    tk = kdim if tk is None else _tile(kdim, tk)
    nk = kdim // tk
    assert col0 % tn == 0 and (norm is None or tn % max(norm[0], LANES) == 0)
    assert a2 is None or (nk == 1 and a2.shape == a.shape and w.shape[0] == 2 * kdim)
    cb = col0 // tn
    in_specs = [pl.BlockSpec((tm, tk), lambda i, j, k: (i, k)),
                pl.BlockSpec((tk, tn), lambda i, j, k: (k, j + cb))]
    args = [a, w]
    if a2 is not None:
        in_specs += [pl.BlockSpec((tm, tk), lambda i, j, k: (i, 0)),
                     pl.BlockSpec((tk, tn), lambda i, j, k: (1, j + cb))]
        args += [a2, w]
    if residual is not None:
        in_specs.append(pl.BlockSpec((tm, tn), lambda i, j, k: (i, j)))
        args.append(residual)
    if norm is not None:
        gain = _group_gain(norm[1], norm[0])
        in_specs.append(pl.BlockSpec(gain.shape, lambda i, j, k: (0, 0)))
        args.append(gain)
    if transpose_out:
        out_spec = pl.BlockSpec((tn, tm), lambda i, j, k: (j, i))
        out_shape = jax.ShapeDtypeStruct((n, m), out_dtype)
    else:
        out_spec = pl.BlockSpec((tm, tn), lambda i, j, k: (i, j))
        out_shape = jax.ShapeDtypeStruct((m, n), out_dtype)
    osz = jnp.dtype(out_dtype).itemsize
    nops = 2 if a2 is not None else 1
    vmem = 2 * (nops * (tm * tk * 2 + tk * tn * 2) + tm * tn * osz) + tm * tn * 4 * 3
    if residual is not None:
        vmem += 2 * tm * tn * 4
    return pl.pallas_call(
        functools.partial(_mm_kernel, nk=nk, has_a2=a2 is not None, has_res=residual is not None,
                          norm_gd=None if norm is None else norm[0], transpose_out=transpose_out),
        grid=(m // tm, n // tn, nk),
        in_specs=in_specs,
        out_specs=out_spec,
        out_shape=out_shape,
        scratch_shapes=[pltpu.VMEM((tm, tn), F32)] if nk > 1 else [],
        compiler_params=_cparams(("parallel", "parallel", "arbitrary"), vmem + 4 * MIB),
        name="matmul",
    )(*args)


HALO = 16


def _ffn_gu_kernel(a_ref, ah_ref, wg_ref, wu_ref, cw_ref, cb_ref, o_ref, *, tm, seq):
    i = pl.program_id(0)
    a = a_ref[...]
    wg = wg_ref[...].astype(BF16)
    g = jnp.dot(a, wg, preferred_element_type=F32)
    u = jnp.dot(a, wu_ref[...].astype(BF16), preferred_element_type=F32)
    gh = jnp.dot(ah_ref[...], wg, preferred_element_type=F32)
    seq_start = (i * tm) % seq == 0
    gh = jnp.where(seq_start, 0.0, gh)
    rows = lax.broadcasted_iota(I32, g.shape, 0)
    hm1 = gh[HALO - 1:HALO, :]
    hm2 = gh[HALO - 2:HALO - 1, :]
    g1 = jnp.where(rows == 0, hm1, pltpu.roll(g, 1, 0))
    g2 = jnp.where(rows == 0, hm2, jnp.where(rows == 1, hm1, pltpu.roll(g, 2, 0)))
    cw = cw_ref[...]
    c = cw[0:1, :] * g2 + cw[1:2, :] * g1 + cw[2:3, :] * g + cb_ref[...]
    o_ref[...] = (c * jax.nn.sigmoid(c) * u).astype(o_ref.dtype)


def ffn_gate_up(h, wg, wu, layer, conv_w, conv_b, seq):
    m, d = h.shape
    f = wg.shape[2]
    tm = _tile(seq, 1024)
    tn = _tile(f, 256)
    hb = tm // HALO
    vmem = 2 * (tm * d * 2 + HALO * d * 2 + 2 * d * tn * 4 + tm * tn * 2) + 2 * d * tn * 2 + 8 * tm * tn * 4
    return pl.pallas_call(
        functools.partial(_ffn_gu_kernel, tm=tm, seq=seq),
        grid=(m // tm, f // tn),
        in_specs=[pl.BlockSpec((tm, d), lambda i, j: (i, 0)),
                  pl.BlockSpec((HALO, d), lambda i, j: (jnp.maximum(i * hb - 1, 0), 0)),
                  pl.BlockSpec((None, d, tn), lambda i, j: (layer, 0, j)),
                  pl.BlockSpec((None, d, tn), lambda i, j: (layer, 0, j)),
                  pl.BlockSpec((CONV_W, tn), lambda i, j: (0, j)),
                  pl.BlockSpec((1, tn), lambda i, j: (0, j))],
        out_specs=pl.BlockSpec((tm, tn), lambda i, j: (i, j)),
        out_shape=jax.ShapeDtypeStruct((m, f), BF16),
        compiler_params=_cparams(("parallel", "parallel"), vmem + 4 * MIB),
        name="ffn_gate_up",
    )(h, h, wg, wu, conv_w.astype(F32), conv_b.reshape(1, f).astype(F32))


def _t5_bucket(dist):
    n = jnp.maximum(dist, 0)
    nf = jnp.maximum(n, 1).astype(F32)
    large = MAX_EXACT + (jnp.log(nf / MAX_EXACT) / math.log(MAX_DISTANCE / MAX_EXACT)
                         * (NUM_BUCKETS - MAX_EXACT)).astype(I32)
    return jnp.where(n < MAX_EXACT, n, jnp.minimum(large, NUM_BUCKETS - 1))


def _near_kinds(tq, tk):
    return -(-(FAR_DIST - 1 + tk) // tq)


def _toeplitz_tiles(table, d0s, tq, tk, transposed):
    n = tq + tk
    j = jnp.arange(n, dtype=I32)
    if transposed:
        rel, rows, cols = jnp.where(j < tq, j, j - n), tk, tq
    else:
        rel, rows, cols = jnp.where(j < tk, -j, n - j), tq, tk
    d0 = jnp.asarray(d0s, I32)[:, None]
    vals = jnp.moveaxis(table.astype(F32)[_t5_bucket(d0 + rel[None, :])], -1, 1)
    x = jnp.tile(vals, (1, 1, rows))[:, :, :rows * (n - 1)]
    return x.reshape(len(d0s), table.shape[1], rows, n - 1)[..., :cols]


def _toeplitz_bias(table, tq, tk, transposed=False):
    kinds = _near_kinds(tq, tk) + 1
    return _toeplitz_tiles(table, [d * tq for d in range(kinds)], tq, tk, transposed)


def _sortable_key(x):
    b = lax.bitcast_convert_type(x, I32)
    return b ^ ((b >> 31) & jnp.int32(0x7FFFFFFF))


I16 = jnp.int16
HALF_MIN = -(2 ** 15)


def _indexer_kernel(qt_ref, k_ref, wi_ref, sc_ref, thr_ref, hi_scr, lo_scr, *, tq, tkc, topk, wscale):
    i = pl.program_id(1)
    nch = (i * tq + tq + tkc - 1) // tkc
    wi = wi_ref[...] * wscale
    sc_ref[...] = jnp.full(sc_ref.shape, -jnp.inf, F32)
    q_idx = i * tq + lax.broadcasted_iota(I32, (tkc, tq), 1)
    k_loc = lax.broadcasted_iota(I32, (tkc, tq), 0)

    def chunk(c, carry):
        off = pl.multiple_of(c * tkc, tkc)
        kc = k_ref[pl.ds(off, tkc), :]
        s = jnp.zeros((tkc, tq), F32)
        for h in range(IDX_HEADS):
            r = jnp.dot(kc, qt_ref[h * IDX_DIM:(h + 1) * IDX_DIM, :], preferred_element_type=F32)
            s = s + jnp.maximum(r, 0.0) * wi[h:h + 1, :]
        s = jnp.where(off + k_loc <= q_idx, s, -jnp.inf)
        sc_ref[pl.ds(off, tkc), :] = s
        key = _sortable_key(s)
        hi_scr[pl.ds(off, tkc), :] = (key >> 16).astype(I16)
        lo_scr[pl.ds(off, tkc), :] = ((key & 0xFFFF) + HALF_MIN).astype(I16)
        return carry

    lax.fori_loop(0, nch, chunk, 0)

    def count(ref, hit_fn):
        def body(c, acc):
            off = pl.multiple_of(c * tkc, tkc)
            ones = jnp.where(hit_fn(ref[pl.ds(off, tkc), :]), jnp.int16(1), jnp.int16(0))
            for j in range(tkc // 16):
                acc = acc + ones[j * 16:(j + 1) * 16, :]
            return acc

        acc = lax.fori_loop(0, nch, body, jnp.zeros((16, tq), I16))
        return jnp.sum(acc.astype(I32), axis=0, keepdims=True)

    def search(ref, need):
        def bit_step(it, v):
            cand = v + jnp.left_shift(jnp.int32(1), 15 - it)
            c16 = cand.astype(I16)
            cnt = count(ref, lambda x: x >= c16)
            return jnp.where(cnt >= need, cand, v)

        return lax.fori_loop(0, 16, bit_step, jnp.full((1, tq), HALF_MIN, I32))

    vh = search(hi_scr, topk)
    vh16 = vh.astype(I16)
    need_lo = topk - count(hi_scr, lambda x: x > vh16)

    def mask_lo(c, carry):
        off = pl.multiple_of(c * tkc, tkc)
        sl = pl.ds(off, tkc)
        lo_scr[sl, :] = jnp.where(hi_scr[sl, :] == vh16, lo_scr[sl, :], jnp.int16(HALF_MIN))
        return carry

    lax.fori_loop(0, nch, mask_lo, 0)
    vl = search(lo_scr, need_lo)
    thr_ref[...] = vh * 65536 + (vl - HALF_MIN)


def dsa_indexer(qit, ki, wit, bsz, seq, topk):
    n = bsz * seq
    tq = _tile(seq, 256)
    tkc = _tile(seq, 256)
    nq = seq // tq
    vmem = 2 * (IDX_HEADS * IDX_DIM * tq * 2 + seq * IDX_DIM * 2 + IDX_HEADS * tq * 4 + seq * tq * 4) \
        + seq * tq * 4 + 8 * tkc * tq * 4
    return pl.pallas_call(
        functools.partial(_indexer_kernel, tq=tq, tkc=tkc, topk=topk,
                          wscale=IDX_HEADS ** -0.5 * IDX_DIM ** -0.5),
        grid=(bsz, nq),
        in_specs=[pl.BlockSpec((IDX_HEADS * IDX_DIM, tq), lambda b, i: (0, b * nq + i)),
                  pl.BlockSpec((seq, IDX_DIM), lambda b, i: (b, 0)),
                  pl.BlockSpec((IDX_HEADS, tq), lambda b, i: (0, b * nq + i))],
        out_specs=[pl.BlockSpec((seq, tq), lambda b, i: (b, i)),
                   pl.BlockSpec((1, tq), lambda b, i: (0, b * nq + i))],
        out_shape=[jax.ShapeDtypeStruct((n, seq), F32),
                   jax.ShapeDtypeStruct((1, n), I32)],
        scratch_shapes=[pltpu.VMEM((seq, tq), I16), pltpu.VMEM((seq, tq), I16)],
        compiler_params=_cparams(("parallel", "parallel"), vmem + 4 * MIB),
        name="dsa_indexer",
    )(qit, ki, wit)


def _dsa_attn_kernel(qi_tab, kb_tab, qt_ref, c_ref, ct_ref, sc_ref, thr_ref, bias_ref, wuvt_ref, o_ref,
                     m_scr, l_scr, acc_scr, madd_scr, *, tq, tk, scale):
    s_id = pl.program_id(1)
    i = qi_tab[s_id]
    kb = kb_tab[s_id]
    nh = A_HEADS
    rank = A_KV_RANK

    @pl.when(kb == 0)
    def _():
        m_scr[...] = jnp.full(m_scr.shape, M_INIT, F32)
        l_scr[...] = jnp.zeros(l_scr.shape, F32)
        acc_scr[...] = jnp.zeros(acc_scr.shape, F32)

    key = _sortable_key(sc_ref[...])
    k_idx = kb * tk + lax.broadcasted_iota(I32, (tk, tq), 0)
    q_idx = i * tq + lax.broadcasted_iota(I32, (tk, tq), 1)
    madd_scr[...] = jnp.where(k_idx <= q_idx, jnp.where(key >= thr_ref[...], 0.0, -jnp.inf), -jnp.inf)

    for h in range(nh):
        qh = qt_ref[h * rank:(h + 1) * rank, :]
        s = jnp.dot(c_ref[...], qh, preferred_element_type=F32) * scale + bias_ref[h] + madd_scr[...]
        m_old = m_scr[h]
        m_new = jnp.maximum(m_old, jnp.max(s, axis=0, keepdims=True))
        alpha = jnp.exp2(m_old - m_new)
        p = jnp.exp2(s - m_new)
        l_scr[h] = alpha * l_scr[h] + jnp.sum(p, axis=0, keepdims=True)
        acc_scr[h] = alpha * acc_scr[h] + jnp.dot(ct_ref[...], p.astype(BF16),
                                                  preferred_element_type=F32)
        m_scr[h] = m_new

    @pl.when(kb == (i * tq + tq - 1) // tk)
    def _():
        for h in range(nh):
            o_lat = (acc_scr[h] / l_scr[h]).astype(BF16)
            out_t = jnp.dot(wuvt_ref[h], o_lat, preferred_element_type=F32)
            o_ref[:, h * A_V_DIM:(h + 1) * A_V_DIM] = out_t.T.astype(o_ref.dtype)


def dsa_attention(qt, c, ct, scores_t, thr, bias_a, w_uvt, bsz, seq):
    n = bsz * seq
    tq = _tile(seq, 256)
    tk = tq
    nq, nkb = seq // tq, seq // tk
    bias = _toeplitz_bias(bias_a * LOG2E, tq, tk, transposed=True)
    kinds = bias.shape[0]
    rank = A_KV_RANK
    pairs = [(i, kb) for i in range(nq) for kb in range((i * tq + tq - 1) // tk + 1)]
    qi_tab = jnp.asarray([p[0] for p in pairs], I32)
    kb_tab = jnp.asarray([p[1] for p in pairs], I32)

    def kind(i, kb):
        return jnp.minimum(i - kb * (tk // tq), kinds - 1)

    vmem = 2 * (A_HEADS * rank * tq * 2 + 2 * tk * rank * 2 + tk * tq * 4 + A_HEADS * tk * tq * 4
                + A_HEADS * rank * A_V_DIM * 2 + tq * A_HEADS * A_V_DIM * 2) \
        + A_HEADS * rank * tq * 4 + 10 * tk * tq * 4
    grid_spec = pltpu.PrefetchScalarGridSpec(
        num_scalar_prefetch=2,
        grid=(bsz, len(pairs)),
        in_specs=[pl.BlockSpec((A_HEADS * rank, tq), lambda b, s, qi, kb: (0, b * nq + qi[s])),
                  pl.BlockSpec((tk, rank), lambda b, s, qi, kb: (b * nkb + kb[s], 0)),
                  pl.BlockSpec((rank, tk), lambda b, s, qi, kb: (0, b * nkb + kb[s])),
                  pl.BlockSpec((tk, tq), lambda b, s, qi, kb: (b * nkb + kb[s], qi[s])),
                  pl.BlockSpec((1, tq), lambda b, s, qi, kb: (0, b * nq + qi[s])),
                  pl.BlockSpec((None, A_HEADS, tk, tq), lambda b, s, qi, kb: (kind(qi[s], kb[s]), 0, 0, 0)),
                  pl.BlockSpec((A_HEADS, A_V_DIM, rank), lambda b, s, qi, kb: (0, 0, 0))],
        out_specs=pl.BlockSpec((tq, A_HEADS * A_V_DIM), lambda b, s, qi, kb: (b * nq + qi[s], 0)),
        scratch_shapes=[pltpu.VMEM((A_HEADS, 1, tq), F32),
                        pltpu.VMEM((A_HEADS, 1, tq), F32),
                        pltpu.VMEM((A_HEADS, rank, tq), F32),
                        pltpu.VMEM((tk, tq), F32)])
    return pl.pallas_call(
        functools.partial(_dsa_attn_kernel, tq=tq, tk=tk, scale=rank ** -0.5 * LOG2E),
        grid_spec=grid_spec,
        out_shape=jax.ShapeDtypeStruct((n, A_HEADS * A_V_DIM), BF16),
        compiler_params=_cparams(("parallel", "arbitrary"), vmem + 4 * MIB),
        name="dsa_attention",
    )(qi_tab, kb_tab, qt, c, ct, scores_t, thr, bias, w_uvt)


def _sb_kernel(q_ref, k_ref, v_ref, u_ref, o_ref, carry_scr, acc_scr, *, tq, scale):
    i = pl.program_id(2)
    q = q_ref[...]
    tri = u_ref[...]
    row = lax.broadcasted_iota(I32, (tq, tq), 0)
    col = lax.broadcasted_iota(I32, (tq, tq), 1)
    strict = col < row

    def block(kb, diag):
        off = pl.multiple_of(kb * tq, tq)
        k = k_ref[pl.ds(off, tq), :]
        z = lax.dot_general(q, k, (((1,), (1,)), ((), ())), preferred_element_type=F32) * scale
        log1m = -(jnp.maximum(z, 0.0) + jnp.log(1.0 + jnp.exp(-jnp.abs(z))))
        if diag:
            log1m = jnp.where(strict, log1m, 0.0)
        hi = log1m.astype(BF16)
        lo = (log1m - hi.astype(F32)).astype(BF16)
        between = (jnp.dot(hi, tri, preferred_element_type=F32)
                   + jnp.dot(lo, tri, preferred_element_type=F32))
        logw = z + log1m + between
        return logw, jnp.sum(log1m, axis=-1, keepdims=True), v_ref[pl.ds(off, tq), :]

    has_prev = i > 0
    lw_d, rs_d, v_d = block(i, True)
    lw_p, rs_p, v_p = block(jnp.maximum(i - 1, 0), False)
    w_d = jnp.where(strict, jnp.exp(lw_d), 0.0)
    w_p = jnp.where(has_prev, jnp.exp(lw_p + rs_d), 0.0)
    acc_scr[...] = (jnp.dot(w_d.astype(BF16), v_d, preferred_element_type=F32)
                    + jnp.dot(w_p.astype(BF16), v_p, preferred_element_type=F32))
    carry = rs_d + jnp.where(has_prev, rs_p, 0.0)
    carry_scr[...] = carry

    def cond(state):
        kb, cmax = state
        return jnp.logical_and(kb >= 0, cmax > SB_SKIP)

    def body(state):
        kb, _ = state
        logw, rs, v = block(kb, False)
        carry = carry_scr[...]
        w = jnp.exp(logw + carry)
        acc_scr[...] += jnp.dot(w.astype(BF16), v, preferred_element_type=F32)
        carry = carry + rs
        carry_scr[...] = carry
        return kb - 1, jnp.max(carry)

    lax.while_loop(cond, body, (i - 2, jnp.max(carry)))
    o_ref[...] = acc_scr[...].astype(o_ref.dtype)


def stickbreak_attention(qkv, bsz, seq):
    n = bsz * seq
    tq = _tile(seq, 256)
    nq = seq // tq
    tri = (jnp.arange(tq)[:, None] > jnp.arange(tq)[None, :]).astype(BF16)
    vmem = 2 * (2 * seq * B_DIM * 2 + 2 * tq * B_DIM * 2 + tq * tq * 2) + 12 * tq * tq * 4
    return pl.pallas_call(
        functools.partial(_sb_kernel, tq=tq, scale=B_DIM ** -0.5),
        grid=(bsz, B_HEADS, nq),
        in_specs=[pl.BlockSpec((tq, B_DIM), lambda b, h, i: (b * nq + i, h)),
                  pl.BlockSpec((seq, B_DIM), lambda b, h, i: (b, B_HEADS + h)),
                  pl.BlockSpec((seq, B_DIM), lambda b, h, i: (b, 2 * B_HEADS + h)),
                  pl.BlockSpec((tq, tq), lambda b, h, i: (0, 0))],
        out_specs=pl.BlockSpec((tq, B_DIM), lambda b, h, i: (b * nq + i, h)),
        out_shape=jax.ShapeDtypeStruct((n, B_HEADS * B_DIM), BF16),
        scratch_shapes=[pltpu.VMEM((tq, 1), F32), pltpu.VMEM((tq, B_DIM), F32)],
        compiler_params=_cparams(("parallel", "parallel", "parallel"), vmem + 4 * MIB),
        name="stickbreak_attention",
    )(qkv, qkv, qkv, tri)


def _diff_kernel(qt_ref, k_ref, vt_ref, bias_ref, lam_ref, subg_ref, o_ref, m_scr, l_scr, acc_scr,
                 *, tq, kinds, scale, lambda_init):
    i = pl.program_id(2)
    dq = D_QK_DIM
    lv = lam_ref[...]
    lam = (jnp.exp(jnp.sum(lv[0:1] * lv[1:2], axis=-1, keepdims=True))
           - jnp.exp(jnp.sum(lv[2:3] * lv[3:4], axis=-1, keepdims=True))) + lambda_init
    key_i = lax.broadcasted_iota(I32, (tq, tq), 0)
    qry_i = lax.broadcasted_iota(I32, (tq, tq), 1)
    causal = key_i <= qry_i

    m_scr[...] = jnp.full(m_scr.shape, M_INIT, F32)
    l_scr[...] = jnp.zeros(l_scr.shape, F32)
    acc_scr[...] = jnp.zeros(acc_scr.shape, F32)

    def step(kb, nblk, diag):
        tk = nblk * tq
        off = pl.multiple_of(kb * tq, tq)
        k = k_ref[pl.ds(off, tk), :]
        vt = vt_ref[:, pl.ds(off, tk)]
        bias = jnp.concatenate([bias_ref[jnp.minimum(i - kb - j, kinds - 1)] for j in range(nblk)],
                               axis=0)
        for c in range(2):
            s = jnp.dot(k[:, c * dq:(c + 1) * dq], qt_ref[c * dq:(c + 1) * dq, :],
                        preferred_element_type=F32) * scale + bias
            if diag:
                s = jnp.where(causal, s, -jnp.inf)
            m_old = m_scr[c]
            m_new = jnp.maximum(m_old, jnp.max(s, axis=0, keepdims=True))
            alpha = jnp.exp2(m_old - m_new)
            p = jnp.exp2(s - m_new)
            l_scr[c] = alpha * l_scr[c] + jnp.sum(p, axis=0, keepdims=True)
            acc_scr[c] = alpha * acc_scr[c] + jnp.dot(vt, p.astype(BF16), preferred_element_type=F32)
            m_scr[c] = m_new

    def far_oct(j, carry):
        step(8 * j, 2, False)
        step(8 * j + 2, 2, False)
        step(8 * j + 4, 2, False)
        step(8 * j + 6, 2, False)
        return carry

    lax.fori_loop(0, i // 8, far_oct, 0)

    @pl.when(i % 8 >= 4)
    def _():
        step((i // 8) * 8, 2, False)
        step((i // 8) * 8 + 2, 2, False)

    @pl.when(i % 4 >= 2)
    def _():
        step((i // 4) * 4, 2, False)

    @pl.when(i % 2 == 1)
    def _():
        step(i - 1, 1, False)

    step(i, 1, True)
    out = acc_scr[0] / l_scr[0] - lam * (acc_scr[1] / l_scr[1])
    ms = jnp.mean(out * out, axis=0, keepdims=True)
    out = out * lax.rsqrt(ms + EPS) * subg_ref[...] * (1.0 - lambda_init)
    o_ref[...] = out.T.astype(o_ref.dtype)


def diff_attention(dqt, dk, dvt, bias_d, lam_vecs, sub_g, bsz, seq, lambda_init):
    n = bsz * seq
    tq = _tile(seq, 256)
    nq = seq // tq
    bias = _toeplitz_bias(bias_d * LOG2E, tq, tq, transposed=True)
    kinds = bias.shape[0]
    width = 2 * D_QK_DIM
    vmem = 2 * (tq * width * 2 + seq * width * 2 + seq * D_V_DIM * 2 + kinds * tq * tq * 4
                + tq * D_V_DIM * 2) + 2 * tq * D_V_DIM * 4 + 10 * tq * tq * 4
    return pl.pallas_call(
        functools.partial(_diff_kernel, tq=tq, kinds=kinds, scale=D_QK_DIM ** -0.5 * LOG2E, lambda_init=lambda_init),
        grid=(bsz, D_HEADS, nq),
        in_specs=[pl.BlockSpec((width, tq), lambda b, h, i: (h, b * nq + i)),
                  pl.BlockSpec((seq, width), lambda b, h, i: (b, h)),
                  pl.BlockSpec((D_V_DIM, seq), lambda b, h, i: (h, b)),
                  pl.BlockSpec((kinds, None, tq, tq), lambda b, h, i: (0, h, 0, 0)),
                  pl.BlockSpec((4, D_QK_DIM), lambda b, h, i: (0, 0)),
                  pl.BlockSpec((D_V_DIM, 1), lambda b, h, i: (0, 0))],
        out_specs=pl.BlockSpec((tq, D_V_DIM), lambda b, h, i: (b * nq + i, h)),
        out_shape=jax.ShapeDtypeStruct((n, D_HEADS * D_V_DIM), BF16),
        scratch_shapes=[pltpu.VMEM((2, 1, tq), F32), pltpu.VMEM((2, 1, tq), F32),
                        pltpu.VMEM((2, D_V_DIM, tq), F32)],
        compiler_params=_cparams(("parallel", "parallel", "parallel"), vmem + 4 * MIB),
        name="diff_attention",
    )(dqt, dk, dvt, bias, lam_vecs, sub_g.reshape(D_V_DIM, 1).astype(F32))


def _swa_kernel(q_ref, kp_ref, ko_ref, vp_ref, vo_ref, bias_ref, sink_ref, o_ref, *, blk, scale):
    i = pl.program_id(1)
    left = lax.broadcasted_iota(I32, (blk, LANES), 1) < C_DIM
    qa = lax.broadcasted_iota(I32, (blk, 2 * blk), 0)
    sa = lax.broadcasted_iota(I32, (blk, 2 * blk), 1)
    rel = qa + blk - sa
    lo = jnp.where(i > 0, 0, blk)
    madd = jnp.where(rel >= 0, jnp.where(rel < WINDOW, jnp.where(sa >= lo, 0.0, -jnp.inf), -jnp.inf), -jnp.inf)
    grp = C_HEADS // C_KV_HEADS
    for g in range(C_KV_HEADS):
        ksl = slice(g * LANES, (g + 1) * LANES)
        kk = jnp.concatenate([kp_ref[:, ksl], ko_ref[:, ksl]], axis=0)
        vv = jnp.concatenate([vp_ref[:, ksl], vo_ref[:, ksl]], axis=0)
        for pr in range(grp // 2):
            cb = g * (grp // 2) + pr
            qp = q_ref[:, cb * LANES:(cb + 1) * LANES]
            outs = []
            for half in range(2):
                h = 2 * cb + half
                qh = jnp.where(left if half == 0 else jnp.logical_not(left), qp, jnp.zeros_like(qp))
                lg = lax.dot_general(qh, kk, (((1,), (1,)), ((), ())), preferred_element_type=F32) * scale
                lg = lg + bias_ref[h] + madd
                sink = sink_ref[h]
                m = jnp.maximum(jnp.max(lg, axis=-1, keepdims=True), sink)
                e = jnp.exp(lg - m)
                p = e / (jnp.sum(e, axis=-1, keepdims=True) + jnp.exp(sink - m))
                outs.append(jnp.dot(p.astype(BF16), vv, preferred_element_type=F32))
            o_ref[:, cb * LANES:(cb + 1) * LANES] = jnp.where(left, outs[0], outs[1]).astype(o_ref.dtype)


def swa_attention(cq, ck2, cv2, bias_c, sinks, bsz, seq):
    n = bsz * seq
    blk = WINDOW
    nb = seq // blk
    bias = _toeplitz_tiles(bias_c, [blk], blk, 2 * blk, False)[0]
    qw = C_HEADS * C_DIM
    kw = C_KV_HEADS * LANES
    own = lambda b, i: (b * nb + i, 0)
    prev = lambda b, i: (b * nb + jnp.maximum(i - 1, 0), 0)
    vmem = 2 * (2 * blk * qw * 2 + 4 * blk * kw * 2 + C_HEADS * blk * 2 * blk * 4) + 16 * blk * 2 * blk * 4
    return pl.pallas_call(
        functools.partial(_swa_kernel, blk=blk, scale=C_DIM ** -0.5),
        grid=(bsz, nb),
        in_specs=[pl.BlockSpec((blk, qw), own),
                  pl.BlockSpec((blk, kw), prev), pl.BlockSpec((blk, kw), own),
                  pl.BlockSpec((blk, kw), prev), pl.BlockSpec((blk, kw), own),
                  pl.BlockSpec((C_HEADS, blk, 2 * blk), lambda b, i: (0, 0, 0)),
                  pl.BlockSpec(memory_space=pltpu.SMEM)],
        out_specs=pl.BlockSpec((blk, qw), own),
        out_shape=jax.ShapeDtypeStruct((n, qw), BF16),
        compiler_params=_cparams(("parallel", "parallel"), vmem + 4 * MIB),
        name="swa_attention",
    )(cq, ck2, ck2, cv2, cv2, bias, sinks.astype(F32))


def _rms_rows(x, g):
    return x * lax.rsqrt(jnp.mean(x * x, axis=-1, keepdims=True) + EPS) * g


def _xblock_kernel(x_ref, g1_ref, wq_ref, k_ref, v_ref, qg_ref, kg_ref, wo_ref, g2_ref, o_ref, h_ref, *, scale):
    x = x_ref[...]
    h = _rms_rows(x, g1_ref[...]).astype(BF16)
    q_all = jnp.dot(h, wq_ref[...], preferred_element_type=F32)
    outs = []
    for hd in range(X_HEADS):
        sl = slice(hd * X_DIM, (hd + 1) * X_DIM)
        qn = _rms_rows(q_all[:, sl], qg_ref[...]).astype(BF16)
        kn = _rms_rows(k_ref[:, sl], kg_ref[...]).astype(BF16)
        lg = lax.dot_general(qn, kn, (((1,), (1,)), ((), ())), preferred_element_type=F32) * scale
        m = jnp.max(lg, axis=-1, keepdims=True)
        e = jnp.exp(lg - m)
        p = (e / jnp.sum(e, axis=-1, keepdims=True)).astype(BF16)
        outs.append(jnp.dot(p, v_ref[:, sl], preferred_element_type=F32).astype(BF16))
    o = jnp.concatenate(outs, axis=-1)
    y = x + jnp.dot(o, wo_ref[...], preferred_element_type=F32)
    o_ref[...] = y
    h_ref[...] = _rms_rows(y, g2_ref[...]).astype(h_ref.dtype)


def cross_block(xf, g1, wq, kx, vx, q_g, k_g, wo, g2, bsz, seq, mem_len):
    n, d = xf.shape
    tm = _tile(seq, 256)
    nq = seq // tm
    w = X_HEADS * X_DIM
    row = lambda b, i: (b * nq + i, 0)
    const = lambda b, i: (0, 0)
    vmem = 2 * (tm * d * 10 + 2 * d * w * 2 + mem_len * w * 6) + 6 * tm * d * 4
    return pl.pallas_call(
        functools.partial(_xblock_kernel, scale=X_DIM ** -0.5),
        grid=(bsz, nq),
        in_specs=[pl.BlockSpec((tm, d), row),
                  pl.BlockSpec((1, d), const),
                  pl.BlockSpec((d, w), const),
                  pl.BlockSpec((mem_len, w), lambda b, i: (b, 0)),
                  pl.BlockSpec((mem_len, w), lambda b, i: (b, 0)),
                  pl.BlockSpec((1, X_DIM), const),
                  pl.BlockSpec((1, X_DIM), const),
                  pl.BlockSpec((w, d), const),
                  pl.BlockSpec((1, d), const)],
        out_specs=[pl.BlockSpec((tm, d), row), pl.BlockSpec((tm, d), row)],
        out_shape=[jax.ShapeDtypeStruct((n, d), F32), jax.ShapeDtypeStruct((n, d), BF16)],
        compiler_params=_cparams(("parallel", "parallel"), vmem + 4 * MIB),
        name="cross_block",
    )(xf, g1.reshape(1, d).astype(F32), wq, kx, vx, q_g.reshape(1, X_DIM).astype(F32),
      k_g.reshape(1, X_DIM).astype(F32), wo, g2.reshape(1, d).astype(F32))


def _even_mixer(xf, h, bsz, seq, w_in, q_lat_g, w_qb, q_g, kv_g, w_uv, w_qi, kidx_g, w_out, rel_bias):
    a_cols = A_Q_RANK + A_KV_RANK + IDX_DIM
    a_width = -(-(a_cols + IDX_HEADS) // 256) * 256
    w_a = w_in[:, :a_width].astype(BF16)
    w_b = w_in[:, a_cols + IDX_HEADS:].astype(BF16)
    pa = matmul(h, w_a, out_dtype=F32, tn=256)
    qkv = matmul(h, w_b, out_dtype=BF16, tn=512)

    cq = groupnorm(pa, 0, A_Q_RANK, q_lat_g, A_Q_RANK)
    c, ct = groupnorm(pa, A_Q_RANK, A_KV_RANK, kv_g, A_KV_RANK, layouts=(False, True))
    ki = groupnorm(pa, A_Q_RANK + A_KV_RANK, IDX_DIM, kidx_g, IDX_DIM)
    qt = matmul(cq, w_qb.reshape(A_Q_RANK, A_HEADS * A_KV_RANK).astype(BF16), out_dtype=BF16,
                tn=A_KV_RANK, norm=(A_KV_RANK, q_g), transpose_out=True)
    qit = matmul(cq, w_qi.reshape(A_Q_RANK, IDX_HEADS * IDX_DIM).astype(BF16), out_dtype=BF16,
                 tn=512, transpose_out=True)
    wit = pa[:, a_cols:a_cols + IDX_HEADS].T
    topk = min(TOPK_MAX, seq // 4)
    scores_t, thr = dsa_indexer(qit, ki, wit, bsz, seq, topk)
    o_a = dsa_attention(qt, c, ct, scores_t, thr, rel_bias[:, BIAS_A_OFF:BIAS_A_OFF + A_HEADS],
                        jnp.swapaxes(w_uv, 1, 2).astype(BF16), bsz, seq)
    o_b = stickbreak_attention(qkv, bsz, seq)
    return matmul(o_a, w_out.astype(BF16), a2=o_b, out_dtype=F32, tn=1024, residual=xf)


def _odd_mixer(xf, h, bsz, seq, w_in, c_q_g, c_k_g, sinks, d_q_g, d_k_g, lam_q1, lam_k1, lam_q2, lam_k2,
               sub_g, w_out, rel_bias, lambda_init):
    n = xf.shape[0]
    cw = C_HEADS * C_DIM
    ckw = C_KV_HEADS * C_DIM
    dw = D_HEADS * 2 * D_QK_DIM
    c_end = cw + 2 * ckw
    w_bf = w_in.astype(BF16)
    proj = functools.partial(matmul, h, w_bf, out_dtype=BF16)
    cq = proj(n=cw, norm=(C_DIM, c_q_g), tn=512)
    ck = proj(col0=cw, n=ckw, norm=(C_DIM, c_k_g), tn=256)
    cv = proj(col0=cw + ckw, n=ckw, tn=256)
    dqt = proj(col0=c_end, n=dw, norm=(D_QK_DIM, d_q_g), tn=512, transpose_out=True)
    dk = proj(col0=c_end + dw, n=dw, norm=(D_QK_DIM, d_k_g), tn=512)
    dvt = proj(col0=c_end + 2 * dw, n=D_HEADS * D_V_DIM, tn=512, transpose_out=True)

    dup = lambda a: jnp.concatenate([a.reshape(n, C_KV_HEADS, C_DIM)] * 2, axis=-1).reshape(n, C_KV_HEADS * LANES)
    o_c = swa_attention(cq, dup(ck), dup(cv), rel_bias[:, BIAS_C_OFF:BIAS_C_OFF + C_HEADS], sinks, bsz, seq)

    lam_vecs = jnp.stack([lam_q1, lam_k1, lam_q2, lam_k2]).astype(F32)
    o_d = diff_attention(dqt, dk, dvt, rel_bias[:, BIAS_D_OFF:BIAS_D_OFF + D_HEADS], lam_vecs, sub_g,
                         bsz, seq, lambda_init)
    return matmul(o_c, w_out.astype(BF16), a2=o_d, out_dtype=F32, tn=1024, residual=xf)


def _cross_block(xf, memn, norm_g, wq, wk, wv, q_g, k_g, wo, ffn_norm_g, bsz, seq, mem_len):
    d = xf.shape[1]
    w = X_HEADS * X_DIM
    kx = matmul(memn, wk.reshape(d, w).astype(BF16), out_dtype=F32, tn=512)
    vx = matmul(memn, wv.reshape(d, w).astype(BF16), out_dtype=BF16, tn=512)
    return cross_block(xf, norm_g, wq.reshape(d, w).astype(BF16), kx, vx, q_g, k_g, wo.astype(BF16),
                       ffn_norm_g, bsz, seq, mem_len)


def _ffn_block(xf, h, w_gate, w_up, layer, conv_w, conv_b, w_down, seq):
    act = ffn_gate_up(h, w_gate, w_up, layer, conv_w, conv_b, seq)
    return matmul(act, w_down.astype(BF16), out_dtype=F32, tm=512, tn=512, residual=xf)


def kernel(x, mem, rel_bias, mem_norm_g, mix_norm_g, xattn_norm_g, ffn_norm_g, ev_w_in, ev_q_lat_g, ev_w_qb, ev_q_g, ev_kv_g, ev_w_uv, ev_w_qi, ev_kidx_g, ev_w_out, od_w_in, od_c_q_g, od_c_k_g, od_sinks, od_d_q_g, od_d_k_g, od_lam_q1, od_lam_k1, od_lam_q2, od_lam_k2, od_sub_g, od_w_out, x_wq, x_wk, x_wv, x_q_g, x_k_g, x_wo, f_w_gate, f_w_up, f_conv_w, f_conv_b, f_w_down):
    bsz, seq, d = x.shape
    mem_len = mem.shape[1]
    depth = mix_norm_g.shape[0]
    xf = x.reshape(bsz * seq, d)
    memn = rmsnorm_rows(mem.reshape(bsz * mem_len, d), mem_norm_g)
    for l in range(depth):
        h = rmsnorm_rows(xf, mix_norm_g[l])
        if l % 2 == 0:
            e = l // 2
            xf = _even_mixer(xf, h, bsz, seq, ev_w_in[e], ev_q_lat_g[e], ev_w_qb[e], ev_q_g[e], ev_kv_g[e],
                             ev_w_uv[e], ev_w_qi[e], ev_kidx_g[e], ev_w_out[e], rel_bias)
        else:
            o = l // 2
            lambda_init = 0.8 - 0.6 * math.exp(-0.3 * l)
            xf = _odd_mixer(xf, h, bsz, seq, od_w_in[o], od_c_q_g[o], od_c_k_g[o], od_sinks[o], od_d_q_g[o],
                            od_d_k_g[o], od_lam_q1[o], od_lam_k1[o], od_lam_q2[o], od_lam_k2[o], od_sub_g[o],
                            od_w_out[o], rel_bias, lambda_init)
        xf, h = _cross_block(xf, memn, xattn_norm_g[l], x_wq[l], x_wk[l], x_wv[l], x_q_g[l], x_k_g[l], x_wo[l],
                             ffn_norm_g[l], bsz, seq, mem_len)
        xf = _ffn_block(xf, h, f_w_gate, f_w_up, l, f_conv_w[l], f_conv_b[l], f_w_down[l], seq)
    return xf.reshape(bsz, seq, d)
```

```python
import functools
import math

import jax
import jax.numpy as jnp
from jax import lax
from jax.experimental import pallas as pl
from jax.experimental.pallas import tpu as pltpu

F32 = jnp.float32
BF16 = jnp.bfloat16
I32 = jnp.int32

EPS = 1e-6
LANES = 128
MIB = 1024 * 1024
VMEM_CAP = 58 * MIB
M_INIT = -1e30
INT_MIN = -(2 ** 31)

A_HEADS, A_Q_RANK, A_KV_RANK, A_V_DIM = 16, 1024, 512, 128
IDX_HEADS, IDX_DIM, TOPK_MAX = 32, 128, 256
B_HEADS, B_DIM = 16, 128
C_HEADS, C_KV_HEADS, C_DIM, WINDOW = 32, 4, 64, 128
D_HEADS, D_QK_DIM, D_V_DIM = 8, 128, 256
X_HEADS, X_DIM = 4, 128
CONV_W = 3
NUM_BUCKETS, MAX_EXACT, MAX_DISTANCE = 32, 16, 128
FAR_DIST = 113
BIAS_A_OFF, BIAS_C_OFF, BIAS_D_OFF = 0, A_HEADS, A_HEADS + C_HEADS
LOG2E = math.log2(math.e)
SB_SKIP = -104.0


def _cparams(sem, vmem_bytes):
    return pltpu.CompilerParams(dimension_semantics=sem,
                                vmem_limit_bytes=int(min(max(vmem_bytes, 16 * MIB), VMEM_CAP)))


def _tile(n, pref):
    t = min(n, pref)
    while n % t:
        t //= 2
    return t


def _rmsnorm_kernel(x_ref, g_ref, o_ref):
    x = x_ref[...]
    ms = jnp.mean(x * x, axis=-1, keepdims=True)
    o_ref[...] = (x * lax.rsqrt(ms + EPS) * g_ref[...]).astype(o_ref.dtype)


def rmsnorm_rows(x, g):
    m, d = x.shape
    tm = _tile(m, 256)
    return pl.pallas_call(
        _rmsnorm_kernel,
        grid=(m // tm,),
        in_specs=[pl.BlockSpec((tm, d), lambda i: (i, 0)),
                  pl.BlockSpec((1, d), lambda i: (0, 0))],
        out_specs=pl.BlockSpec((tm, d), lambda i: (i, 0)),
        out_shape=jax.ShapeDtypeStruct((m, d), BF16),
        compiler_params=_cparams(("parallel",), 4 * tm * d * 6),
        name="rmsnorm_rows",
    )(x, g.reshape(1, d).astype(F32))


def _rms_groups(x, g, gd):
    width = x.shape[-1]
    parts = []
    if gd >= LANES:
        for j in range(width // gd):
            xs = x[:, j * gd:(j + 1) * gd]
            ms = jnp.mean(xs * xs, axis=-1, keepdims=True)
            parts.append(xs * lax.rsqrt(ms + EPS) * g)
    else:
        left = lax.broadcasted_iota(I32, (x.shape[0], LANES), 1) < gd
        for j in range(width // LANES):
            xs = x[:, j * LANES:(j + 1) * LANES]
            sq = xs * xs
            tot = jnp.sum(sq, axis=-1, keepdims=True)
            lsum = jnp.sum(jnp.where(left, sq, 0.0), axis=-1, keepdims=True)
            ms = jnp.where(left, lsum, tot - lsum) * (1.0 / gd)
            parts.append(xs * lax.rsqrt(ms + EPS) * g)
    return parts


def _group_gain(gain, gd):
    gw = max(gd, LANES)
    return jnp.tile(gain.astype(F32), gw // gd).reshape(1, gw)


def _groupnorm_kernel(x_ref, g_ref, *o_refs, gd, layouts):
    parts = _rms_groups(x_ref[...].astype(F32), g_ref[...], gd)
    pw = parts[0].shape[-1]
    for o_ref, transposed in zip(o_refs, layouts):
        for j, y in enumerate(parts):
            if transposed:
                o_ref[j * pw:(j + 1) * pw, :] = y.T.astype(o_ref.dtype)
            else:
                o_ref[:, j * pw:(j + 1) * pw] = y.astype(o_ref.dtype)


def groupnorm(x, col0, width, gain, gd, layouts=(False,)):
    m = x.shape[0]
    assert col0 % width == 0 and width % gd == 0 and (gd % LANES == 0 or 2 * gd == LANES)
    tm = _tile(m, 256)
    g = _group_gain(gain, gd)
    gw = g.shape[1]
    cb = col0 // width
    out_specs = [pl.BlockSpec((width, tm), lambda i: (0, i)) if t else pl.BlockSpec((tm, width), lambda i: (i, 0))
                 for t in layouts]
    out_shape = [jax.ShapeDtypeStruct((width, m) if t else (m, width), BF16) for t in layouts]
    outs = pl.pallas_call(
        functools.partial(_groupnorm_kernel, gd=gd, layouts=tuple(layouts)),
        grid=(m // tm,),
        in_specs=[pl.BlockSpec((tm, width), lambda i: (i, cb)),
                  pl.BlockSpec((1, gw), lambda i: (0, 0))],
        out_specs=out_specs,
        out_shape=out_shape,
        compiler_params=_cparams(("parallel",), 4 * tm * width * (8 + 2 * len(layouts))),
        name="groupnorm",
    )(x, g)
    return outs[0] if len(layouts) == 1 else outs


def _mm_kernel(*refs, nk, has_a2, has_res, norm_gd, transpose_out):
    a_ref, w_ref = refs[0], refs[1]
    pos = 2
    a2_ref = w2_ref = res_ref = g_ref = None
    if has_a2:
        a2_ref, w2_ref = refs[pos], refs[pos + 1]
        pos += 2
    if has_res:
        res_ref = refs[pos]
        pos += 1
    if norm_gd:
        g_ref = refs[pos]
        pos += 1
    o_ref = refs[pos]
    acc_ref = refs[pos + 1] if nk > 1 else None

    def epilogue(acc):
        if norm_gd:
            parts = _rms_groups(acc, g_ref[...], norm_gd)
            acc = parts[0] if len(parts) == 1 else jnp.concatenate(parts, axis=-1)
        if has_res:
            acc = acc + res_ref[...]
        acc = acc.astype(o_ref.dtype)
        o_ref[...] = acc.T if transpose_out else acc

    part = jnp.dot(a_ref[...], w_ref[...], preferred_element_type=F32)
    if has_a2:
        part = part + jnp.dot(a2_ref[...], w2_ref[...], preferred_element_type=F32)
    if nk == 1:
        epilogue(part)
    else:
        k = pl.program_id(2)

        @pl.when(k == 0)
        def _():
            acc_ref[...] = part

        @pl.when(k > 0)
        def _():
            acc_ref[...] += part

        @pl.when(k == nk - 1)
        def _():
            epilogue(acc_ref[...])


def matmul(a, w, *, out_dtype, tm=1024, tn=512, tk=None, col0=0, n=None, a2=None, residual=None,
           norm=None, transpose_out=False):
    m, kdim = a.shape
    n = w.shape[1] if n is None else n
    tm = _tile(m, tm)
    tn = _tile(n, tn)
    tk = kdim if tk is None else _tile(kdim, tk)
    nk = kdim // tk
    assert col0 % tn == 0 and (norm is None or tn % max(norm[0], LANES) == 0)
    assert a2 is None or (nk == 1 and a2.shape == a.shape and w.shape[0] == 2 * kdim)
    cb = col0 // tn
    in_specs = [pl.BlockSpec((tm, tk), lambda i, j, k: (i, k)),
                pl.BlockSpec((tk, tn), lambda i, j, k: (k, j + cb))]
    args = [a, w]
    if a2 is not None:
        in_specs += [pl.BlockSpec((tm, tk), lambda i, j, k: (i, 0)),
                     pl.BlockSpec((tk, tn), lambda i, j, k: (1, j + cb))]
        args += [a2, w]
    if residual is not None:
        in_specs.append(pl.BlockSpec((tm, tn), lambda i, j, k: (i, j)))
        args.append(residual)
    if norm is not None:
        gain = _group_gain(norm[1], norm[0])
        in_specs.append(pl.BlockSpec(gain.shape, lambda i, j, k: (0, 0)))
        args.append(gain)
    if transpose_out:
        out_spec = pl.BlockSpec((tn, tm), lambda i, j, k: (j, i))
        out_shape = jax.ShapeDtypeStruct((n, m), out_dtype)
    else:
        out_spec = pl.BlockSpec((tm, tn), lambda i, j, k: (i, j))
        out_shape = jax.ShapeDtypeStruct((m, n), out_dtype)
    osz = jnp.dtype(out_dtype).itemsize
    nops = 2 if a2 is not None else 1
    vmem = 2 * (nops * (tm * tk * 2 + tk * tn * 2) + tm * tn * osz) + tm * tn * 4 * 3
    if residual is not None:
        vmem += 2 * tm * tn * 4
    return pl.pallas_call(
        functools.partial(_mm_kernel, nk=nk, has_a2=a2 is not None, has_res=residual is not None,
                          norm_gd=None if norm is None else norm[0], transpose_out=transpose_out),
        grid=(m // tm, n // tn, nk),
        in_specs=in_specs,
        out_specs=out_spec,
        out_shape=out_shape,
        scratch_shapes=[pltpu.VMEM((tm, tn), F32)] if nk > 1 else [],
        compiler_params=_cparams(("parallel", "parallel", "arbitrary"), vmem + 4 * MIB),
        name="matmul",
    )(*args)


HALO = 16


def _ffn_gu_kernel(a_ref, ah_ref, wg_ref, wu_ref, cw_ref, cb_ref, o_ref, *, tm, seq):
    i = pl.program_id(0)
    a = a_ref[...]
    wg = wg_ref[...].astype(BF16)
    g = jnp.dot(a, wg, preferred_element_type=F32)
    u = jnp.dot(a, wu_ref[...].astype(BF16), preferred_element_type=F32)
    gh = jnp.dot(ah_ref[...], wg, preferred_element_type=F32)
    seq_start = (i * tm) % seq == 0
    gh = jnp.where(seq_start, 0.0, gh)
    rows = lax.broadcasted_iota(I32, g.shape, 0)
    hm1 = gh[HALO - 1:HALO, :]
    hm2 = gh[HALO - 2:HALO - 1, :]
    g1 = jnp.where(rows == 0, hm1, pltpu.roll(g, 1, 0))
    g2 = jnp.where(rows == 0, hm2, jnp.where(rows == 1, hm1, pltpu.roll(g, 2, 0)))
    cw = cw_ref[...]
    c = cw[0:1, :] * g2 + cw[1:2, :] * g1 + cw[2:3, :] * g + cb_ref[...]
    o_ref[...] = (c * jax.nn.sigmoid(c) * u).astype(o_ref.dtype)


def ffn_gate_up(h, wg, wu, layer, conv_w, conv_b, seq):
    m, d = h.shape
    f = wg.shape[2]
    tm = _tile(seq, 1024)
    tn = _tile(f, 256)
    hb = tm // HALO
    vmem = 2 * (tm * d * 2 + HALO * d * 2 + 2 * d * tn * 4 + tm * tn * 2) + 2 * d * tn * 2 + 8 * tm * tn * 4
    return pl.pallas_call(
        functools.partial(_ffn_gu_kernel, tm=tm, seq=seq),
        grid=(m // tm, f // tn),
        in_specs=[pl.BlockSpec((tm, d), lambda i, j: (i, 0)),
                  pl.BlockSpec((HALO, d), lambda i, j: (jnp.maximum(i * hb - 1, 0), 0)),
                  pl.BlockSpec((None, d, tn), lambda i, j: (layer, 0, j)),
                  pl.BlockSpec((None, d, tn), lambda i, j: (layer, 0, j)),
                  pl.BlockSpec((CONV_W, tn), lambda i, j: (0, j)),
                  pl.BlockSpec((1, tn), lambda i, j: (0, j))],
        out_specs=pl.BlockSpec((tm, tn), lambda i, j: (i, j)),
        out_shape=jax.ShapeDtypeStruct((m, f), BF16),
        compiler_params=_cparams(("parallel", "parallel"), vmem + 4 * MIB),
        name="ffn_gate_up",
    )(h, h, wg, wu, conv_w.astype(F32), conv_b.reshape(1, f).astype(F32))


def _t5_bucket(dist):
    n = jnp.maximum(dist, 0)
    nf = jnp.maximum(n, 1).astype(F32)
    large = MAX_EXACT + (jnp.log(nf / MAX_EXACT) / math.log(MAX_DISTANCE / MAX_EXACT)
                         * (NUM_BUCKETS - MAX_EXACT)).astype(I32)
    return jnp.where(n < MAX_EXACT, n, jnp.minimum(large, NUM_BUCKETS - 1))


def _near_kinds(tq, tk):
    return -(-(FAR_DIST - 1 + tk) // tq)


def _toeplitz_tiles(table, d0s, tq, tk, transposed):
    n = tq + tk
    j = jnp.arange(n, dtype=I32)
    if transposed:
        rel, rows, cols = jnp.where(j < tq, j, j - n), tk, tq
    else:
        rel, rows, cols = jnp.where(j < tk, -j, n - j), tq, tk
    d0 = jnp.asarray(d0s, I32)[:, None]
    vals = jnp.moveaxis(table.astype(F32)[_t5_bucket(d0 + rel[None, :])], -1, 1)
    x = jnp.tile(vals, (1, 1, rows))[:, :, :rows * (n - 1)]
    return x.reshape(len(d0s), table.shape[1], rows, n - 1)[..., :cols]


def _toeplitz_bias(table, tq, tk, transposed=False):
    kinds = _near_kinds(tq, tk) + 1
    return _toeplitz_tiles(table, [d * tq for d in range(kinds)], tq, tk, transposed)


def _sortable_key(x):
    b = lax.bitcast_convert_type(x, I32)
    return b ^ ((b >> 31) & jnp.int32(0x7FFFFFFF))


I16 = jnp.int16
HALF_MIN = -(2 ** 15)


def _indexer_kernel(qt_ref, k_ref, wi_ref, sc_ref, thr_ref, neq_ref, hi_scr, lo_scr, *, tq, tkc, topk, wscale):
    i = pl.program_id(1)
    nch = (i * tq + tq + tkc - 1) // tkc
    wi = wi_ref[...] * wscale
    sc_ref[...] = jnp.full(sc_ref.shape, -jnp.inf, F32)
    q_idx = i * tq + lax.broadcasted_iota(I32, (tkc, tq), 1)
    k_loc = lax.broadcasted_iota(I32, (tkc, tq), 0)

    def chunk(c, carry):
        off = pl.multiple_of(c * tkc, tkc)
        kc = k_ref[pl.ds(off, tkc), :]
        s = jnp.zeros((tkc, tq), F32)
        for h in range(IDX_HEADS):
            r = jnp.dot(kc, qt_ref[h * IDX_DIM:(h + 1) * IDX_DIM, :], preferred_element_type=F32)
            s = s + jnp.maximum(r, 0.0) * wi[h:h + 1, :]
        s = jnp.where(off + k_loc <= q_idx, s, -jnp.inf)
        sc_ref[pl.ds(off, tkc), :] = s
        key = _sortable_key(s)
        hi_scr[pl.ds(off, tkc), :] = (key >> 16).astype(I16)
        lo_scr[pl.ds(off, tkc), :] = ((key & 0xFFFF) + HALF_MIN).astype(I16)
        return carry

    lax.fori_loop(0, nch, chunk, 0)

    def count(ref, hit_fn):
        def body(c, acc):
            off = pl.multiple_of(c * tkc, tkc)
            ones = jnp.where(hit_fn(ref[pl.ds(off, tkc), :]), jnp.int16(1), jnp.int16(0))
            for j in range(tkc // 16):
                acc = acc + ones[j * 16:(j + 1) * 16, :]
            return acc

        acc = lax.fori_loop(0, nch, body, jnp.zeros((16, tq), I16))
        return jnp.sum(acc.astype(I32), axis=0, keepdims=True)

    def search(ref, need):
        def bit_step(it, v):
            cand = v + jnp.left_shift(jnp.int32(1), 15 - it)
            c16 = cand.astype(I16)
            cnt = count(ref, lambda x: x >= c16)
            return jnp.where(cnt >= need, cand, v)

        return lax.fori_loop(0, 16, bit_step, jnp.full((1, tq), HALF_MIN, I32))

    vh = search(hi_scr, topk)
    vh16 = vh.astype(I16)
    need_lo = topk - count(hi_scr, lambda x: x > vh16)

    def mask_lo(c, carry):
        off = pl.multiple_of(c * tkc, tkc)
        sl = pl.ds(off, tkc)
        lo_scr[sl, :] = jnp.where(hi_scr[sl, :] == vh16, lo_scr[sl, :], jnp.int16(HALF_MIN))
        return carry

    lax.fori_loop(0, nch, mask_lo, 0)
    vl = search(lo_scr, need_lo)
    vl16 = vl.astype(I16)
    thr_ref[...] = vh * 65536 + (vl - HALF_MIN)
    neq_ref[...] = need_lo - count(lo_scr, lambda x: x > vl16)


def dsa_indexer(qit, ki, wit, bsz, seq, topk):
    n = bsz * seq
    tq = _tile(seq, 256)
    tkc = _tile(seq, 256)
    nq = seq // tq
    vmem = 2 * (IDX_HEADS * IDX_DIM * tq * 2 + seq * IDX_DIM * 2 + IDX_HEADS * tq * 4 + seq * tq * 4) \
        + seq * tq * 4 + 8 * tkc * tq * 4
    return pl.pallas_call(
        functools.partial(_indexer_kernel, tq=tq, tkc=tkc, topk=topk,
                          wscale=IDX_HEADS ** -0.5 * IDX_DIM ** -0.5),
        grid=(bsz, nq),
        in_specs=[pl.BlockSpec((IDX_HEADS * IDX_DIM, tq), lambda b, i: (0, b * nq + i)),
                  pl.BlockSpec((seq, IDX_DIM), lambda b, i: (b, 0)),
                  pl.BlockSpec((IDX_HEADS, tq), lambda b, i: (0, b * nq + i))],
        out_specs=[pl.BlockSpec((seq, tq), lambda b, i: (b, i)),
                   pl.BlockSpec((1, tq), lambda b, i: (0, b * nq + i)),
                   pl.BlockSpec((1, tq), lambda b, i: (0, b * nq + i))],
        out_shape=[jax.ShapeDtypeStruct((n, seq), F32),
                   jax.ShapeDtypeStruct((1, n), I32),
                   jax.ShapeDtypeStruct((1, n), I32)],
        scratch_shapes=[pltpu.VMEM((seq, tq), I16), pltpu.VMEM((seq, tq), I16)],
        compiler_params=_cparams(("parallel", "parallel"), vmem + 4 * MIB),
        name="dsa_indexer",
    )(qit, ki, wit)


def _dsa_attn_kernel(qi_tab, kb_tab, qt_ref, c_ref, ct_ref, sc_ref, thr_ref, neq_ref, low_ref, bias_ref, wuvt_ref,
                     o_ref, m_scr, l_scr, acc_scr, madd_scr, ties_scr, *, tq, tk, scale):
    s_id = pl.program_id(1)
    i = qi_tab[s_id]
    kb = kb_tab[s_id]
    nh = A_HEADS
    rank = A_KV_RANK

    @pl.when(kb == 0)
    def _():
        m_scr[...] = jnp.full(m_scr.shape, M_INIT, F32)
        l_scr[...] = jnp.zeros(l_scr.shape, F32)
        acc_scr[...] = jnp.zeros(acc_scr.shape, F32)
        ties_scr[...] = jnp.zeros(ties_scr.shape, F32)

    key = _sortable_key(sc_ref[...])
    thr = thr_ref[...]
    k_idx = kb * tk + lax.broadcasted_iota(I32, (tk, tq), 0)
    q_idx = i * tq + lax.broadcasted_iota(I32, (tk, tq), 1)
    causal = k_idx <= q_idx
    tie = jnp.where(causal, jnp.where(key == thr, 1.0, 0.0), 0.0)
    earlier = jnp.dot(low_ref[...], tie.astype(BF16), preferred_element_type=F32) + ties_scr[...]
    tie_add = jnp.where(earlier < neq_ref[...].astype(F32), 0.0, -jnp.inf)
    member = jnp.where(key > thr, 0.0, jnp.where(key == thr, tie_add, -jnp.inf))
    madd_scr[...] = jnp.where(causal, member, -jnp.inf)
    ties_scr[...] += jnp.sum(tie, axis=0, keepdims=True)

    for h in range(nh):
        qh = qt_ref[h * rank:(h + 1) * rank, :]
        s = jnp.dot(c_ref[...], qh, preferred_element_type=F32) * scale + bias_ref[h] + madd_scr[...]
        m_old = m_scr[h]
        m_new = jnp.maximum(m_old, jnp.max(s, axis=0, keepdims=True))
        alpha = jnp.exp2(m_old - m_new)
        p = jnp.exp2(s - m_new)
        l_scr[h] = alpha * l_scr[h] + jnp.sum(p, axis=0, keepdims=True)
        acc_scr[h] = alpha * acc_scr[h] + jnp.dot(ct_ref[...], p.astype(BF16),
                                                  preferred_element_type=F32)
        m_scr[h] = m_new

    @pl.when(kb == (i * tq + tq - 1) // tk)
    def _():
        for h in range(nh):
            o_lat = (acc_scr[h] / l_scr[h]).astype(BF16)
            out_t = jnp.dot(wuvt_ref[h], o_lat, preferred_element_type=F32)
            o_ref[:, h * A_V_DIM:(h + 1) * A_V_DIM] = out_t.T.astype(o_ref.dtype)


def dsa_attention(qt, c, ct, scores_t, thr, neq, bias_a, w_uvt, bsz, seq):
    n = bsz * seq
    tq = _tile(seq, 256)
    tk = tq
    nq, nkb = seq // tq, seq // tk
    bias = _toeplitz_bias(bias_a * LOG2E, tq, tk, transposed=True)
    kinds = bias.shape[0]
    rank = A_KV_RANK
    low = (jnp.arange(tk)[None, :] < jnp.arange(tk)[:, None]).astype(BF16)
    pairs = [(i, kb) for i in range(nq) for kb in range((i * tq + tq - 1) // tk + 1)]
    qi_tab = jnp.asarray([p[0] for p in pairs], I32)
    kb_tab = jnp.asarray([p[1] for p in pairs], I32)

    def kind(i, kb):
        return jnp.minimum(i - kb * (tk // tq), kinds - 1)

    vmem = 2 * (A_HEADS * rank * tq * 2 + 2 * tk * rank * 2 + tk * tq * 4 + A_HEADS * tk * tq * 4
                + A_HEADS * rank * A_V_DIM * 2 + tq * A_HEADS * A_V_DIM * 2) \
        + A_HEADS * rank * tq * 4 + 10 * tk * tq * 4
    grid_spec = pltpu.PrefetchScalarGridSpec(
        num_scalar_prefetch=2,
        grid=(bsz, len(pairs)),
        in_specs=[pl.BlockSpec((A_HEADS * rank, tq), lambda b, s, qi, kb: (0, b * nq + qi[s])),
                  pl.BlockSpec((tk, rank), lambda b, s, qi, kb: (b * nkb + kb[s], 0)),
                  pl.BlockSpec((rank, tk), lambda b, s, qi, kb: (0, b * nkb + kb[s])),
                  pl.BlockSpec((tk, tq), lambda b, s, qi, kb: (b * nkb + kb[s], qi[s])),
                  pl.BlockSpec((1, tq), lambda b, s, qi, kb: (0, b * nq + qi[s])),
                  pl.BlockSpec((1, tq), lambda b, s, qi, kb: (0, b * nq + qi[s])),
                  pl.BlockSpec((tk, tk), lambda b, s, qi, kb: (0, 0)),
                  pl.BlockSpec((None, A_HEADS, tk, tq), lambda b, s, qi, kb: (kind(qi[s], kb[s]), 0, 0, 0)),
                  pl.BlockSpec((A_HEADS, A_V_DIM, rank), lambda b, s, qi, kb: (0, 0, 0))],
        out_specs=pl.BlockSpec((tq, A_HEADS * A_V_DIM), lambda b, s, qi, kb: (b * nq + qi[s], 0)),
        scratch_shapes=[pltpu.VMEM((A_HEADS, 1, tq), F32),
                        pltpu.VMEM((A_HEADS, 1, tq), F32),
                        pltpu.VMEM((A_HEADS, rank, tq), F32),
                        pltpu.VMEM((tk, tq), F32),
                        pltpu.VMEM((1, tq), F32)])
    return pl.pallas_call(
        functools.partial(_dsa_attn_kernel, tq=tq, tk=tk, scale=rank ** -0.5 * LOG2E),
        grid_spec=grid_spec,
        out_shape=jax.ShapeDtypeStruct((n, A_HEADS * A_V_DIM), BF16),
        compiler_params=_cparams(("parallel", "arbitrary"), vmem + 4 * MIB),
        name="dsa_attention",
    )(qi_tab, kb_tab, qt, c, ct, scores_t, thr, neq, low, bias, w_uvt)


def _sb_kernel(q_ref, k_ref, v_ref, u_ref, o_ref, carry_scr, acc_scr, *, tq, scale):
    i = pl.program_id(2)
    q = q_ref[...]
    tri = u_ref[...]
    row = lax.broadcasted_iota(I32, (tq, tq), 0)
    col = lax.broadcasted_iota(I32, (tq, tq), 1)
    strict = col < row

    def block(kb, diag):
        off = pl.multiple_of(kb * tq, tq)
        k = k_ref[pl.ds(off, tq), :]
        z = lax.dot_general(q, k, (((1,), (1,)), ((), ())), preferred_element_type=F32) * scale
        log1m = -(jnp.maximum(z, 0.0) + jnp.log(1.0 + jnp.exp(-jnp.abs(z))))
        if diag:
            log1m = jnp.where(strict, log1m, 0.0)
        hi = log1m.astype(BF16)
        lo = (log1m - hi.astype(F32)).astype(BF16)
        between = (jnp.dot(hi, tri, preferred_element_type=F32)
                   + jnp.dot(lo, tri, preferred_element_type=F32))
        logw = z + log1m + between
        return logw, jnp.sum(log1m, axis=-1, keepdims=True), v_ref[pl.ds(off, tq), :]

    has_prev = i > 0
    lw_d, rs_d, v_d = block(i, True)
    lw_p, rs_p, v_p = block(jnp.maximum(i - 1, 0), False)
    w_d = jnp.where(strict, jnp.exp(lw_d), 0.0)
    w_p = jnp.where(has_prev, jnp.exp(lw_p + rs_d), 0.0)
    acc_scr[...] = (jnp.dot(w_d.astype(BF16), v_d, preferred_element_type=F32)
                    + jnp.dot(w_p.astype(BF16), v_p, preferred_element_type=F32))
    carry = rs_d + jnp.where(has_prev, rs_p, 0.0)
    carry_scr[...] = carry

    def cond(state):
        kb, cmax = state
        return jnp.logical_and(kb >= 0, cmax > SB_SKIP)

    def body(state):
        kb, _ = state
        logw, rs, v = block(kb, False)
        carry = carry_scr[...]
        w = jnp.exp(logw + carry)
        acc_scr[...] += jnp.dot(w.astype(BF16), v, preferred_element_type=F32)
        carry = carry + rs
        carry_scr[...] = carry
        return kb - 1, jnp.max(carry)

    lax.while_loop(cond, body, (i - 2, jnp.max(carry)))
    o_ref[...] = acc_scr[...].astype(o_ref.dtype)


def stickbreak_attention(qkv, bsz, seq):
    n = bsz * seq
    tq = _tile(seq, 256)
    nq = seq // tq
    tri = (jnp.arange(tq)[:, None] > jnp.arange(tq)[None, :]).astype(BF16)
    vmem = 2 * (2 * seq * B_DIM * 2 + 2 * tq * B_DIM * 2 + tq * tq * 2) + 12 * tq * tq * 4
    return pl.pallas_call(
        functools.partial(_sb_kernel, tq=tq, scale=B_DIM ** -0.5),
        grid=(bsz, B_HEADS, nq),
        in_specs=[pl.BlockSpec((tq, B_DIM), lambda b, h, i: (b * nq + i, h)),
                  pl.BlockSpec((seq, B_DIM), lambda b, h, i: (b, B_HEADS + h)),
                  pl.BlockSpec((seq, B_DIM), lambda b, h, i: (b, 2 * B_HEADS + h)),
                  pl.BlockSpec((tq, tq), lambda b, h, i: (0, 0))],
        out_specs=pl.BlockSpec((tq, B_DIM), lambda b, h, i: (b * nq + i, h)),
        out_shape=jax.ShapeDtypeStruct((n, B_HEADS * B_DIM), BF16),
        scratch_shapes=[pltpu.VMEM((tq, 1), F32), pltpu.VMEM((tq, B_DIM), F32)],
        compiler_params=_cparams(("parallel", "parallel", "parallel"), vmem + 4 * MIB),
        name="stickbreak_attention",
    )(qkv, qkv, qkv, tri)


def _diff_kernel(qt_ref, k_ref, vt_ref, bias_ref, lam_ref, subg_ref, o_ref, m_scr, l_scr, acc_scr,
                 *, tq, kinds, scale, lambda_init):
    i = pl.program_id(2)
    dq = D_QK_DIM
    lv = lam_ref[...]
    lam = (jnp.exp(jnp.sum(lv[0:1] * lv[1:2], axis=-1, keepdims=True))
           - jnp.exp(jnp.sum(lv[2:3] * lv[3:4], axis=-1, keepdims=True))) + lambda_init
    key_i = lax.broadcasted_iota(I32, (tq, tq), 0)
    qry_i = lax.broadcasted_iota(I32, (tq, tq), 1)
    causal = key_i <= qry_i

    m_scr[...] = jnp.full(m_scr.shape, M_INIT, F32)
    l_scr[...] = jnp.zeros(l_scr.shape, F32)
    acc_scr[...] = jnp.zeros(acc_scr.shape, F32)

    def step(kb, nblk, diag):
        tk = nblk * tq
        off = pl.multiple_of(kb * tq, tq)
        k = k_ref[pl.ds(off, tk), :]
        vt = vt_ref[:, pl.ds(off, tk)]
        bias = jnp.concatenate([bias_ref[jnp.minimum(i - kb - j, kinds - 1)] for j in range(nblk)],
                               axis=0)
        for c in range(2):
            s = jnp.dot(k[:, c * dq:(c + 1) * dq], qt_ref[c * dq:(c + 1) * dq, :],
                        preferred_element_type=F32) * scale + bias
            if diag:
                s = jnp.where(causal, s, -jnp.inf)
            m_old = m_scr[c]
            m_new = jnp.maximum(m_old, jnp.max(s, axis=0, keepdims=True))
            alpha = jnp.exp2(m_old - m_new)
            p = jnp.exp2(s - m_new)
            l_scr[c] = alpha * l_scr[c] + jnp.sum(p, axis=0, keepdims=True)
            acc_scr[c] = alpha * acc_scr[c] + jnp.dot(vt, p.astype(BF16), preferred_element_type=F32)
            m_scr[c] = m_new

    def far_oct(j, carry):
        step(8 * j, 2, False)
        step(8 * j + 2, 2, False)
        step(8 * j + 4, 2, False)
        step(8 * j + 6, 2, False)
        return carry

    lax.fori_loop(0, i // 8, far_oct, 0)

    @pl.when(i % 8 >= 4)
    def _():
        step((i // 8) * 8, 2, False)
        step((i // 8) * 8 + 2, 2, False)

    @pl.when(i % 4 >= 2)
    def _():
        step((i // 4) * 4, 2, False)

    @pl.when(i % 2 == 1)
    def _():
        step(i - 1, 1, False)

    step(i, 1, True)
    out = acc_scr[0] / l_scr[0] - lam * (acc_scr[1] / l_scr[1])
    ms = jnp.mean(out * out, axis=0, keepdims=True)
    out = out * lax.rsqrt(ms + EPS) * subg_ref[...] * (1.0 - lambda_init)
    o_ref[...] = out.T.astype(o_ref.dtype)


def diff_attention(dqt, dk, dvt, bias_d, lam_vecs, sub_g, bsz, seq, lambda_init):
    n = bsz * seq
    tq = _tile(seq, 256)
    nq = seq // tq
    bias = _toeplitz_bias(bias_d * LOG2E, tq, tq, transposed=True)
    kinds = bias.shape[0]
    width = 2 * D_QK_DIM
    vmem = 2 * (tq * width * 2 + seq * width * 2 + seq * D_V_DIM * 2 + kinds * tq * tq * 4
                + tq * D_V_DIM * 2) + 2 * tq * D_V_DIM * 4 + 10 * tq * tq * 4
    return pl.pallas_call(
        functools.partial(_diff_kernel, tq=tq, kinds=kinds, scale=D_QK_DIM ** -0.5 * LOG2E, lambda_init=lambda_init),
        grid=(bsz, D_HEADS, nq),
        in_specs=[pl.BlockSpec((width, tq), lambda b, h, i: (h, b * nq + i)),
                  pl.BlockSpec((seq, width), lambda b, h, i: (b, h)),
                  pl.BlockSpec((D_V_DIM, seq), lambda b, h, i: (h, b)),
                  pl.BlockSpec((kinds, None, tq, tq), lambda b, h, i: (0, h, 0, 0)),
                  pl.BlockSpec((4, D_QK_DIM), lambda b, h, i: (0, 0)),
                  pl.BlockSpec((D_V_DIM, 1), lambda b, h, i: (0, 0))],
        out_specs=pl.BlockSpec((tq, D_V_DIM), lambda b, h, i: (b * nq + i, h)),
        out_shape=jax.ShapeDtypeStruct((n, D_HEADS * D_V_DIM), BF16),
        scratch_shapes=[pltpu.VMEM((2, 1, tq), F32), pltpu.VMEM((2, 1, tq), F32),
                        pltpu.VMEM((2, D_V_DIM, tq), F32)],
        compiler_params=_cparams(("parallel", "parallel", "parallel"), vmem + 4 * MIB),
        name="diff_attention",
    )(dqt, dk, dvt, bias, lam_vecs, sub_g.reshape(D_V_DIM, 1).astype(F32))


def _swa_kernel(q_ref, kp_ref, ko_ref, vp_ref, vo_ref, bias_ref, sink_ref, o_ref, *, blk, scale):
    i = pl.program_id(1)
    left = lax.broadcasted_iota(I32, (blk, LANES), 1) < C_DIM
    qa = lax.broadcasted_iota(I32, (blk, 2 * blk), 0)
    sa = lax.broadcasted_iota(I32, (blk, 2 * blk), 1)
    rel = qa + blk - sa
    lo = jnp.where(i > 0, 0, blk)
    madd = jnp.where(rel >= 0, jnp.where(rel < WINDOW, jnp.where(sa >= lo, 0.0, -jnp.inf), -jnp.inf), -jnp.inf)
    grp = C_HEADS // C_KV_HEADS
    for g in range(C_KV_HEADS):
        ksl = slice(g * LANES, (g + 1) * LANES)
        kk = jnp.concatenate([kp_ref[:, ksl], ko_ref[:, ksl]], axis=0)
        vv = jnp.concatenate([vp_ref[:, ksl], vo_ref[:, ksl]], axis=0)
        for pr in range(grp // 2):
            cb = g * (grp // 2) + pr
            qp = q_ref[:, cb * LANES:(cb + 1) * LANES]
            outs = []
            for half in range(2):
                h = 2 * cb + half
                qh = jnp.where(left if half == 0 else jnp.logical_not(left), qp, jnp.zeros_like(qp))
                lg = lax.dot_general(qh, kk, (((1,), (1,)), ((), ())), preferred_element_type=F32) * scale
                lg = lg + bias_ref[h] + madd
                sink = sink_ref[h]
                m = jnp.maximum(jnp.max(lg, axis=-1, keepdims=True), sink)
                e = jnp.exp(lg - m)
                p = e / (jnp.sum(e, axis=-1, keepdims=True) + jnp.exp(sink - m))
                outs.append(jnp.dot(p.astype(BF16), vv, preferred_element_type=F32))
            o_ref[:, cb * LANES:(cb + 1) * LANES] = jnp.where(left, outs[0], outs[1]).astype(o_ref.dtype)


def swa_attention(cq, ck2, cv2, bias_c, sinks, bsz, seq):
    n = bsz * seq
    blk = WINDOW
    nb = seq // blk
    bias = _toeplitz_tiles(bias_c, [blk], blk, 2 * blk, False)[0]
    qw = C_HEADS * C_DIM
    kw = C_KV_HEADS * LANES
    own = lambda b, i: (b * nb + i, 0)
    prev = lambda b, i: (b * nb + jnp.maximum(i - 1, 0), 0)
    vmem = 2 * (2 * blk * qw * 2 + 4 * blk * kw * 2 + C_HEADS * blk * 2 * blk * 4) + 16 * blk * 2 * blk * 4
    return pl.pallas_call(
        functools.partial(_swa_kernel, blk=blk, scale=C_DIM ** -0.5),
        grid=(bsz, nb),
        in_specs=[pl.BlockSpec((blk, qw), own),
                  pl.BlockSpec((blk, kw), prev), pl.BlockSpec((blk, kw), own),
                  pl.BlockSpec((blk, kw), prev), pl.BlockSpec((blk, kw), own),
                  pl.BlockSpec((C_HEADS, blk, 2 * blk), lambda b, i: (0, 0, 0)),
                  pl.BlockSpec(memory_space=pltpu.SMEM)],
        out_specs=pl.BlockSpec((blk, qw), own),
        out_shape=jax.ShapeDtypeStruct((n, qw), BF16),
        compiler_params=_cparams(("parallel", "parallel"), vmem + 4 * MIB),
        name="swa_attention",
    )(cq, ck2, ck2, cv2, cv2, bias, sinks.astype(F32))


def _rms_rows(x, g):
    return x * lax.rsqrt(jnp.mean(x * x, axis=-1, keepdims=True) + EPS) * g


def _xblock_kernel(x_ref, g1_ref, wq_ref, k_ref, v_ref, qg_ref, kg_ref, wo_ref, g2_ref, o_ref, h_ref, *, scale):
    x = x_ref[...]
    h = _rms_rows(x, g1_ref[...]).astype(BF16)
    q_all = jnp.dot(h, wq_ref[...], preferred_element_type=F32)
    outs = []
    for hd in range(X_HEADS):
        sl = slice(hd * X_DIM, (hd + 1) * X_DIM)
        qn = _rms_rows(q_all[:, sl], qg_ref[...]).astype(BF16)
        kn = _rms_rows(k_ref[:, sl], kg_ref[...]).astype(BF16)
        lg = lax.dot_general(qn, kn, (((1,), (1,)), ((), ())), preferred_element_type=F32) * scale
        m = jnp.max(lg, axis=-1, keepdims=True)
        e = jnp.exp(lg - m)
        p = (e / jnp.sum(e, axis=-1, keepdims=True)).astype(BF16)
        outs.append(jnp.dot(p, v_ref[:, sl], preferred_element_type=F32).astype(BF16))
    o = jnp.concatenate(outs, axis=-1)
    y = x + jnp.dot(o, wo_ref[...], preferred_element_type=F32)
    o_ref[...] = y
    h_ref[...] = _rms_rows(y, g2_ref[...]).astype(h_ref.dtype)


def cross_block(xf, g1, wq, kx, vx, q_g, k_g, wo, g2, bsz, seq, mem_len):
    n, d = xf.shape
    tm = _tile(seq, 256)
    nq = seq // tm
    w = X_HEADS * X_DIM
    row = lambda b, i: (b * nq + i, 0)
    const = lambda b, i: (0, 0)
    vmem = 2 * (tm * d * 10 + 2 * d * w * 2 + mem_len * w * 6) + 6 * tm * d * 4
    return pl.pallas_call(
        functools.partial(_xblock_kernel, scale=X_DIM ** -0.5),
        grid=(bsz, nq),
        in_specs=[pl.BlockSpec((tm, d), row),
                  pl.BlockSpec((1, d), const),
                  pl.BlockSpec((d, w), const),
                  pl.BlockSpec((mem_len, w), lambda b, i: (b, 0)),
                  pl.BlockSpec((mem_len, w), lambda b, i: (b, 0)),
                  pl.BlockSpec((1, X_DIM), const),
                  pl.BlockSpec((1, X_DIM), const),
                  pl.BlockSpec((w, d), const),
                  pl.BlockSpec((1, d), const)],
        out_specs=[pl.BlockSpec((tm, d), row), pl.BlockSpec((tm, d), row)],
        out_shape=[jax.ShapeDtypeStruct((n, d), F32), jax.ShapeDtypeStruct((n, d), BF16)],
        compiler_params=_cparams(("parallel", "parallel"), vmem + 4 * MIB),
        name="cross_block",
    )(xf, g1.reshape(1, d).astype(F32), wq, kx, vx, q_g.reshape(1, X_DIM).astype(F32),
      k_g.reshape(1, X_DIM).astype(F32), wo, g2.reshape(1, d).astype(F32))


def _even_mixer(xf, h, bsz, seq, w_in, q_lat_g, w_qb, q_g, kv_g, w_uv, w_qi, kidx_g, w_out, rel_bias):
    a_cols = A_Q_RANK + A_KV_RANK + IDX_DIM
    a_width = -(-(a_cols + IDX_HEADS) // 256) * 256
    w_a = w_in[:, :a_width].astype(BF16)
    w_b = w_in[:, a_cols + IDX_HEADS:].astype(BF16)
    pa = matmul(h, w_a, out_dtype=F32, tn=256)
    qkv = matmul(h, w_b, out_dtype=BF16, tn=512)

    cq = groupnorm(pa, 0, A_Q_RANK, q_lat_g, A_Q_RANK)
    c, ct = groupnorm(pa, A_Q_RANK, A_KV_RANK, kv_g, A_KV_RANK, layouts=(False, True))
    ki = groupnorm(pa, A_Q_RANK + A_KV_RANK, IDX_DIM, kidx_g, IDX_DIM)
    qt = matmul(cq, w_qb.reshape(A_Q_RANK, A_HEADS * A_KV_RANK).astype(BF16), out_dtype=BF16,
                tn=A_KV_RANK, norm=(A_KV_RANK, q_g), transpose_out=True)
    qit = matmul(cq, w_qi.reshape(A_Q_RANK, IDX_HEADS * IDX_DIM).astype(BF16), out_dtype=BF16,
                 tn=512, transpose_out=True)
    wit = pa[:, a_cols:a_cols + IDX_HEADS].T
    topk = min(TOPK_MAX, seq // 4)
    scores_t, thr, neq = dsa_indexer(qit, ki, wit, bsz, seq, topk)
    o_a = dsa_attention(qt, c, ct, scores_t, thr, neq, rel_bias[:, BIAS_A_OFF:BIAS_A_OFF + A_HEADS],
                        jnp.swapaxes(w_uv, 1, 2).astype(BF16), bsz, seq)
    o_b = stickbreak_attention(qkv, bsz, seq)
    return matmul(o_a, w_out.astype(BF16), a2=o_b, out_dtype=F32, tn=1024, residual=xf)


def _odd_mixer(xf, h, bsz, seq, w_in, c_q_g, c_k_g, sinks, d_q_g, d_k_g, lam_q1, lam_k1, lam_q2, lam_k2,
               sub_g, w_out, rel_bias, lambda_init):
    n = xf.shape[0]
    cw = C_HEADS * C_DIM
    ckw = C_KV_HEADS * C_DIM
    dw = D_HEADS * 2 * D_QK_DIM
    c_end = cw + 2 * ckw
    w_bf = w_in.astype(BF16)
    proj = functools.partial(matmul, h, w_bf, out_dtype=BF16)
    cq = proj(n=cw, norm=(C_DIM, c_q_g), tn=512)
    ck = proj(col0=cw, n=ckw, norm=(C_DIM, c_k_g), tn=256)
    cv = proj(col0=cw + ckw, n=ckw, tn=256)
    dqt = proj(col0=c_end, n=dw, norm=(D_QK_DIM, d_q_g), tn=512, transpose_out=True)
    dk = proj(col0=c_end + dw, n=dw, norm=(D_QK_DIM, d_k_g), tn=512)
    dvt = proj(col0=c_end + 2 * dw, n=D_HEADS * D_V_DIM, tn=512, transpose_out=True)

    dup = lambda a: jnp.concatenate([a.reshape(n, C_KV_HEADS, C_DIM)] * 2, axis=-1).reshape(n, C_KV_HEADS * LANES)
    o_c = swa_attention(cq, dup(ck), dup(cv), rel_bias[:, BIAS_C_OFF:BIAS_C_OFF + C_HEADS], sinks, bsz, seq)

    lam_vecs = jnp.stack([lam_q1, lam_k1, lam_q2, lam_k2]).astype(F32)
    o_d = diff_attention(dqt, dk, dvt, rel_bias[:, BIAS_D_OFF:BIAS_D_OFF + D_HEADS], lam_vecs, sub_g,
                         bsz, seq, lambda_init)
    return matmul(o_c, w_out.astype(BF16), a2=o_d, out_dtype=F32, tn=1024, residual=xf)


def _cross_block(xf, memn, norm_g, wq, wk, wv, q_g, k_g, wo, ffn_norm_g, bsz, seq, mem_len):
    d = xf.shape[1]
    w = X_HEADS * X_DIM
    kx = matmul(memn, wk.reshape(d, w).astype(BF16), out_dtype=F32, tn=512)
    vx = matmul(memn, wv.reshape(d, w).astype(BF16), out_dtype=BF16, tn=512)
    return cross_block(xf, norm_g, wq.reshape(d, w).astype(BF16), kx, vx, q_g, k_g, wo.astype(BF16),
                       ffn_norm_g, bsz, seq, mem_len)


def _ffn_block(xf, h, w_gate, w_up, layer, conv_w, conv_b, w_down, seq):
    act = ffn_gate_up(h, w_gate, w_up, layer, conv_w, conv_b, seq)
    return matmul(act, w_down.astype(BF16), out_dtype=F32, tm=512, tn=512, residual=xf)


def kernel(x, mem, rel_bias, mem_norm_g, mix_norm_g, xattn_norm_g, ffn_norm_g, ev_w_in, ev_q_lat_g, ev_w_qb, ev_q_g, ev_kv_g, ev_w_uv, ev_w_qi, ev_kidx_g, ev_w_out, od_w_in, od_c_q_g, od_c_k_g, od_sinks, od_d_q_g, od_d_k_g, od_lam_q1, od_lam_k1, od_lam_q2, od_lam_k2, od_sub_g, od_w_out, x_wq, x_wk, x_wv, x_q_g, x_k_g, x_wo, f_w_gate, f_w_up, f_conv_w, f_conv_b, f_w_down):
    bsz, seq, d = x.shape
    mem_len = mem.shape[1]
    depth = mix_norm_g.shape[0]
    xf = x.reshape(bsz * seq, d)
    memn = rmsnorm_rows(mem.reshape(bsz * mem_len, d), mem_norm_g)
    for l in range(depth):
        h = rmsnorm_rows(xf, mix_norm_g[l])
        if l % 2 == 0:
            e = l // 2
            xf = _even_mixer(xf, h, bsz, seq, ev_w_in[e], ev_q_lat_g[e], ev_w_qb[e], ev_q_g[e], ev_kv_g[e],
                             ev_w_uv[e], ev_w_qi[e], ev_kidx_g[e], ev_w_out[e], rel_bias)
        else:
            o = l // 2
            lambda_init = 0.8 - 0.6 * math.exp(-0.3 * l)
            xf = _odd_mixer(xf, h, bsz, seq, od_w_in[o], od_c_q_g[o], od_c_k_g[o], od_sinks[o], od_d_q_g[o],
                            od_d_k_g[o], od_lam_q1[o], od_lam_k1[o], od_lam_q2[o], od_lam_k2[o], od_sub_g[o],
                            od_w_out[o], rel_bias, lambda_init)
        xf, h = _cross_block(xf, memn, xattn_norm_g[l], x_wq[l], x_wk[l], x_wv[l], x_q_g[l], x_k_g[l], x_wo[l],
                             ffn_norm_g[l], bsz, seq, mem_len)
        xf = _ffn_block(xf, h, f_w_gate, f_w_up, l, f_conv_w[l], f_conv_b[l], f_w_down[l], seq)
    return xf.reshape(bsz, seq, d)
```

```python
import functools
import math

import jax
import jax.numpy as jnp
from jax import lax
from jax.experimental import pallas as pl
from jax.experimental.pallas import tpu as pltpu

F32 = jnp.float32
BF16 = jnp.bfloat16
I32 = jnp.int32

EPS = 1e-6
LANES = 128
MIB = 1024 * 1024
VMEM_CAP = 58 * MIB
M_INIT = -1e30
INT_MIN = -(2 ** 31)

A_HEADS, A_Q_RANK, A_KV_RANK, A_V_DIM = 16, 1024, 512, 128
IDX_HEADS, IDX_DIM, TOPK_MAX = 32, 128, 256
B_HEADS, B_DIM = 16, 128
C_HEADS, C_KV_HEADS, C_DIM, WINDOW = 32, 4, 64, 128
D_HEADS, D_QK_DIM, D_V_DIM = 8, 128, 256
X_HEADS, X_DIM = 4, 128
CONV_W = 3
NUM_BUCKETS, MAX_EXACT, MAX_DISTANCE = 32, 16, 128
FAR_DIST = 113
BIAS_A_OFF, BIAS_C_OFF, BIAS_D_OFF = 0, A_HEADS, A_HEADS + C_HEADS
LOG2E = math.log2(math.e)
SB_SKIP = -104.0


def _cparams(sem, vmem_bytes):
    return pltpu.CompilerParams(dimension_semantics=sem,
                                vmem_limit_bytes=int(min(max(vmem_bytes, 16 * MIB), VMEM_CAP)))


def _tile(n, pref):
    t = min(n, pref)
    while n % t:
        t //= 2
    return t


def _rmsnorm_kernel(x_ref, g_ref, o_ref):
    x = x_ref[...]
    ms = jnp.mean(x * x, axis=-1, keepdims=True)
    o_ref[...] = (x * lax.rsqrt(ms + EPS) * g_ref[...]).astype(o_ref.dtype)


def rmsnorm_rows(x, g):
    m, d = x.shape
    tm = _tile(m, 256)
    return pl.pallas_call(
        _rmsnorm_kernel,
        grid=(m // tm,),
        in_specs=[pl.BlockSpec((tm, d), lambda i: (i, 0)),
                  pl.BlockSpec((1, d), lambda i: (0, 0))],
        out_specs=pl.BlockSpec((tm, d), lambda i: (i, 0)),
        out_shape=jax.ShapeDtypeStruct((m, d), BF16),
        compiler_params=_cparams(("parallel",), 4 * tm * d * 6),
        name="rmsnorm_rows",
    )(x, g.reshape(1, d).astype(F32))


def _rms_groups(x, g, gd):
    width = x.shape[-1]
    parts = []
    if gd >= LANES:
        for j in range(width // gd):
            xs = x[:, j * gd:(j + 1) * gd]
            ms = jnp.mean(xs * xs, axis=-1, keepdims=True)
            parts.append(xs * lax.rsqrt(ms + EPS) * g)
    else:
        left = lax.broadcasted_iota(I32, (x.shape[0], LANES), 1) < gd
        for j in range(width // LANES):
            xs = x[:, j * LANES:(j + 1) * LANES]
            sq = xs * xs
            tot = jnp.sum(sq, axis=-1, keepdims=True)
            lsum = jnp.sum(jnp.where(left, sq, 0.0), axis=-1, keepdims=True)
            ms = jnp.where(left, lsum, tot - lsum) * (1.0 / gd)
            parts.append(xs * lax.rsqrt(ms + EPS) * g)
    return parts


def _group_gain(gain, gd):
    gw = max(gd, LANES)
    return jnp.tile(gain.astype(F32), gw // gd).reshape(1, gw)


def _groupnorm_kernel(x_ref, g_ref, *o_refs, gd, layouts):
    parts = _rms_groups(x_ref[...].astype(F32), g_ref[...], gd)
    pw = parts[0].shape[-1]
    for o_ref, transposed in zip(o_refs, layouts):
        for j, y in enumerate(parts):
            if transposed:
                o_ref[j * pw:(j + 1) * pw, :] = y.T.astype(o_ref.dtype)
            else:
                o_ref[:, j * pw:(j + 1) * pw] = y.astype(o_ref.dtype)


def groupnorm(x, col0, width, gain, gd, layouts=(False,)):
    m = x.shape[0]
    assert col0 % width == 0 and width % gd == 0 and (gd % LANES == 0 or 2 * gd == LANES)
    tm = _tile(m, 256)
    g = _group_gain(gain, gd)
    gw = g.shape[1]
    cb = col0 // width
    out_specs = [pl.BlockSpec((width, tm), lambda i: (0, i)) if t else pl.BlockSpec((tm, width), lambda i: (i, 0))
                 for t in layouts]
    out_shape = [jax.ShapeDtypeStruct((width, m) if t else (m, width), BF16) for t in layouts]
    outs = pl.pallas_call(
        functools.partial(_groupnorm_kernel, gd=gd, layouts=tuple(layouts)),
        grid=(m // tm,),
        in_specs=[pl.BlockSpec((tm, width), lambda i: (i, cb)),
                  pl.BlockSpec((1, gw), lambda i: (0, 0))],
        out_specs=out_specs,
        out_shape=out_shape,
        compiler_params=_cparams(("parallel",), 4 * tm * width * (8 + 2 * len(layouts))),
        name="groupnorm",
    )(x, g)
    return outs[0] if len(layouts) == 1 else outs


def _mm_kernel(*refs, nk, has_a2, has_res, norm_gd, transpose_out):
    a_ref, w_ref = refs[0], refs[1]
    pos = 2
    a2_ref = w2_ref = res_ref = g_ref = None
    if has_a2:
        a2_ref, w2_ref = refs[pos], refs[pos + 1]
        pos += 2
    if has_res:
        res_ref = refs[pos]
        pos += 1
    if norm_gd:
        g_ref = refs[pos]
        pos += 1
    o_ref = refs[pos]
    acc_ref = refs[pos + 1] if nk > 1 else None

    def epilogue(acc):
        if norm_gd:
            parts = _rms_groups(acc, g_ref[...], norm_gd)
            acc = parts[0] if len(parts) == 1 else jnp.concatenate(parts, axis=-1)
        if has_res:
            acc = acc + res_ref[...]
        acc = acc.astype(o_ref.dtype)
        o_ref[...] = acc.T if transpose_out else acc

    part = jnp.dot(a_ref[...], w_ref[...], preferred_element_type=F32)
    if has_a2:
        part = part + jnp.dot(a2_ref[...], w2_ref[...], preferred_element_type=F32)
    if nk == 1:
        epilogue(part)
    else:
        k = pl.program_id(2)

        @pl.when(k == 0)
        def _():
            acc_ref[...] = part

        @pl.when(k > 0)
        def _():
            acc_ref[...] += part

        @pl.when(k == nk - 1)
        def _():
            epilogue(acc_ref[...])


def matmul(a, w, *, out_dtype, tm=1024, tn=512, tk=None, col0=0, n=None, a2=None, residual=None,
           norm=None, transpose_out=False):
    m, kdim = a.shape
    n = w.shape[1] if n is None else n
    tm = _tile(m, tm)
    tn = _tile(n, tn)
    tk = kdim if tk is None else _tile(kdim, tk)
    nk = kdim // tk
    assert col0 % tn == 0 and (norm is None or tn % max(norm[0], LANES) == 0)
    assert a2 is None or (nk == 1 and a2.shape == a.shape and w.shape[0] == 2 * kdim)
    cb = col0 // tn
    in_specs = [pl.BlockSpec((tm, tk), lambda i, j, k: (i, k)),
                pl.BlockSpec((tk, tn), lambda i, j, k: (k, j + cb))]
    args = [a, w]
    if a2 is not None:
        in_specs += [pl.BlockSpec((tm, tk), lambda i, j, k: (i, 0)),
                     pl.BlockSpec((tk, tn), lambda i, j, k: (1, j + cb))]
        args += [a2, w]
    if residual is not None:
        in_specs.append(pl.BlockSpec((tm, tn), lambda i, j, k: (i, j)))
        args.append(residual)
    if norm is not None:
        gain = _group_gain(norm[1], norm[0])
        in_specs.append(pl.BlockSpec(gain.shape, lambda i, j, k: (0, 0)))
        args.append(gain)
    if transpose_out:
        out_spec = pl.BlockSpec((tn, tm), lambda i, j, k: (j, i))
        out_shape = jax.ShapeDtypeStruct((n, m), out_dtype)
    else:
        out_spec = pl.BlockSpec((tm, tn), lambda i, j, k: (i, j))
        out_shape = jax.ShapeDtypeStruct((m, n), out_dtype)
    osz = jnp.dtype(out_dtype).itemsize
    nops = 2 if a2 is not None else 1
    vmem = 2 * (nops * (tm * tk * 2 + tk * tn * 2) + tm * tn * osz) + tm * tn * 4 * 3
    if residual is not None:
        vmem += 2 * tm * tn * 4
    return pl.pallas_call(
        functools.partial(_mm_kernel, nk=nk, has_a2=a2 is not None, has_res=residual is not None,
                          norm_gd=None if norm is None else norm[0], transpose_out=transpose_out),
        grid=(m // tm, n // tn, nk),
        in_specs=in_specs,
        out_specs=out_spec,
        out_shape=out_shape,
        scratch_shapes=[pltpu.VMEM((tm, tn), F32)] if nk > 1 else [],
        compiler_params=_cparams(("parallel", "parallel", "arbitrary"), vmem + 4 * MIB),
        name="matmul",
    )(*args)


HALO = 16


def _ffn_gu_kernel(a_ref, ah_ref, wg_ref, wu_ref, cw_ref, cb_ref, o_ref, *, tm, seq):
    i = pl.program_id(0)
    a = a_ref[...]
    wg = wg_ref[...].astype(BF16)
    g = jnp.dot(a, wg, preferred_element_type=F32)
    u = jnp.dot(a, wu_ref[...].astype(BF16), preferred_element_type=F32)
    gh = jnp.dot(ah_ref[...], wg, preferred_element_type=F32)
    seq_start = (i * tm) % seq == 0
    gh = jnp.where(seq_start, 0.0, gh)
    rows = lax.broadcasted_iota(I32, g.shape, 0)
    hm1 = gh[HALO - 1:HALO, :]
    hm2 = gh[HALO - 2:HALO - 1, :]
    g1 = jnp.where(rows == 0, hm1, pltpu.roll(g, 1, 0))
    g2 = jnp.where(rows == 0, hm2, jnp.where(rows == 1, hm1, pltpu.roll(g, 2, 0)))
    cw = cw_ref[...]
    c = cw[0:1, :] * g2 + cw[1:2, :] * g1 + cw[2:3, :] * g + cb_ref[...]
    o_ref[...] = (c * jax.nn.sigmoid(c) * u).astype(o_ref.dtype)


def ffn_gate_up(h, wg, wu, layer, conv_w, conv_b, seq):
    m, d = h.shape
    f = wg.shape[2]
    tm = _tile(seq, 1024)
    tn = _tile(f, 256)
    hb = tm // HALO
    vmem = 2 * (tm * d * 2 + HALO * d * 2 + 2 * d * tn * 4 + tm * tn * 2) + 2 * d * tn * 2 + 8 * tm * tn * 4
    return pl.pallas_call(
        functools.partial(_ffn_gu_kernel, tm=tm, seq=seq),
        grid=(m // tm, f // tn),
        in_specs=[pl.BlockSpec((tm, d), lambda i, j: (i, 0)),
                  pl.BlockSpec((HALO, d), lambda i, j: (jnp.maximum(i * hb - 1, 0), 0)),
                  pl.BlockSpec((None, d, tn), lambda i, j: (layer, 0, j)),
                  pl.BlockSpec((None, d, tn), lambda i, j: (layer, 0, j)),
                  pl.BlockSpec((CONV_W, tn), lambda i, j: (0, j)),
                  pl.BlockSpec((1, tn), lambda i, j: (0, j))],
        out_specs=pl.BlockSpec((tm, tn), lambda i, j: (i, j)),
        out_shape=jax.ShapeDtypeStruct((m, f), BF16),
        compiler_params=_cparams(("parallel", "parallel"), vmem + 4 * MIB),
        name="ffn_gate_up",
    )(h, h, wg, wu, conv_w.astype(F32), conv_b.reshape(1, f).astype(F32))


def _t5_bucket(dist):
    n = jnp.maximum(dist, 0)
    nf = jnp.maximum(n, 1).astype(F32)
    large = MAX_EXACT + (jnp.log(nf / MAX_EXACT) / math.log(MAX_DISTANCE / MAX_EXACT)
                         * (NUM_BUCKETS - MAX_EXACT)).astype(I32)
    return jnp.where(n < MAX_EXACT, n, jnp.minimum(large, NUM_BUCKETS - 1))


def _near_kinds(tq, tk):
    return -(-(FAR_DIST - 1 + tk) // tq)


def _toeplitz_tiles(table, d0s, tq, tk, transposed):
    n = tq + tk
    j = jnp.arange(n, dtype=I32)
    if transposed:
        rel, rows, cols = jnp.where(j < tq, j, j - n), tk, tq
    else:
        rel, rows, cols = jnp.where(j < tk, -j, n - j), tq, tk
    d0 = jnp.asarray(d0s, I32)[:, None]
    vals = jnp.moveaxis(table.astype(F32)[_t5_bucket(d0 + rel[None, :])], -1, 1)
    x = jnp.tile(vals, (1, 1, rows))[:, :, :rows * (n - 1)]
    return x.reshape(len(d0s), table.shape[1], rows, n - 1)[..., :cols]


def _toeplitz_bias(table, tq, tk, transposed=False):
    kinds = _near_kinds(tq, tk) + 1
    return _toeplitz_tiles(table, [d * tq for d in range(kinds)], tq, tk, transposed)


def _sortable_key(x):
    b = lax.bitcast_convert_type(x, I32)
    return b ^ ((b >> 31) & jnp.int32(0x7FFFFFFF))


I16 = jnp.int16
HALF_MIN = -(2 ** 15)


def _indexer_kernel(qt_ref, k_ref, wi_ref, sc_ref, thr_ref, neq_ref, hi_scr, lo_scr, *, tq, tkc, topk, wscale):
    i = pl.program_id(1)
    nch = (i * tq + tq + tkc - 1) // tkc
    wi = wi_ref[...] * wscale
    sc_ref[...] = jnp.full(sc_ref.shape, -jnp.inf, F32)
    q_idx = i * tq + lax.broadcasted_iota(I32, (tkc, tq), 1)
    k_loc = lax.broadcasted_iota(I32, (tkc, tq), 0)

    def chunk(c, carry):
        off = pl.multiple_of(c * tkc, tkc)
        kc = k_ref[pl.ds(off, tkc), :]
        s = jnp.zeros((tkc, tq), F32)
        for h in range(IDX_HEADS):
            r = jnp.dot(kc, qt_ref[h * IDX_DIM:(h + 1) * IDX_DIM, :], preferred_element_type=F32)
            s = s + jnp.maximum(r, 0.0) * wi[h:h + 1, :]
        s = jnp.where(off + k_loc <= q_idx, s, -jnp.inf)
        sc_ref[pl.ds(off, tkc), :] = s
        key = _sortable_key(s)
        hi_scr[pl.ds(off, tkc), :] = (key >> 16).astype(I16)
        lo_scr[pl.ds(off, tkc), :] = ((key & 0xFFFF) + HALF_MIN).astype(I16)
        return carry

    lax.fori_loop(0, nch, chunk, 0)

    def count(ref, hit_fn):
        def body(c, acc):
            off = pl.multiple_of(c * tkc, tkc)
            ones = jnp.where(hit_fn(ref[pl.ds(off, tkc), :]), jnp.int16(1), jnp.int16(0))
            for j in range(tkc // 16):
                acc = acc + ones[j * 16:(j + 1) * 16, :]
            return acc

        acc = lax.fori_loop(0, nch, body, jnp.zeros((16, tq), I16))
        return jnp.sum(acc.astype(I32), axis=0, keepdims=True)

    def search(ref, need):
        def bit_step(it, v):
            cand = v + jnp.left_shift(jnp.int32(1), 15 - it)
            c16 = cand.astype(I16)
            cnt = count(ref, lambda x: x >= c16)
            return jnp.where(cnt >= need, cand, v)

        return lax.fori_loop(0, 16, bit_step, jnp.full((1, tq), HALF_MIN, I32))

    vh = search(hi_scr, topk)
    vh16 = vh.astype(I16)
    need_lo = topk - count(hi_scr, lambda x: x > vh16)

    def mask_lo(c, carry):
        off = pl.multiple_of(c * tkc, tkc)
        sl = pl.ds(off, tkc)
        lo_scr[sl, :] = jnp.where(hi_scr[sl, :] == vh16, lo_scr[sl, :], jnp.int16(HALF_MIN))
        return carry

    lax.fori_loop(0, nch, mask_lo, 0)
    vl = search(lo_scr, need_lo)
    vl16 = vl.astype(I16)
    thr_ref[...] = vh * 65536 + (vl - HALF_MIN)
    neq_ref[...] = need_lo - count(lo_scr, lambda x: x > vl16)


def dsa_indexer(qit, ki, wit, bsz, seq, topk):
    n = bsz * seq
    tq = _tile(seq, 256)
    tkc = _tile(seq, 256)
    nq = seq // tq
    vmem = 2 * (IDX_HEADS * IDX_DIM * tq * 2 + seq * IDX_DIM * 2 + IDX_HEADS * tq * 4 + seq * tq * 4) \
        + seq * tq * 4 + 8 * tkc * tq * 4
    return pl.pallas_call(
        functools.partial(_indexer_kernel, tq=tq, tkc=tkc, topk=topk,
                          wscale=IDX_HEADS ** -0.5 * IDX_DIM ** -0.5),
        grid=(bsz, nq),
        in_specs=[pl.BlockSpec((IDX_HEADS * IDX_DIM, tq), lambda b, i: (0, b * nq + i)),
                  pl.BlockSpec((seq, IDX_DIM), lambda b, i: (b, 0)),
                  pl.BlockSpec((IDX_HEADS, tq), lambda b, i: (0, b * nq + i))],
        out_specs=[pl.BlockSpec((seq, tq), lambda b, i: (b, i)),
                   pl.BlockSpec((1, tq), lambda b, i: (0, b * nq + i)),
                   pl.BlockSpec((1, tq), lambda b, i: (0, b * nq + i))],
        out_shape=[jax.ShapeDtypeStruct((n, seq), F32),
                   jax.ShapeDtypeStruct((1, n), I32),
                   jax.ShapeDtypeStruct((1, n), I32)],
        scratch_shapes=[pltpu.VMEM((seq, tq), I16), pltpu.VMEM((seq, tq), I16)],
        compiler_params=_cparams(("parallel", "parallel"), vmem + 4 * MIB),
        name="dsa_indexer",
    )(qit, ki, wit)


def _dsa_attn_kernel(qi_tab, kb_tab, qt_ref, c_ref, ct_ref, sc_ref, thr_ref, neq_ref, low_ref, bias_ref, wuvt_ref,
                     o_ref, m_scr, l_scr, acc_scr, madd_scr, ties_scr, *, tq, tk, scale):
    s_id = pl.program_id(1)
    i = qi_tab[s_id]
    kb = kb_tab[s_id]
    nh = A_HEADS
    rank = A_KV_RANK

    @pl.when(kb == 0)
    def _():
        m_scr[...] = jnp.full(m_scr.shape, M_INIT, F32)
        l_scr[...] = jnp.zeros(l_scr.shape, F32)
        acc_scr[...] = jnp.zeros(acc_scr.shape, F32)
        ties_scr[...] = jnp.zeros(ties_scr.shape, F32)

    key = _sortable_key(sc_ref[...])
    thr = thr_ref[...]
    k_idx = kb * tk + lax.broadcasted_iota(I32, (tk, tq), 0)
    q_idx = i * tq + lax.broadcasted_iota(I32, (tk, tq), 1)
    causal = k_idx <= q_idx
    tie = jnp.where(causal, jnp.where(key == thr, 1.0, 0.0), 0.0)
    earlier = jnp.dot(low_ref[...], tie.astype(BF16), preferred_element_type=F32) + ties_scr[...]
    tie_add = jnp.where(earlier < neq_ref[...].astype(F32), 0.0, -jnp.inf)
    member = jnp.where(key > thr, 0.0, jnp.where(key == thr, tie_add, -jnp.inf))
    madd_scr[...] = jnp.where(causal, member, -jnp.inf)
    ties_scr[...] += jnp.sum(tie, axis=0, keepdims=True)

    for h in range(nh):
        qh = qt_ref[h * rank:(h + 1) * rank, :]
        s = jnp.dot(c_ref[...], qh, preferred_element_type=F32) * scale + bias_ref[h] + madd_scr[...]
        m_old = m_scr[h]
        m_new = jnp.maximum(m_old, jnp.max(s, axis=0, keepdims=True))
        alpha = jnp.exp2(m_old - m_new)
        p = jnp.exp2(s - m_new)
        l_scr[h] = alpha * l_scr[h] + jnp.sum(p, axis=0, keepdims=True)
        acc_scr[h] = alpha * acc_scr[h] + jnp.dot(ct_ref[...], p.astype(BF16),
                                                  preferred_element_type=F32)
        m_scr[h] = m_new

    @pl.when(kb == (i * tq + tq - 1) // tk)
    def _():
        for h in range(nh):
            o_lat = (acc_scr[h] / l_scr[h]).astype(BF16)
            out_t = jnp.dot(wuvt_ref[h], o_lat, preferred_element_type=F32)
            o_ref[:, h * A_V_DIM:(h + 1) * A_V_DIM] = out_t.T.astype(o_ref.dtype)


def dsa_attention(qt, c, ct, scores_t, thr, neq, bias_a, w_uvt, bsz, seq):
    n = bsz * seq
    tq = _tile(seq, 256)
    tk = tq
    nq, nkb = seq // tq, seq // tk
    bias = _toeplitz_bias(bias_a * LOG2E, tq, tk, transposed=True)
    kinds = bias.shape[0]
    rank = A_KV_RANK
    low = (jnp.arange(tk)[None, :] < jnp.arange(tk)[:, None]).astype(BF16)
    pairs = [(i, kb) for i in range(nq) for kb in range((i * tq + tq - 1) // tk + 1)]
    qi_tab = jnp.asarray([p[0] for p in pairs], I32)
    kb_tab = jnp.asarray([p[1] for p in pairs], I32)

    def kind(i, kb):
        return jnp.minimum(i - kb * (tk // tq), kinds - 1)

    vmem = 2 * (A_HEADS * rank * tq * 2 + 2 * tk * rank * 2 + tk * tq * 4 + A_HEADS * tk * tq * 4
                + A_HEADS * rank * A_V_DIM * 2 + tq * A_HEADS * A_V_DIM * 2) \
        + A_HEADS * rank * tq * 4 + 10 * tk * tq * 4
    grid_spec = pltpu.PrefetchScalarGridSpec(
        num_scalar_prefetch=2,
        grid=(bsz, len(pairs)),
        in_specs=[pl.BlockSpec((A_HEADS * rank, tq), lambda b, s, qi, kb: (0, b * nq + qi[s])),
                  pl.BlockSpec((tk, rank), lambda b, s, qi, kb: (b * nkb + kb[s], 0)),
                  pl.BlockSpec((rank, tk), lambda b, s, qi, kb: (0, b * nkb + kb[s])),
                  pl.BlockSpec((tk, tq), lambda b, s, qi, kb: (b * nkb + kb[s], qi[s])),
                  pl.BlockSpec((1, tq), lambda b, s, qi, kb: (0, b * nq + qi[s])),
                  pl.BlockSpec((1, tq), lambda b, s, qi, kb: (0, b * nq + qi[s])),
                  pl.BlockSpec((tk, tk), lambda b, s, qi, kb: (0, 0)),
                  pl.BlockSpec((None, A_HEADS, tk, tq), lambda b, s, qi, kb: (kind(qi[s], kb[s]), 0, 0, 0)),
                  pl.BlockSpec((A_HEADS, A_V_DIM, rank), lambda b, s, qi, kb: (0, 0, 0))],
        out_specs=pl.BlockSpec((tq, A_HEADS * A_V_DIM), lambda b, s, qi, kb: (b * nq + qi[s], 0)),
        scratch_shapes=[pltpu.VMEM((A_HEADS, 1, tq), F32),
                        pltpu.VMEM((A_HEADS, 1, tq), F32),
                        pltpu.VMEM((A_HEADS, rank, tq), F32),
                        pltpu.VMEM((tk, tq), F32),
                        pltpu.VMEM((1, tq), F32)])
    return pl.pallas_call(
        functools.partial(_dsa_attn_kernel, tq=tq, tk=tk, scale=rank ** -0.5 * LOG2E),
        grid_spec=grid_spec,
        out_shape=jax.ShapeDtypeStruct((n, A_HEADS * A_V_DIM), BF16),
        compiler_params=_cparams(("parallel", "arbitrary"), vmem + 4 * MIB),
        name="dsa_attention",
    )(qi_tab, kb_tab, qt, c, ct, scores_t, thr, neq, low, bias, w_uvt)


SB_HEADS_PER_STEP = 4


def _sb_kernel(q_ref, k_ref, v_ref, u_ref, o_ref, carry_scr, acc_scr, *, tq, scale):
    i = pl.program_id(2)
    tri = u_ref[...]
    row = lax.broadcasted_iota(I32, (tq, tq), 0)
    col = lax.broadcasted_iota(I32, (tq, tq), 1)
    strict = col < row
    has_prev = i > 0

    def block(hh, q, kb, diag):
        off = pl.multiple_of(kb * tq, tq)
        hs = slice(hh * B_DIM, (hh + 1) * B_DIM)
        k = k_ref[pl.ds(off, tq), hs]
        z = lax.dot_general(q, k, (((1,), (1,)), ((), ())), preferred_element_type=F32) * scale
        log1m = -(jnp.maximum(z, 0.0) + jnp.log(1.0 + jnp.exp(-jnp.abs(z))))
        if diag:
            log1m = jnp.where(strict, log1m, 0.0)
        hi = log1m.astype(BF16)
        lo = (log1m - hi.astype(F32)).astype(BF16)
        between = (jnp.dot(hi, tri, preferred_element_type=F32)
                   + jnp.dot(lo, tri, preferred_element_type=F32))
        logw = z + log1m + between
        return logw, jnp.sum(log1m, axis=-1, keepdims=True), v_ref[pl.ds(off, tq), hs]

    cmax = []
    for hh in range(SB_HEADS_PER_STEP):
        q = q_ref[:, hh * B_DIM:(hh + 1) * B_DIM]
        lw_d, rs_d, v_d = block(hh, q, i, True)
        lw_p, rs_p, v_p = block(hh, q, jnp.maximum(i - 1, 0), False)
        w_d = jnp.where(strict, jnp.exp(lw_d), 0.0)
        w_p = jnp.where(has_prev, jnp.exp(lw_p + rs_d), 0.0)
        acc_scr[hh] = (jnp.dot(w_d.astype(BF16), v_d, preferred_element_type=F32)
                       + jnp.dot(w_p.astype(BF16), v_p, preferred_element_type=F32))
        carry = rs_d + jnp.where(has_prev, rs_p, 0.0)
        carry_scr[hh] = carry
        cmax.append(jnp.max(carry))

    for hh in range(SB_HEADS_PER_STEP):
        q = q_ref[:, hh * B_DIM:(hh + 1) * B_DIM]

        def cond(state):
            kb, cm = state
            return jnp.logical_and(kb >= 0, cm > SB_SKIP)

        def body(state, hh=hh, q=q):
            kb, _ = state
            logw, rs, v = block(hh, q, kb, False)
            carry = carry_scr[hh]
            w = jnp.exp(logw + carry)
            acc_scr[hh] += jnp.dot(w.astype(BF16), v, preferred_element_type=F32)
            carry = carry + rs
            carry_scr[hh] = carry
            return kb - 1, jnp.max(carry)

        lax.while_loop(cond, body, (i - 2, cmax[hh]))
        o_ref[:, hh * B_DIM:(hh + 1) * B_DIM] = acc_scr[hh].astype(o_ref.dtype)


def stickbreak_attention(qkv, bsz, seq):
    n = bsz * seq
    tq = _tile(seq, 256)
    nq = seq // tq
    hps = SB_HEADS_PER_STEP
    width = hps * B_DIM
    groups = B_HEADS // hps
    tri = (jnp.arange(tq)[:, None] > jnp.arange(tq)[None, :]).astype(BF16)
    vmem = 2 * (2 * seq * width * 2 + 2 * tq * width * 2 + tq * tq * 2) + 24 * tq * tq * 4
    return pl.pallas_call(
        functools.partial(_sb_kernel, tq=tq, scale=B_DIM ** -0.5),
        grid=(bsz, groups, nq),
        in_specs=[pl.BlockSpec((tq, width), lambda b, g, i: (b * nq + i, g)),
                  pl.BlockSpec((seq, width), lambda b, g, i: (b, groups + g)),
                  pl.BlockSpec((seq, width), lambda b, g, i: (b, 2 * groups + g)),
                  pl.BlockSpec((tq, tq), lambda b, g, i: (0, 0))],
        out_specs=pl.BlockSpec((tq, width), lambda b, g, i: (b * nq + i, g)),
        out_shape=jax.ShapeDtypeStruct((n, B_HEADS * B_DIM), BF16),
        scratch_shapes=[pltpu.VMEM((hps, tq, 1), F32), pltpu.VMEM((hps, tq, B_DIM), F32)],
        compiler_params=_cparams(("parallel", "parallel", "parallel"), vmem + 4 * MIB),
        name="stickbreak_attention",
    )(qkv, qkv, qkv, tri)


DIFF_HEADS_PER_STEP = 2


def _diff_kernel(qt_ref, k_ref, vt_ref, bias_ref, lam_ref, subg_ref, o_ref, m_scr, l_scr, acc_scr,
                 *, tq, kinds, scale, lambda_init):
    i = pl.program_id(2)
    dq = D_QK_DIM
    dv = D_V_DIM
    lv = lam_ref[...]
    lam = (jnp.exp(jnp.sum(lv[0:1] * lv[1:2], axis=-1, keepdims=True))
           - jnp.exp(jnp.sum(lv[2:3] * lv[3:4], axis=-1, keepdims=True))) + lambda_init
    key_i = lax.broadcasted_iota(I32, (tq, tq), 0)
    qry_i = lax.broadcasted_iota(I32, (tq, tq), 1)
    causal = key_i <= qry_i

    m_scr[...] = jnp.full(m_scr.shape, M_INIT, F32)
    l_scr[...] = jnp.zeros(l_scr.shape, F32)
    acc_scr[...] = jnp.zeros(acc_scr.shape, F32)

    def step(kb, nblk, diag):
        tk = nblk * tq
        off = pl.multiple_of(kb * tq, tq)
        for hh in range(DIFF_HEADS_PER_STEP):
            bias = jnp.concatenate([bias_ref[jnp.minimum(i - kb - j, kinds - 1), hh] for j in range(nblk)],
                                   axis=0)
            vt = vt_ref[hh * dv:(hh + 1) * dv, pl.ds(off, tk)]
            for c in range(2):
                col = (2 * hh + c) * dq
                s = jnp.dot(k_ref[pl.ds(off, tk), col:col + dq], qt_ref[col:col + dq, :],
                            preferred_element_type=F32) * scale + bias
                if diag:
                    s = jnp.where(causal, s, -jnp.inf)
                u = 2 * hh + c
                m_old = m_scr[u]
                m_new = jnp.maximum(m_old, jnp.max(s, axis=0, keepdims=True))
                alpha = jnp.exp2(m_old - m_new)
                p = jnp.exp2(s - m_new)
                l_scr[u] = alpha * l_scr[u] + jnp.sum(p, axis=0, keepdims=True)
                acc_scr[u] = alpha * acc_scr[u] + jnp.dot(vt, p.astype(BF16), preferred_element_type=F32)
                m_scr[u] = m_new

    def far_oct(j, carry):
        step(8 * j, 2, False)
        step(8 * j + 2, 2, False)
        step(8 * j + 4, 2, False)
        step(8 * j + 6, 2, False)
        return carry

    lax.fori_loop(0, i // 8, far_oct, 0)

    @pl.when(i % 8 >= 4)
    def _():
        step((i // 8) * 8, 2, False)
        step((i // 8) * 8 + 2, 2, False)

    @pl.when(i % 4 >= 2)
    def _():
        step((i // 4) * 4, 2, False)

    @pl.when(i % 2 == 1)
    def _():
        step(i - 1, 1, False)

    step(i, 1, True)
    for hh in range(DIFF_HEADS_PER_STEP):
        out = acc_scr[2 * hh] / l_scr[2 * hh] - lam * (acc_scr[2 * hh + 1] / l_scr[2 * hh + 1])
        ms = jnp.mean(out * out, axis=0, keepdims=True)
        out = out * lax.rsqrt(ms + EPS) * subg_ref[...] * (1.0 - lambda_init)
        o_ref[:, hh * dv:(hh + 1) * dv] = out.T.astype(o_ref.dtype)


def diff_attention(dqt, dk, dvt, bias_d, lam_vecs, sub_g, bsz, seq, lambda_init):
    n = bsz * seq
    tq = _tile(seq, 256)
    nq = seq // tq
    hps = DIFF_HEADS_PER_STEP
    bias = _toeplitz_bias(bias_d * LOG2E, tq, tq, transposed=True)
    kinds = bias.shape[0]
    width = hps * 2 * D_QK_DIM
    vw = hps * D_V_DIM
    vmem = 2 * (tq * width * 2 + seq * width * 2 + seq * vw * 2 + kinds * hps * tq * tq * 4
                + tq * vw * 2) + 2 * hps * tq * D_V_DIM * 4 + 16 * tq * tq * 4
    return pl.pallas_call(
        functools.partial(_diff_kernel, tq=tq, kinds=kinds, scale=D_QK_DIM ** -0.5 * LOG2E, lambda_init=lambda_init),
        grid=(bsz, D_HEADS // hps, nq),
        in_specs=[pl.BlockSpec((width, tq), lambda b, g, i: (g, b * nq + i)),
                  pl.BlockSpec((seq, width), lambda b, g, i: (b, g)),
                  pl.BlockSpec((vw, seq), lambda b, g, i: (g, b)),
                  pl.BlockSpec((kinds, hps, tq, tq), lambda b, g, i: (0, g, 0, 0)),
                  pl.BlockSpec((4, D_QK_DIM), lambda b, g, i: (0, 0)),
                  pl.BlockSpec((D_V_DIM, 1), lambda b, g, i: (0, 0))],
        out_specs=pl.BlockSpec((tq, vw), lambda b, g, i: (b * nq + i, g)),
        out_shape=jax.ShapeDtypeStruct((n, D_HEADS * D_V_DIM), BF16),
        scratch_shapes=[pltpu.VMEM((2 * hps, 1, tq), F32), pltpu.VMEM((2 * hps, 1, tq), F32),
                        pltpu.VMEM((2 * hps, D_V_DIM, tq), F32)],
        compiler_params=_cparams(("parallel", "parallel", "parallel"), vmem + 4 * MIB),
        name="diff_attention",
    )(dqt, dk, dvt, bias, lam_vecs, sub_g.reshape(D_V_DIM, 1).astype(F32))


def _swa_kernel(q_ref, kp_ref, ko_ref, vp_ref, vo_ref, bias_ref, sink_ref, o_ref, *, blk, scale):
    i = pl.program_id(1)
    left = lax.broadcasted_iota(I32, (blk, LANES), 1) < C_DIM
    qa = lax.broadcasted_iota(I32, (blk, 2 * blk), 0)
    sa = lax.broadcasted_iota(I32, (blk, 2 * blk), 1)
    rel = qa + blk - sa
    lo = jnp.where(i > 0, 0, blk)
    madd = jnp.where(rel >= 0, jnp.where(rel < WINDOW, jnp.where(sa >= lo, 0.0, -jnp.inf), -jnp.inf), -jnp.inf)
    grp = C_HEADS // C_KV_HEADS
    for g in range(C_KV_HEADS):
        ksl = slice(g * LANES, (g + 1) * LANES)
        kk = jnp.concatenate([kp_ref[:, ksl], ko_ref[:, ksl]], axis=0)
        vv = jnp.concatenate([vp_ref[:, ksl], vo_ref[:, ksl]], axis=0)
        for pr in range(grp // 2):
            cb = g * (grp // 2) + pr
            qp = q_ref[:, cb * LANES:(cb + 1) * LANES]
            outs = []
            for half in range(2):
                h = 2 * cb + half
                qh = jnp.where(left if half == 0 else jnp.logical_not(left), qp, jnp.zeros_like(qp))
                lg = lax.dot_general(qh, kk, (((1,), (1,)), ((), ())), preferred_element_type=F32) * scale
                lg = lg + bias_ref[h] + madd
                sink = sink_ref[h]
                m = jnp.maximum(jnp.max(lg, axis=-1, keepdims=True), sink)
                e = jnp.exp(lg - m)
                p = e / (jnp.sum(e, axis=-1, keepdims=True) + jnp.exp(sink - m))
                outs.append(jnp.dot(p.astype(BF16), vv, preferred_element_type=F32))
            o_ref[:, cb * LANES:(cb + 1) * LANES] = jnp.where(left, outs[0], outs[1]).astype(o_ref.dtype)


def swa_attention(cq, ck2, cv2, bias_c, sinks, bsz, seq):
    n = bsz * seq
    blk = WINDOW
    nb = seq // blk
    bias = _toeplitz_tiles(bias_c, [blk], blk, 2 * blk, False)[0]
    qw = C_HEADS * C_DIM
    kw = C_KV_HEADS * LANES
    own = lambda b, i: (b * nb + i, 0)
    prev = lambda b, i: (b * nb + jnp.maximum(i - 1, 0), 0)
    vmem = 2 * (2 * blk * qw * 2 + 4 * blk * kw * 2 + C_HEADS * blk * 2 * blk * 4) + 16 * blk * 2 * blk * 4
    return pl.pallas_call(
        functools.partial(_swa_kernel, blk=blk, scale=C_DIM ** -0.5),
        grid=(bsz, nb),
        in_specs=[pl.BlockSpec((blk, qw), own),
                  pl.BlockSpec((blk, kw), prev), pl.BlockSpec((blk, kw), own),
                  pl.BlockSpec((blk, kw), prev), pl.BlockSpec((blk, kw), own),
                  pl.BlockSpec((C_HEADS, blk, 2 * blk), lambda b, i: (0, 0, 0)),
                  pl.BlockSpec(memory_space=pltpu.SMEM)],
        out_specs=pl.BlockSpec((blk, qw), own),
        out_shape=jax.ShapeDtypeStruct((n, qw), BF16),
        compiler_params=_cparams(("parallel", "parallel"), vmem + 4 * MIB),
        name="swa_attention",
    )(cq, ck2, ck2, cv2, cv2, bias, sinks.astype(F32))


def _rms_rows(x, g):
    return x * lax.rsqrt(jnp.mean(x * x, axis=-1, keepdims=True) + EPS) * g


def _xblock_kernel(x_ref, g1_ref, wq_ref, k_ref, v_ref, qg_ref, kg_ref, wo_ref, g2_ref, o_ref, h_ref, *, scale):
    x = x_ref[...]
    h = _rms_rows(x, g1_ref[...]).astype(BF16)
    q_all = jnp.dot(h, wq_ref[...], preferred_element_type=F32)
    outs = []
    for hd in range(X_HEADS):
        sl = slice(hd * X_DIM, (hd + 1) * X_DIM)
        qn = _rms_rows(q_all[:, sl], qg_ref[...]).astype(BF16)
        kn = _rms_rows(k_ref[:, sl], kg_ref[...]).astype(BF16)
        lg = lax.dot_general(qn, kn, (((1,), (1,)), ((), ())), preferred_element_type=F32) * scale
        m = jnp.max(lg, axis=-1, keepdims=True)
        e = jnp.exp(lg - m)
        p = (e / jnp.sum(e, axis=-1, keepdims=True)).astype(BF16)
        outs.append(jnp.dot(p, v_ref[:, sl], preferred_element_type=F32).astype(BF16))
    o = jnp.concatenate(outs, axis=-1)
    y = x + jnp.dot(o, wo_ref[...], preferred_element_type=F32)
    o_ref[...] = y
    h_ref[...] = _rms_rows(y, g2_ref[...]).astype(h_ref.dtype)


def cross_block(xf, g1, wq, kx, vx, q_g, k_g, wo, g2, bsz, seq, mem_len):
    n, d = xf.shape
    tm = _tile(seq, 256)
    nq = seq // tm
    w = X_HEADS * X_DIM
    row = lambda b, i: (b * nq + i, 0)
    const = lambda b, i: (0, 0)
    vmem = 2 * (tm * d * 10 + 2 * d * w * 2 + mem_len * w * 6) + 6 * tm * d * 4
    return pl.pallas_call(
        functools.partial(_xblock_kernel, scale=X_DIM ** -0.5),
        grid=(bsz, nq),
        in_specs=[pl.BlockSpec((tm, d), row),
                  pl.BlockSpec((1, d), const),
                  pl.BlockSpec((d, w), const),
                  pl.BlockSpec((mem_len, w), lambda b, i: (b, 0)),
                  pl.BlockSpec((mem_len, w), lambda b, i: (b, 0)),
                  pl.BlockSpec((1, X_DIM), const),
                  pl.BlockSpec((1, X_DIM), const),
                  pl.BlockSpec((w, d), const),
                  pl.BlockSpec((1, d), const)],
        out_specs=[pl.BlockSpec((tm, d), row), pl.BlockSpec((tm, d), row)],
        out_shape=[jax.ShapeDtypeStruct((n, d), F32), jax.ShapeDtypeStruct((n, d), BF16)],
        compiler_params=_cparams(("parallel", "parallel"), vmem + 4 * MIB),
        name="cross_block",
    )(xf, g1.reshape(1, d).astype(F32), wq, kx, vx, q_g.reshape(1, X_DIM).astype(F32),
      k_g.reshape(1, X_DIM).astype(F32), wo, g2.reshape(1, d).astype(F32))


def _even_mixer(xf, h, bsz, seq, w_in, q_lat_g, w_qb, q_g, kv_g, w_uv, w_qi, kidx_g, w_out, rel_bias):
    a_cols = A_Q_RANK + A_KV_RANK + IDX_DIM
    a_width = -(-(a_cols + IDX_HEADS) // 256) * 256
    w_a = w_in[:, :a_width].astype(BF16)
    w_b = w_in[:, a_cols + IDX_HEADS:].astype(BF16)
    pa = matmul(h, w_a, out_dtype=F32, tn=256)
    qkv = matmul(h, w_b, out_dtype=BF16, tn=512)

    cq = groupnorm(pa, 0, A_Q_RANK, q_lat_g, A_Q_RANK)
    c, ct = groupnorm(pa, A_Q_RANK, A_KV_RANK, kv_g, A_KV_RANK, layouts=(False, True))
    ki = groupnorm(pa, A_Q_RANK + A_KV_RANK, IDX_DIM, kidx_g, IDX_DIM)
    qt = matmul(cq, w_qb.reshape(A_Q_RANK, A_HEADS * A_KV_RANK).astype(BF16), out_dtype=BF16,
                tn=A_KV_RANK, norm=(A_KV_RANK, q_g), transpose_out=True)
    qit = matmul(cq, w_qi.reshape(A_Q_RANK, IDX_HEADS * IDX_DIM).astype(BF16), out_dtype=BF16,
                 tn=512, transpose_out=True)
    wit = pa[:, a_cols:a_cols + IDX_HEADS].T
    topk = min(TOPK_MAX, seq // 4)
    scores_t, thr, neq = dsa_indexer(qit, ki, wit, bsz, seq, topk)
    o_a = dsa_attention(qt, c, ct, scores_t, thr, neq, rel_bias[:, BIAS_A_OFF:BIAS_A_OFF + A_HEADS],
                        jnp.swapaxes(w_uv, 1, 2).astype(BF16), bsz, seq)
    o_b = stickbreak_attention(qkv, bsz, seq)
    return matmul(o_a, w_out.astype(BF16), a2=o_b, out_dtype=F32, tn=1024, residual=xf)


def _odd_mixer(xf, h, bsz, seq, w_in, c_q_g, c_k_g, sinks, d_q_g, d_k_g, lam_q1, lam_k1, lam_q2, lam_k2,
               sub_g, w_out, rel_bias, lambda_init):
    n = xf.shape[0]
    cw = C_HEADS * C_DIM
    ckw = C_KV_HEADS * C_DIM
    dw = D_HEADS * 2 * D_QK_DIM
    c_end = cw + 2 * ckw
    w_bf = w_in.astype(BF16)
    proj = functools.partial(matmul, h, w_bf, out_dtype=BF16)
    cq = proj(n=cw, norm=(C_DIM, c_q_g), tn=512)
    ck = proj(col0=cw, n=ckw, norm=(C_DIM, c_k_g), tn=256)
    cv = proj(col0=cw + ckw, n=ckw, tn=256)
    dqt = proj(col0=c_end, n=dw, norm=(D_QK_DIM, d_q_g), tn=512, transpose_out=True)
    dk = proj(col0=c_end + dw, n=dw, norm=(D_QK_DIM, d_k_g), tn=512)
    dvt = proj(col0=c_end + 2 * dw, n=D_HEADS * D_V_DIM, tn=512, transpose_out=True)

    dup = lambda a: jnp.concatenate([a.reshape(n, C_KV_HEADS, C_DIM)] * 2, axis=-1).reshape(n, C_KV_HEADS * LANES)
    o_c = swa_attention(cq, dup(ck), dup(cv), rel_bias[:, BIAS_C_OFF:BIAS_C_OFF + C_HEADS], sinks, bsz, seq)

    lam_vecs = jnp.stack([lam_q1, lam_k1, lam_q2, lam_k2]).astype(F32)
    o_d = diff_attention(dqt, dk, dvt, rel_bias[:, BIAS_D_OFF:BIAS_D_OFF + D_HEADS], lam_vecs, sub_g,
                         bsz, seq, lambda_init)
    return matmul(o_c, w_out.astype(BF16), a2=o_d, out_dtype=F32, tn=1024, residual=xf)


def _cross_block(xf, memn, norm_g, wq, wk, wv, q_g, k_g, wo, ffn_norm_g, bsz, seq, mem_len):
    d = xf.shape[1]
    w = X_HEADS * X_DIM
    kx = matmul(memn, wk.reshape(d, w).astype(BF16), out_dtype=F32, tn=512)
    vx = matmul(memn, wv.reshape(d, w).astype(BF16), out_dtype=BF16, tn=512)
    return cross_block(xf, norm_g, wq.reshape(d, w).astype(BF16), kx, vx, q_g, k_g, wo.astype(BF16),
                       ffn_norm_g, bsz, seq, mem_len)


def _ffn_block(xf, h, w_gate, w_up, layer, conv_w, conv_b, w_down, seq):
    act = ffn_gate_up(h, w_gate, w_up, layer, conv_w, conv_b, seq)
    return matmul(act, w_down.astype(BF16), out_dtype=F32, tm=512, tn=512, residual=xf)


def kernel(x, mem, rel_bias, mem_norm_g, mix_norm_g, xattn_norm_g, ffn_norm_g, ev_w_in, ev_q_lat_g, ev_w_qb, ev_q_g, ev_kv_g, ev_w_uv, ev_w_qi, ev_kidx_g, ev_w_out, od_w_in, od_c_q_g, od_c_k_g, od_sinks, od_d_q_g, od_d_k_g, od_lam_q1, od_lam_k1, od_lam_q2, od_lam_k2, od_sub_g, od_w_out, x_wq, x_wk, x_wv, x_q_g, x_k_g, x_wo, f_w_gate, f_w_up, f_conv_w, f_conv_b, f_w_down):
    bsz, seq, d = x.shape
    mem_len = mem.shape[1]
    depth = mix_norm_g.shape[0]
    xf = x.reshape(bsz * seq, d)
    memn = rmsnorm_rows(mem.reshape(bsz * mem_len, d), mem_norm_g)
    for l in range(depth):
        h = rmsnorm_rows(xf, mix_norm_g[l])
        if l % 2 == 0:
            e = l // 2
            xf = _even_mixer(xf, h, bsz, seq, ev_w_in[e], ev_q_lat_g[e], ev_w_qb[e], ev_q_g[e], ev_kv_g[e],
                             ev_w_uv[e], ev_w_qi[e], ev_kidx_g[e], ev_w_out[e], rel_bias)
        else:
            o = l // 2
            lambda_init = 0.8 - 0.6 * math.exp(-0.3 * l)
            xf = _odd_mixer(xf, h, bsz, seq, od_w_in[o], od_c_q_g[o], od_c_k_g[o], od_sinks[o], od_d_q_g[o],
                            od_d_k_g[o], od_lam_q1[o], od_lam_k1[o], od_lam_q2[o], od_lam_k2[o], od_sub_g[o],
                            od_w_out[o], rel_bias, lambda_init)
        xf, h = _cross_block(xf, memn, xattn_norm_g[l], x_wq[l], x_wk[l], x_wv[l], x_q_g[l], x_k_g[l], x_wo[l],
                             ffn_norm_g[l], bsz, seq, mem_len)
        xf = _ffn_block(xf, h, f_w_gate, f_w_up, l, f_conv_w[l], f_conv_b[l], f_w_down[l], seq)
    return xf.reshape(bsz, seq, d)
```

```python
import functools
import math

import jax
import jax.numpy as jnp
from jax import lax
from jax.experimental import pallas as pl
from jax.experimental.pallas import tpu as pltpu

F32 = jnp.float32
BF16 = jnp.bfloat16
I32 = jnp.int32

EPS = 1e-6
LANES = 128
MIB = 1024 * 1024
VMEM_CAP = 58 * MIB
M_INIT = -1e30
INT_MIN = -(2 ** 31)

A_HEADS, A_Q_RANK, A_KV_RANK, A_V_DIM = 16, 1024, 512, 128
IDX_HEADS, IDX_DIM, TOPK_MAX = 32, 128, 256
B_HEADS, B_DIM = 16, 128
C_HEADS, C_KV_HEADS, C_DIM, WINDOW = 32, 4, 64, 128
D_HEADS, D_QK_DIM, D_V_DIM = 8, 128, 256
X_HEADS, X_DIM = 4, 128
CONV_W = 3
NUM_BUCKETS, MAX_EXACT, MAX_DISTANCE = 32, 16, 128
FAR_DIST = 113
BIAS_A_OFF, BIAS_C_OFF, BIAS_D_OFF = 0, A_HEADS, A_HEADS + C_HEADS
LOG2E = math.log2(math.e)
SB_SKIP = -104.0


def _cparams(sem, vmem_bytes):
    return pltpu.CompilerParams(dimension_semantics=sem,
                                vmem_limit_bytes=int(min(max(vmem_bytes, 16 * MIB), VMEM_CAP)))


def _tile(n, pref):
    t = min(n, pref)
    while n % t:
        t //= 2
    return t


def _rmsnorm_kernel(x_ref, g_ref, o_ref):
    x = x_ref[...]
    ms = jnp.mean(x * x, axis=-1, keepdims=True)
    o_ref[...] = (x * lax.rsqrt(ms + EPS) * g_ref[...]).astype(o_ref.dtype)


def rmsnorm_rows(x, g):
    m, d = x.shape
    tm = _tile(m, 256)
    return pl.pallas_call(
        _rmsnorm_kernel,
        grid=(m // tm,),
        in_specs=[pl.BlockSpec((tm, d), lambda i: (i, 0)),
                  pl.BlockSpec((1, d), lambda i: (0, 0))],
        out_specs=pl.BlockSpec((tm, d), lambda i: (i, 0)),
        out_shape=jax.ShapeDtypeStruct((m, d), BF16),
        compiler_params=_cparams(("parallel",), 4 * tm * d * 6),
        name="rmsnorm_rows",
    )(x, g.reshape(1, d).astype(F32))


def _rms_groups(x, g, gd):
    width = x.shape[-1]
    parts = []
    if gd >= LANES:
        for j in range(width // gd):
            xs = x[:, j * gd:(j + 1) * gd]
            ms = jnp.mean(xs * xs, axis=-1, keepdims=True)
            parts.append(xs * lax.rsqrt(ms + EPS) * g)
    else:
        left = lax.broadcasted_iota(I32, (x.shape[0], LANES), 1) < gd
        for j in range(width // LANES):
            xs = x[:, j * LANES:(j + 1) * LANES]
            sq = xs * xs
            tot = jnp.sum(sq, axis=-1, keepdims=True)
            lsum = jnp.sum(jnp.where(left, sq, 0.0), axis=-1, keepdims=True)
            ms = jnp.where(left, lsum, tot - lsum) * (1.0 / gd)
            parts.append(xs * lax.rsqrt(ms + EPS) * g)
    return parts


def _group_gain(gain, gd):
    gw = max(gd, LANES)
    return jnp.tile(gain.astype(F32), gw // gd).reshape(1, gw)


def _groupnorm_kernel(x_ref, g_ref, *o_refs, gd, layouts):
    parts = _rms_groups(x_ref[...].astype(F32), g_ref[...], gd)
    pw = parts[0].shape[-1]
    for o_ref, transposed in zip(o_refs, layouts):
        for j, y in enumerate(parts):
            if transposed:
                o_ref[j * pw:(j + 1) * pw, :] = y.T.astype(o_ref.dtype)
            else:
                o_ref[:, j * pw:(j + 1) * pw] = y.astype(o_ref.dtype)


def groupnorm(x, col0, width, gain, gd, layouts=(False,)):
    m = x.shape[0]
    assert col0 % width == 0 and width % gd == 0 and (gd % LANES == 0 or 2 * gd == LANES)
    tm = _tile(m, 256)
    g = _group_gain(gain, gd)
    gw = g.shape[1]
    cb = col0 // width
    out_specs = [pl.BlockSpec((width, tm), lambda i: (0, i)) if t else pl.BlockSpec((tm, width), lambda i: (i, 0))
                 for t in layouts]
    out_shape = [jax.ShapeDtypeStruct((width, m) if t else (m, width), BF16) for t in layouts]
    outs = pl.pallas_call(
        functools.partial(_groupnorm_kernel, gd=gd, layouts=tuple(layouts)),
        grid=(m // tm,),
        in_specs=[pl.BlockSpec((tm, width), lambda i: (i, cb)),
                  pl.BlockSpec((1, gw), lambda i: (0, 0))],
        out_specs=out_specs,
        out_shape=out_shape,
        compiler_params=_cparams(("parallel",), 4 * tm * width * (8 + 2 * len(layouts))),
        name="groupnorm",
    )(x, g)
    return outs[0] if len(layouts) == 1 else outs


def _mm_kernel(*refs, nk, has_a2, has_res, norm_gd, transpose_out):
    a_ref, w_ref = refs[0], refs[1]
    pos = 2
    a2_ref = w2_ref = res_ref = g_ref = None
    if has_a2:
        a2_ref, w2_ref = refs[pos], refs[pos + 1]
        pos += 2
    if has_res:
        res_ref = refs[pos]
        pos += 1
    if norm_gd:
        g_ref = refs[pos]
        pos += 1
    o_ref = refs[pos]
    acc_ref = refs[pos + 1] if nk > 1 else None

    def epilogue(acc):
        if norm_gd:
            parts = _rms_groups(acc, g_ref[...], norm_gd)
            acc = parts[0] if len(parts) == 1 else jnp.concatenate(parts, axis=-1)
        if has_res:
            acc = acc + res_ref[...]
        acc = acc.astype(o_ref.dtype)
        o_ref[...] = acc.T if transpose_out else acc

    part = jnp.dot(a_ref[...], w_ref[...], preferred_element_type=F32)
    if has_a2:
        part = part + jnp.dot(a2_ref[...], w2_ref[...], preferred_element_type=F32)
    if nk == 1:
        epilogue(part)
    else:
        k = pl.program_id(2)

        @pl.when(k == 0)
        def _():
            acc_ref[...] = part

        @pl.when(k > 0)
        def _():
            acc_ref[...] += part

        @pl.when(k == nk - 1)
        def _():
            epilogue(acc_ref[...])


def matmul(a, w, *, out_dtype, tm=1024, tn=512, tk=None, col0=0, n=None, a2=None, residual=None,
           norm=None, transpose_out=False):
    m, kdim = a.shape
    n = w.shape[1] if n is None else n
    tm = _tile(m, tm)
    tn = _tile(n, tn)
    tk = kdim if tk is None else _tile(kdim, tk)
    nk = kdim // tk
    assert col0 % tn == 0 and (norm is None or tn % max(norm[0], LANES) == 0)
    assert a2 is None or (nk == 1 and a2.shape == a.shape and w.shape[0] == 2 * kdim)
    cb = col0 // tn
    in_specs = [pl.BlockSpec((tm, tk), lambda i, j, k: (i, k)),
                pl.BlockSpec((tk, tn), lambda i, j, k: (k, j + cb))]
    args = [a, w]
    if a2 is not None:
        in_specs += [pl.BlockSpec((tm, tk), lambda i, j, k: (i, 0)),
                     pl.BlockSpec((tk, tn), lambda i, j, k: (1, j + cb))]
        args += [a2, w]
    if residual is not None:
        in_specs.append(pl.BlockSpec((tm, tn), lambda i, j, k: (i, j)))
        args.append(residual)
    if norm is not None:
        gain = _group_gain(norm[1], norm[0])
        in_specs.append(pl.BlockSpec(gain.shape, lambda i, j, k: (0, 0)))
        args.append(gain)
    if transpose_out:
        out_spec = pl.BlockSpec((tn, tm), lambda i, j, k: (j, i))
        out_shape = jax.ShapeDtypeStruct((n, m), out_dtype)
    else:
        out_spec = pl.BlockSpec((tm, tn), lambda i, j, k: (i, j))
        out_shape = jax.ShapeDtypeStruct((m, n), out_dtype)
    osz = jnp.dtype(out_dtype).itemsize
    nops = 2 if a2 is not None else 1
    vmem = 2 * (nops * (tm * tk * 2 + tk * tn * 2) + tm * tn * osz) + tm * tn * 4 * 3
    if residual is not None:
        vmem += 2 * tm * tn * 4
    return pl.pallas_call(
        functools.partial(_mm_kernel, nk=nk, has_a2=a2 is not None, has_res=residual is not None,
                          norm_gd=None if norm is None else norm[0], transpose_out=transpose_out),
        grid=(m // tm, n // tn, nk),
        in_specs=in_specs,
        out_specs=out_spec,
        out_shape=out_shape,
        scratch_shapes=[pltpu.VMEM((tm, tn), F32)] if nk > 1 else [],
        compiler_params=_cparams(("parallel", "parallel", "arbitrary"), vmem + 4 * MIB),
        name="matmul",
    )(*args)


HALO = 16


def _ffn_gu_kernel(a_ref, ah_ref, wg_ref, wu_ref, cw_ref, cb_ref, o_ref, *, tm, seq):
    i = pl.program_id(0)
    a = a_ref[...]
    wg = wg_ref[...].astype(BF16)
    g = jnp.dot(a, wg, preferred_element_type=F32)
    u = jnp.dot(a, wu_ref[...].astype(BF16), preferred_element_type=F32)
    gh = jnp.dot(ah_ref[...], wg, preferred_element_type=F32)
    seq_start = (i * tm) % seq == 0
    gh = jnp.where(seq_start, 0.0, gh)
    rows = lax.broadcasted_iota(I32, g.shape, 0)
    hm1 = gh[HALO - 1:HALO, :]
    hm2 = gh[HALO - 2:HALO - 1, :]
    g1 = jnp.where(rows == 0, hm1, pltpu.roll(g, 1, 0))
    g2 = jnp.where(rows == 0, hm2, jnp.where(rows == 1, hm1, pltpu.roll(g, 2, 0)))
    cw = cw_ref[...]
    c = cw[0:1, :] * g2 + cw[1:2, :] * g1 + cw[2:3, :] * g + cb_ref[...]
    o_ref[...] = (c * jax.nn.sigmoid(c) * u).astype(o_ref.dtype)


def ffn_gate_up(h, wg, wu, layer, conv_w, conv_b, seq):
    m, d = h.shape
    f = wg.shape[2]
    tm = _tile(seq, 1024)
    tn = _tile(f, 256)
    hb = tm // HALO
    vmem = 2 * (tm * d * 2 + HALO * d * 2 + 2 * d * tn * 4 + tm * tn * 2) + 2 * d * tn * 2 + 8 * tm * tn * 4
    return pl.pallas_call(
        functools.partial(_ffn_gu_kernel, tm=tm, seq=seq),
        grid=(m // tm, f // tn),
        in_specs=[pl.BlockSpec((tm, d), lambda i, j: (i, 0)),
                  pl.BlockSpec((HALO, d), lambda i, j: (jnp.maximum(i * hb - 1, 0), 0)),
                  pl.BlockSpec((None, d, tn), lambda i, j: (layer, 0, j)),
                  pl.BlockSpec((None, d, tn), lambda i, j: (layer, 0, j)),
                  pl.BlockSpec((CONV_W, tn), lambda i, j: (0, j)),
                  pl.BlockSpec((1, tn), lambda i, j: (0, j))],
        out_specs=pl.BlockSpec((tm, tn), lambda i, j: (i, j)),
        out_shape=jax.ShapeDtypeStruct((m, f), BF16),
        compiler_params=_cparams(("parallel", "parallel"), vmem + 4 * MIB),
        name="ffn_gate_up",
    )(h, h, wg, wu, conv_w.astype(F32), conv_b.reshape(1, f).astype(F32))


def _t5_bucket(dist):
    n = jnp.maximum(dist, 0)
    nf = jnp.maximum(n, 1).astype(F32)
    large = MAX_EXACT + (jnp.log(nf / MAX_EXACT) / math.log(MAX_DISTANCE / MAX_EXACT)
                         * (NUM_BUCKETS - MAX_EXACT)).astype(I32)
    return jnp.where(n < MAX_EXACT, n, jnp.minimum(large, NUM_BUCKETS - 1))


def _near_kinds(tq, tk):
    return -(-(FAR_DIST - 1 + tk) // tq)


def _toeplitz_tiles(table, d0s, tq, tk, transposed):
    n = tq + tk
    j = jnp.arange(n, dtype=I32)
    if transposed:
        rel, rows, cols = jnp.where(j < tq, j, j - n), tk, tq
    else:
        rel, rows, cols = jnp.where(j < tk, -j, n - j), tq, tk
    d0 = jnp.asarray(d0s, I32)[:, None]
    vals = jnp.moveaxis(table.astype(F32)[_t5_bucket(d0 + rel[None, :])], -1, 1)
    x = jnp.tile(vals, (1, 1, rows))[:, :, :rows * (n - 1)]
    return x.reshape(len(d0s), table.shape[1], rows, n - 1)[..., :cols]


def _toeplitz_bias(table, tq, tk, transposed=False):
    kinds = _near_kinds(tq, tk) + 1
    return _toeplitz_tiles(table, [d * tq for d in range(kinds)], tq, tk, transposed)


def _sortable_key(x):
    b = lax.bitcast_convert_type(x, I32)
    return b ^ ((b >> 31) & jnp.int32(0x7FFFFFFF))


I16 = jnp.int16
HALF_MIN = -(2 ** 15)


def _indexer_kernel(qt_ref, k_ref, wi_ref, sc_ref, thr_ref, neq_ref, hi_scr, lo_scr, *, tq, tkc, topk, wscale):
    i = pl.program_id(1)
    nch = (i * tq + tq + tkc - 1) // tkc
    wi = wi_ref[...] * wscale
    sc_ref[...] = jnp.full(sc_ref.shape, -jnp.inf, F32)
    q_idx = i * tq + lax.broadcasted_iota(I32, (tkc, tq), 1)
    k_loc = lax.broadcasted_iota(I32, (tkc, tq), 0)

    def chunk(c, carry):
        off = pl.multiple_of(c * tkc, tkc)
        kc = k_ref[pl.ds(off, tkc), :]
        s = jnp.zeros((tkc, tq), F32)
        for h in range(IDX_HEADS):
            r = jnp.dot(kc, qt_ref[h * IDX_DIM:(h + 1) * IDX_DIM, :], preferred_element_type=F32)
            s = s + jnp.maximum(r, 0.0) * wi[h:h + 1, :]
        s = jnp.where(off + k_loc <= q_idx, s, -jnp.inf)
        sc_ref[pl.ds(off, tkc), :] = s
        key = _sortable_key(s)
        hi_scr[pl.ds(off, tkc), :] = (key >> 16).astype(I16)
        lo_scr[pl.ds(off, tkc), :] = ((key & 0xFFFF) + HALF_MIN).astype(I16)
        return carry

    lax.fori_loop(0, nch, chunk, 0)

    def count(ref, hit_fn):
        def body(c, acc):
            off = pl.multiple_of(c * tkc, tkc)
            ones = jnp.where(hit_fn(ref[pl.ds(off, tkc), :]), jnp.int16(1), jnp.int16(0))
            for j in range(tkc // 16):
                acc = acc + ones[j * 16:(j + 1) * 16, :]
            return acc

        acc = lax.fori_loop(0, nch, body, jnp.zeros((16, tq), I16))
        return jnp.sum(acc.astype(I32), axis=0, keepdims=True)

    def search(ref, need):
        def bit_step(it, v):
            cand = v + jnp.left_shift(jnp.int32(1), 15 - it)
            c16 = cand.astype(I16)
            cnt = count(ref, lambda x: x >= c16)
            return jnp.where(cnt >= need, cand, v)

        return lax.fori_loop(0, 16, bit_step, jnp.full((1, tq), HALF_MIN, I32))

    vh = search(hi_scr, topk)
    vh16 = vh.astype(I16)
    need_lo = topk - count(hi_scr, lambda x: x > vh16)

    def mask_lo(c, carry):
        off = pl.multiple_of(c * tkc, tkc)
        sl = pl.ds(off, tkc)
        lo_scr[sl, :] = jnp.where(hi_scr[sl, :] == vh16, lo_scr[sl, :], jnp.int16(HALF_MIN))
        return carry

    lax.fori_loop(0, nch, mask_lo, 0)
    vl = search(lo_scr, need_lo)
    vl16 = vl.astype(I16)
    thr_ref[...] = vh * 65536 + (vl - HALF_MIN)
    neq_ref[...] = need_lo - count(lo_scr, lambda x: x > vl16)


def dsa_indexer(qit, ki, wit, bsz, seq, topk):
    n = bsz * seq
    tq = _tile(seq, 256)
    tkc = _tile(seq, 512)
    nq = seq // tq
    vmem = 2 * (IDX_HEADS * IDX_DIM * tq * 2 + seq * IDX_DIM * 2 + IDX_HEADS * tq * 4 + seq * tq * 4) \
        + seq * tq * 4 + 8 * tkc * tq * 4
    return pl.pallas_call(
        functools.partial(_indexer_kernel, tq=tq, tkc=tkc, topk=topk,
                          wscale=IDX_HEADS ** -0.5 * IDX_DIM ** -0.5),
        grid=(bsz, nq),
        in_specs=[pl.BlockSpec((IDX_HEADS * IDX_DIM, tq), lambda b, i: (0, b * nq + i)),
                  pl.BlockSpec((seq, IDX_DIM), lambda b, i: (b, 0)),
                  pl.BlockSpec((IDX_HEADS, tq), lambda b, i: (0, b * nq + i))],
        out_specs=[pl.BlockSpec((seq, tq), lambda b, i: (b, i)),
                   pl.BlockSpec((1, tq), lambda b, i: (0, b * nq + i)),
                   pl.BlockSpec((1, tq), lambda b, i: (0, b * nq + i))],
        out_shape=[jax.ShapeDtypeStruct((n, seq), F32),
                   jax.ShapeDtypeStruct((1, n), I32),
                   jax.ShapeDtypeStruct((1, n), I32)],
        scratch_shapes=[pltpu.VMEM((seq, tq), I16), pltpu.VMEM((seq, tq), I16)],
        compiler_params=_cparams(("parallel", "parallel"), vmem + 4 * MIB),
        name="dsa_indexer",
    )(qit, ki, wit)


def _dsa_attn_kernel(qi_tab, kb_tab, qt_ref, c_ref, ct_ref, sc_ref, thr_ref, neq_ref, low_ref, bias_ref, wuvt_ref,
                     o_ref, m_scr, l_scr, acc_scr, madd_scr, ties_scr, *, tq, tk, scale):
    s_id = pl.program_id(1)
    i = qi_tab[s_id]
    kb = kb_tab[s_id]
    nh = A_HEADS
    rank = A_KV_RANK

    @pl.when(kb == 0)
    def _():
        m_scr[...] = jnp.full(m_scr.shape, M_INIT, F32)
        l_scr[...] = jnp.zeros(l_scr.shape, F32)
        acc_scr[...] = jnp.zeros(acc_scr.shape, F32)
        ties_scr[...] = jnp.zeros(ties_scr.shape, F32)

    key = _sortable_key(sc_ref[...])
    thr = thr_ref[...]
    k_idx = kb * tk + lax.broadcasted_iota(I32, (tk, tq), 0)
    q_idx = i * tq + lax.broadcasted_iota(I32, (tk, tq), 1)
    causal = k_idx <= q_idx
    tie = jnp.where(causal, jnp.where(key == thr, 1.0, 0.0), 0.0)
    earlier = jnp.dot(low_ref[...], tie.astype(BF16), preferred_element_type=F32) + ties_scr[...]
    tie_add = jnp.where(earlier < neq_ref[...].astype(F32), 0.0, -jnp.inf)
    member = jnp.where(key > thr, 0.0, jnp.where(key == thr, tie_add, -jnp.inf))
    madd_scr[...] = jnp.where(causal, member, -jnp.inf)
    ties_scr[...] += jnp.sum(tie, axis=0, keepdims=True)

    for h in range(nh):
        qh = qt_ref[h * rank:(h + 1) * rank, :]
        s = jnp.dot(c_ref[...], qh, preferred_element_type=F32) * scale + bias_ref[h] + madd_scr[...]
        m_old = m_scr[h]
        m_new = jnp.maximum(m_old, jnp.max(s, axis=0, keepdims=True))
        alpha = jnp.exp2(m_old - m_new)
        p = jnp.exp2(s - m_new)
        l_scr[h] = alpha * l_scr[h] + jnp.sum(p, axis=0, keepdims=True)
        acc_scr[h] = alpha * acc_scr[h] + jnp.dot(ct_ref[...], p.astype(BF16),
                                                  preferred_element_type=F32)
        m_scr[h] = m_new

    @pl.when(kb == (i * tq + tq - 1) // tk)
    def _():
        for h in range(nh):
            o_lat = (acc_scr[h] / l_scr[h]).astype(BF16)
            out_t = jnp.dot(wuvt_ref[h], o_lat, preferred_element_type=F32)
            o_ref[:, h * A_V_DIM:(h + 1) * A_V_DIM] = out_t.T.astype(o_ref.dtype)


def dsa_attention(qt, c, ct, scores_t, thr, neq, bias_a, w_uvt, bsz, seq):
    n = bsz * seq
    tq = _tile(seq, 256)
    tk = tq
    nq, nkb = seq // tq, seq // tk
    bias = _toeplitz_bias(bias_a * LOG2E, tq, tk, transposed=True)
    kinds = bias.shape[0]
    rank = A_KV_RANK
    low = (jnp.arange(tk)[None, :] < jnp.arange(tk)[:, None]).astype(BF16)
    pairs = [(i, kb) for i in range(nq) for kb in range((i * tq + tq - 1) // tk + 1)]
    qi_tab = jnp.asarray([p[0] for p in pairs], I32)
    kb_tab = jnp.asarray([p[1] for p in pairs], I32)

    def kind(i, kb):
        return jnp.minimum(i - kb * (tk // tq), kinds - 1)

    vmem = 2 * (A_HEADS * rank * tq * 2 + 2 * tk * rank * 2 + tk * tq * 4 + A_HEADS * tk * tq * 4
                + A_HEADS * rank * A_V_DIM * 2 + tq * A_HEADS * A_V_DIM * 2) \
        + A_HEADS * rank * tq * 4 + 10 * tk * tq * 4
    grid_spec = pltpu.PrefetchScalarGridSpec(
        num_scalar_prefetch=2,
        grid=(bsz, len(pairs)),
        in_specs=[pl.BlockSpec((A_HEADS * rank, tq), lambda b, s, qi, kb: (0, b * nq + qi[s])),
                  pl.BlockSpec((tk, rank), lambda b, s, qi, kb: (b * nkb + kb[s], 0)),
                  pl.BlockSpec((rank, tk), lambda b, s, qi, kb: (0, b * nkb + kb[s])),
                  pl.BlockSpec((tk, tq), lambda b, s, qi, kb: (b * nkb + kb[s], qi[s])),
                  pl.BlockSpec((1, tq), lambda b, s, qi, kb: (0, b * nq + qi[s])),
                  pl.BlockSpec((1, tq), lambda b, s, qi, kb: (0, b * nq + qi[s])),
                  pl.BlockSpec((tk, tk), lambda b, s, qi, kb: (0, 0)),
                  pl.BlockSpec((None, A_HEADS, tk, tq), lambda b, s, qi, kb: (kind(qi[s], kb[s]), 0, 0, 0)),
                  pl.BlockSpec((A_HEADS, A_V_DIM, rank), lambda b, s, qi, kb: (0, 0, 0))],
        out_specs=pl.BlockSpec((tq, A_HEADS * A_V_DIM), lambda b, s, qi, kb: (b * nq + qi[s], 0)),
        scratch_shapes=[pltpu.VMEM((A_HEADS, 1, tq), F32),
                        pltpu.VMEM((A_HEADS, 1, tq), F32),
                        pltpu.VMEM((A_HEADS, rank, tq), F32),
                        pltpu.VMEM((tk, tq), F32),
                        pltpu.VMEM((1, tq), F32)])
    return pl.pallas_call(
        functools.partial(_dsa_attn_kernel, tq=tq, tk=tk, scale=rank ** -0.5 * LOG2E),
        grid_spec=grid_spec,
        out_shape=jax.ShapeDtypeStruct((n, A_HEADS * A_V_DIM), BF16),
        compiler_params=_cparams(("parallel", "arbitrary"), vmem + 4 * MIB),
        name="dsa_attention",
    )(qi_tab, kb_tab, qt, c, ct, scores_t, thr, neq, low, bias, w_uvt)


SB_HEADS_PER_STEP = 4


def _sb_kernel(q_ref, k_ref, v_ref, u_ref, o_ref, carry_scr, acc_scr, *, tq, scale):
    i = pl.program_id(2)
    tri = u_ref[...]
    row = lax.broadcasted_iota(I32, (tq, tq), 0)
    col = lax.broadcasted_iota(I32, (tq, tq), 1)
    strict = col < row
    has_prev = i > 0

    def block(hh, q, kb, diag):
        off = pl.multiple_of(kb * tq, tq)
        hs = slice(hh * B_DIM, (hh + 1) * B_DIM)
        k = k_ref[pl.ds(off, tq), hs]
        z = lax.dot_general(q, k, (((1,), (1,)), ((), ())), preferred_element_type=F32) * scale
        log1m = -(jnp.maximum(z, 0.0) + jnp.log(1.0 + jnp.exp(-jnp.abs(z))))
        if diag:
            log1m = jnp.where(strict, log1m, 0.0)
        hi = log1m.astype(BF16)
        lo = (log1m - hi.astype(F32)).astype(BF16)
        between = (jnp.dot(hi, tri, preferred_element_type=F32)
                   + jnp.dot(lo, tri, preferred_element_type=F32))
        logw = z + log1m + between
        return logw, jnp.sum(log1m, axis=-1, keepdims=True), v_ref[pl.ds(off, tq), hs]

    cmax = []
    for hh in range(SB_HEADS_PER_STEP):
        q = q_ref[:, hh * B_DIM:(hh + 1) * B_DIM]
        lw_d, rs_d, v_d = block(hh, q, i, True)
        lw_p, rs_p, v_p = block(hh, q, jnp.maximum(i - 1, 0), False)
        w_d = jnp.where(strict, jnp.exp(lw_d), 0.0)
        w_p = jnp.where(has_prev, jnp.exp(lw_p + rs_d), 0.0)
        acc_scr[hh] = (jnp.dot(w_d.astype(BF16), v_d, preferred_element_type=F32)
                       + jnp.dot(w_p.astype(BF16), v_p, preferred_element_type=F32))
        carry = rs_d + jnp.where(has_prev, rs_p, 0.0)
        carry_scr[hh] = carry
        cmax.append(jnp.max(carry))

    for hh in range(SB_HEADS_PER_STEP):
        q = q_ref[:, hh * B_DIM:(hh + 1) * B_DIM]

        def cond(state):
            kb, cm = state
            return jnp.logical_and(kb >= 0, cm > SB_SKIP)

        def body(state, hh=hh, q=q):
            kb, _ = state
            logw, rs, v = block(hh, q, kb, False)
            carry = carry_scr[hh]
            w = jnp.exp(logw + carry)
            acc_scr[hh] += jnp.dot(w.astype(BF16), v, preferred_element_type=F32)
            carry = carry + rs
            carry_scr[hh] = carry
            return kb - 1, jnp.max(carry)

        lax.while_loop(cond, body, (i - 2, cmax[hh]))
        o_ref[:, hh * B_DIM:(hh + 1) * B_DIM] = acc_scr[hh].astype(o_ref.dtype)


def stickbreak_attention(qkv, bsz, seq):
    n = bsz * seq
    tq = _tile(seq, 256)
    nq = seq // tq
    hps = SB_HEADS_PER_STEP
    width = hps * B_DIM
    groups = B_HEADS // hps
    tri = (jnp.arange(tq)[:, None] > jnp.arange(tq)[None, :]).astype(BF16)
    vmem = 2 * (2 * seq * width * 2 + 2 * tq * width * 2 + tq * tq * 2) + 24 * tq * tq * 4
    return pl.pallas_call(
        functools.partial(_sb_kernel, tq=tq, scale=B_DIM ** -0.5),
        grid=(bsz, groups, nq),
        in_specs=[pl.BlockSpec((tq, width), lambda b, g, i: (b * nq + i, g)),
                  pl.BlockSpec((seq, width), lambda b, g, i: (b, groups + g)),
                  pl.BlockSpec((seq, width), lambda b, g, i: (b, 2 * groups + g)),
                  pl.BlockSpec((tq, tq), lambda b, g, i: (0, 0))],
        out_specs=pl.BlockSpec((tq, width), lambda b, g, i: (b * nq + i, g)),
        out_shape=jax.ShapeDtypeStruct((n, B_HEADS * B_DIM), BF16),
        scratch_shapes=[pltpu.VMEM((hps, tq, 1), F32), pltpu.VMEM((hps, tq, B_DIM), F32)],
        compiler_params=_cparams(("parallel", "parallel", "parallel"), vmem + 4 * MIB),
        name="stickbreak_attention",
    )(qkv, qkv, qkv, tri)


DIFF_HEADS_PER_STEP = 2


def _diff_kernel(qt_ref, k_ref, vt_ref, bias_ref, lam_ref, subg_ref, o_ref, m_scr, l_scr, acc_scr,
                 *, tq, kinds, scale, lambda_init):
    i = pl.program_id(2)
    dq = D_QK_DIM
    dv = D_V_DIM
    lv = lam_ref[...]
    lam = (jnp.exp(jnp.sum(lv[0:1] * lv[1:2], axis=-1, keepdims=True))
           - jnp.exp(jnp.sum(lv[2:3] * lv[3:4], axis=-1, keepdims=True))) + lambda_init
    key_i = lax.broadcasted_iota(I32, (tq, tq), 0)
    qry_i = lax.broadcasted_iota(I32, (tq, tq), 1)
    causal = key_i <= qry_i

    m_scr[...] = jnp.full(m_scr.shape, M_INIT, F32)
    l_scr[...] = jnp.zeros(l_scr.shape, F32)
    acc_scr[...] = jnp.zeros(acc_scr.shape, F32)

    def step(kb, nblk, diag):
        tk = nblk * tq
        off = pl.multiple_of(kb * tq, tq)
        for hh in range(DIFF_HEADS_PER_STEP):
            bias = jnp.concatenate([bias_ref[jnp.minimum(i - kb - j, kinds - 1), hh] for j in range(nblk)],
                                   axis=0)
            vt = vt_ref[hh * dv:(hh + 1) * dv, pl.ds(off, tk)]
            for c in range(2):
                col = (2 * hh + c) * dq
                s = jnp.dot(k_ref[pl.ds(off, tk), col:col + dq], qt_ref[col:col + dq, :],
                            preferred_element_type=F32) * scale + bias
                if diag:
                    s = jnp.where(causal, s, -jnp.inf)
                u = 2 * hh + c
                m_old = m_scr[u]
                m_new = jnp.maximum(m_old, jnp.max(s, axis=0, keepdims=True))
                alpha = jnp.exp2(m_old - m_new)
                p = jnp.exp2(s - m_new)
                l_scr[u] = alpha * l_scr[u] + jnp.sum(p, axis=0, keepdims=True)
                acc_scr[u] = alpha * acc_scr[u] + jnp.dot(vt, p.astype(BF16), preferred_element_type=F32)
                m_scr[u] = m_new

    def far_oct(j, carry):
        step(8 * j, 2, False)
        step(8 * j + 2, 2, False)
        step(8 * j + 4, 2, False)
        step(8 * j + 6, 2, False)
        return carry

    lax.fori_loop(0, i // 8, far_oct, 0)

    @pl.when(i % 8 >= 4)
    def _():
        step((i // 8) * 8, 2, False)
        step((i // 8) * 8 + 2, 2, False)

    @pl.when(i % 4 >= 2)
    def _():
        step((i // 4) * 4, 2, False)

    @pl.when(i % 2 == 1)
    def _():
        step(i - 1, 1, False)

    step(i, 1, True)
    for hh in range(DIFF_HEADS_PER_STEP):
        out = acc_scr[2 * hh] / l_scr[2 * hh] - lam * (acc_scr[2 * hh + 1] / l_scr[2 * hh + 1])
        ms = jnp.mean(out * out, axis=0, keepdims=True)
        out = out * lax.rsqrt(ms + EPS) * subg_ref[...] * (1.0 - lambda_init)
        o_ref[:, hh * dv:(hh + 1) * dv] = out.T.astype(o_ref.dtype)


def diff_attention(dqt, dk, dvt, bias_d, lam_vecs, sub_g, bsz, seq, lambda_init):
    n = bsz * seq
    tq = _tile(seq, 256)
    nq = seq // tq
    hps = DIFF_HEADS_PER_STEP
    bias = _toeplitz_bias(bias_d * LOG2E, tq, tq, transposed=True)
    kinds = bias.shape[0]
    width = hps * 2 * D_QK_DIM
    vw = hps * D_V_DIM
    vmem = 2 * (tq * width * 2 + seq * width * 2 + seq * vw * 2 + kinds * hps * tq * tq * 4
                + tq * vw * 2) + 2 * hps * tq * D_V_DIM * 4 + 16 * tq * tq * 4
    return pl.pallas_call(
        functools.partial(_diff_kernel, tq=tq, kinds=kinds, scale=D_QK_DIM ** -0.5 * LOG2E, lambda_init=lambda_init),
        grid=(bsz, D_HEADS // hps, nq),
        in_specs=[pl.BlockSpec((width, tq), lambda b, g, i: (g, b * nq + i)),
                  pl.BlockSpec((seq, width), lambda b, g, i: (b, g)),
                  pl.BlockSpec((vw, seq), lambda b, g, i: (g, b)),
                  pl.BlockSpec((kinds, hps, tq, tq), lambda b, g, i: (0, g, 0, 0)),
                  pl.BlockSpec((4, D_QK_DIM), lambda b, g, i: (0, 0)),
                  pl.BlockSpec((D_V_DIM, 1), lambda b, g, i: (0, 0))],
        out_specs=pl.BlockSpec((tq, vw), lambda b, g, i: (b * nq + i, g)),
        out_shape=jax.ShapeDtypeStruct((n, D_HEADS * D_V_DIM), BF16),
        scratch_shapes=[pltpu.VMEM((2 * hps, 1, tq), F32), pltpu.VMEM((2 * hps, 1, tq), F32),
                        pltpu.VMEM((2 * hps, D_V_DIM, tq), F32)],
        compiler_params=_cparams(("parallel", "parallel", "parallel"), vmem + 4 * MIB),
        name="diff_attention",
    )(dqt, dk, dvt, bias, lam_vecs, sub_g.reshape(D_V_DIM, 1).astype(F32))


def _swa_kernel(q_ref, kp_ref, ko_ref, vp_ref, vo_ref, bias_ref, sink_ref, o_ref, *, blk, scale):
    i = pl.program_id(1)
    left = lax.broadcasted_iota(I32, (blk, LANES), 1) < C_DIM
    qa = lax.broadcasted_iota(I32, (blk, 2 * blk), 0)
    sa = lax.broadcasted_iota(I32, (blk, 2 * blk), 1)
    rel = qa + blk - sa
    lo = jnp.where(i > 0, 0, blk)
    madd = jnp.where(rel >= 0, jnp.where(rel < WINDOW, jnp.where(sa >= lo, 0.0, -jnp.inf), -jnp.inf), -jnp.inf)
    grp = C_HEADS // C_KV_HEADS
    for g in range(C_KV_HEADS):
        ksl = slice(g * LANES, (g + 1) * LANES)
        kk = jnp.concatenate([kp_ref[:, ksl], ko_ref[:, ksl]], axis=0)
        vv = jnp.concatenate([vp_ref[:, ksl], vo_ref[:, ksl]], axis=0)
        for pr in range(grp // 2):
            cb = g * (grp // 2) + pr
            qp = q_ref[:, cb * LANES:(cb + 1) * LANES]
            outs = []
            for half in range(2):
                h = 2 * cb + half
                qh = jnp.where(left if half == 0 else jnp.logical_not(left), qp, jnp.zeros_like(qp))
                lg = lax.dot_general(qh, kk, (((1,), (1,)), ((), ())), preferred_element_type=F32) * scale
                lg = lg + bias_ref[h] + madd
                sink = sink_ref[h]
                m = jnp.maximum(jnp.max(lg, axis=-1, keepdims=True), sink)
                e = jnp.exp(lg - m)
                p = e / (jnp.sum(e, axis=-1, keepdims=True) + jnp.exp(sink - m))
                outs.append(jnp.dot(p.astype(BF16), vv, preferred_element_type=F32))
            o_ref[:, cb * LANES:(cb + 1) * LANES] = jnp.where(left, outs[0], outs[1]).astype(o_ref.dtype)


def swa_attention(cq, ck2, cv2, bias_c, sinks, bsz, seq):
    n = bsz * seq
    blk = WINDOW
    nb = seq // blk
    bias = _toeplitz_tiles(bias_c, [blk], blk, 2 * blk, False)[0]
    qw = C_HEADS * C_DIM
    kw = C_KV_HEADS * LANES
    own = lambda b, i: (b * nb + i, 0)
    prev = lambda b, i: (b * nb + jnp.maximum(i - 1, 0), 0)
    vmem = 2 * (2 * blk * qw * 2 + 4 * blk * kw * 2 + C_HEADS * blk * 2 * blk * 4) + 16 * blk * 2 * blk * 4
    return pl.pallas_call(
        functools.partial(_swa_kernel, blk=blk, scale=C_DIM ** -0.5),
        grid=(bsz, nb),
        in_specs=[pl.BlockSpec((blk, qw), own),
                  pl.BlockSpec((blk, kw), prev), pl.BlockSpec((blk, kw), own),
                  pl.BlockSpec((blk, kw), prev), pl.BlockSpec((blk, kw), own),
                  pl.BlockSpec((C_HEADS, blk, 2 * blk), lambda b, i: (0, 0, 0)),
                  pl.BlockSpec(memory_space=pltpu.SMEM)],
        out_specs=pl.BlockSpec((blk, qw), own),
        out_shape=jax.ShapeDtypeStruct((n, qw), BF16),
        compiler_params=_cparams(("parallel", "parallel"), vmem + 4 * MIB),
        name="swa_attention",
    )(cq, ck2, ck2, cv2, cv2, bias, sinks.astype(F32))


def _rms_rows(x, g):
    return x * lax.rsqrt(jnp.mean(x * x, axis=-1, keepdims=True) + EPS) * g


def _xblock_kernel(x_ref, g1_ref, wq_ref, k_ref, v_ref, qg_ref, kg_ref, wo_ref, g2_ref, o_ref, h_ref, *, scale):
    x = x_ref[...]
    h = _rms_rows(x, g1_ref[...]).astype(BF16)
    q_all = jnp.dot(h, wq_ref[...], preferred_element_type=F32)
    outs = []
    for hd in range(X_HEADS):
        sl = slice(hd * X_DIM, (hd + 1) * X_DIM)
        qn = _rms_rows(q_all[:, sl], qg_ref[...]).astype(BF16)
        kn = _rms_rows(k_ref[:, sl], kg_ref[...]).astype(BF16)
        lg = lax.dot_general(qn, kn, (((1,), (1,)), ((), ())), preferred_element_type=F32) * scale
        m = jnp.max(lg, axis=-1, keepdims=True)
        e = jnp.exp(lg - m)
        p = (e / jnp.sum(e, axis=-1, keepdims=True)).astype(BF16)
        outs.append(jnp.dot(p, v_ref[:, sl], preferred_element_type=F32).astype(BF16))
    o = jnp.concatenate(outs, axis=-1)
    y = x + jnp.dot(o, wo_ref[...], preferred_element_type=F32)
    o_ref[...] = y
    h_ref[...] = _rms_rows(y, g2_ref[...]).astype(h_ref.dtype)


def cross_block(xf, g1, wq, kx, vx, q_g, k_g, wo, g2, bsz, seq, mem_len):
    n, d = xf.shape
    tm = _tile(seq, 256)
    nq = seq // tm
    w = X_HEADS * X_DIM
    row = lambda b, i: (b * nq + i, 0)
    const = lambda b, i: (0, 0)
    vmem = 2 * (tm * d * 10 + 2 * d * w * 2 + mem_len * w * 6) + 6 * tm * d * 4
    return pl.pallas_call(
        functools.partial(_xblock_kernel, scale=X_DIM ** -0.5),
        grid=(bsz, nq),
        in_specs=[pl.BlockSpec((tm, d), row),
                  pl.BlockSpec((1, d), const),
                  pl.BlockSpec((d, w), const),
                  pl.BlockSpec((mem_len, w), lambda b, i: (b, 0)),
                  pl.BlockSpec((mem_len, w), lambda b, i: (b, 0)),
                  pl.BlockSpec((1, X_DIM), const),
                  pl.BlockSpec((1, X_DIM), const),
                  pl.BlockSpec((w, d), const),
                  pl.BlockSpec((1, d), const)],
        out_specs=[pl.BlockSpec((tm, d), row), pl.BlockSpec((tm, d), row)],
        out_shape=[jax.ShapeDtypeStruct((n, d), F32), jax.ShapeDtypeStruct((n, d), BF16)],
        compiler_params=_cparams(("parallel", "parallel"), vmem + 4 * MIB),
        name="cross_block",
    )(xf, g1.reshape(1, d).astype(F32), wq, kx, vx, q_g.reshape(1, X_DIM).astype(F32),
      k_g.reshape(1, X_DIM).astype(F32), wo, g2.reshape(1, d).astype(F32))


def _even_mixer(xf, h, bsz, seq, w_in, q_lat_g, w_qb, q_g, kv_g, w_uv, w_qi, kidx_g, w_out, rel_bias):
    a_cols = A_Q_RANK + A_KV_RANK + IDX_DIM
    a_width = -(-(a_cols + IDX_HEADS) // 256) * 256
    w_a = w_in[:, :a_width].astype(BF16)
    w_b = w_in[:, a_cols + IDX_HEADS:].astype(BF16)
    pa = matmul(h, w_a, out_dtype=F32, tn=256)
    qkv = matmul(h, w_b, out_dtype=BF16, tn=1024)

    cq = groupnorm(pa, 0, A_Q_RANK, q_lat_g, A_Q_RANK)
    c, ct = groupnorm(pa, A_Q_RANK, A_KV_RANK, kv_g, A_KV_RANK, layouts=(False, True))
    ki = groupnorm(pa, A_Q_RANK + A_KV_RANK, IDX_DIM, kidx_g, IDX_DIM)
    qt = matmul(cq, w_qb.reshape(A_Q_RANK, A_HEADS * A_KV_RANK).astype(BF16), out_dtype=BF16,
                tm=512, tn=4 * A_KV_RANK, norm=(A_KV_RANK, q_g), transpose_out=True)
    qit = matmul(cq, w_qi.reshape(A_Q_RANK, IDX_HEADS * IDX_DIM).astype(BF16), out_dtype=BF16,
                 tn=1024, transpose_out=True)
    wit = pa[:, a_cols:a_cols + IDX_HEADS].T
    topk = min(TOPK_MAX, seq // 4)
    scores_t, thr, neq = dsa_indexer(qit, ki, wit, bsz, seq, topk)
    o_a = dsa_attention(qt, c, ct, scores_t, thr, neq, rel_bias[:, BIAS_A_OFF:BIAS_A_OFF + A_HEADS],
                        jnp.swapaxes(w_uv, 1, 2).astype(BF16), bsz, seq)
    o_b = stickbreak_attention(qkv, bsz, seq)
    return matmul(o_a, w_out.astype(BF16), a2=o_b, out_dtype=F32, tn=1024, residual=xf)


def _odd_mixer(xf, h, bsz, seq, w_in, c_q_g, c_k_g, sinks, d_q_g, d_k_g, lam_q1, lam_k1, lam_q2, lam_k2,
               sub_g, w_out, rel_bias, lambda_init):
    n = xf.shape[0]
    cw = C_HEADS * C_DIM
    ckw = C_KV_HEADS * C_DIM
    dw = D_HEADS * 2 * D_QK_DIM
    c_end = cw + 2 * ckw
    w_bf = w_in.astype(BF16)
    proj = functools.partial(matmul, h, w_bf, out_dtype=BF16)
    cq = proj(n=cw, norm=(C_DIM, c_q_g), tn=1024)
    ck = proj(col0=cw, n=ckw, norm=(C_DIM, c_k_g), tn=256)
    cv = proj(col0=cw + ckw, n=ckw, tn=256)
    dqt = proj(col0=c_end, n=dw, norm=(D_QK_DIM, d_q_g), tn=512, transpose_out=True)
    dk = proj(col0=c_end + dw, n=dw, norm=(D_QK_DIM, d_k_g), tn=512)
    dvt = proj(col0=c_end + 2 * dw, n=D_HEADS * D_V_DIM, tn=512, transpose_out=True)

    dup = lambda a: jnp.concatenate([a.reshape(n, C_KV_HEADS, C_DIM)] * 2, axis=-1).reshape(n, C_KV_HEADS * LANES)
    o_c = swa_attention(cq, dup(ck), dup(cv), rel_bias[:, BIAS_C_OFF:BIAS_C_OFF + C_HEADS], sinks, bsz, seq)

    lam_vecs = jnp.stack([lam_q1, lam_k1, lam_q2, lam_k2]).astype(F32)
    o_d = diff_attention(dqt, dk, dvt, rel_bias[:, BIAS_D_OFF:BIAS_D_OFF + D_HEADS], lam_vecs, sub_g,
                         bsz, seq, lambda_init)
    return matmul(o_c, w_out.astype(BF16), a2=o_d, out_dtype=F32, tn=1024, residual=xf)


def _cross_block(xf, memn, norm_g, wq, wk, wv, q_g, k_g, wo, ffn_norm_g, bsz, seq, mem_len):
    d = xf.shape[1]
    w = X_HEADS * X_DIM
    kx = matmul(memn, wk.reshape(d, w).astype(BF16), out_dtype=F32, tn=512)
    vx = matmul(memn, wv.reshape(d, w).astype(BF16), out_dtype=BF16, tn=512)
    return cross_block(xf, norm_g, wq.reshape(d, w).astype(BF16), kx, vx, q_g, k_g, wo.astype(BF16),
                       ffn_norm_g, bsz, seq, mem_len)


def _ffn_block(xf, h, w_gate, w_up, layer, conv_w, conv_b, w_down, seq):
    act = ffn_gate_up(h, w_gate, w_up, layer, conv_w, conv_b, seq)
    return matmul(act, w_down.astype(BF16), out_dtype=F32, tm=512, tn=512, residual=xf)


def kernel(x, mem, rel_bias, mem_norm_g, mix_norm_g, xattn_norm_g, ffn_norm_g, ev_w_in, ev_q_lat_g, ev_w_qb, ev_q_g, ev_kv_g, ev_w_uv, ev_w_qi, ev_kidx_g, ev_w_out, od_w_in, od_c_q_g, od_c_k_g, od_sinks, od_d_q_g, od_d_k_g, od_lam_q1, od_lam_k1, od_lam_q2, od_lam_k2, od_sub_g, od_w_out, x_wq, x_wk, x_wv, x_q_g, x_k_g, x_wo, f_w_gate, f_w_up, f_conv_w, f_conv_b, f_w_down):
    bsz, seq, d = x.shape
    mem_len = mem.shape[1]
    depth = mix_norm_g.shape[0]
    xf = x.reshape(bsz * seq, d)
    memn = rmsnorm_rows(mem.reshape(bsz * mem_len, d), mem_norm_g)
    for l in range(depth):
        h = rmsnorm_rows(xf, mix_norm_g[l])
        if l % 2 == 0:
            e = l // 2
            xf = _even_mixer(xf, h, bsz, seq, ev_w_in[e], ev_q_lat_g[e], ev_w_qb[e], ev_q_g[e], ev_kv_g[e],
                             ev_w_uv[e], ev_w_qi[e], ev_kidx_g[e], ev_w_out[e], rel_bias)
        else:
            o = l // 2
            lambda_init = 0.8 - 0.6 * math.exp(-0.3 * l)
            xf = _odd_mixer(xf, h, bsz, seq, od_w_in[o], od_c_q_g[o], od_c_k_g[o], od_sinks[o], od_d_q_g[o],
                            od_d_k_g[o], od_lam_q1[o], od_lam_k1[o], od_lam_q2[o], od_lam_k2[o], od_sub_g[o],
                            od_w_out[o], rel_bias, lambda_init)
        xf, h = _cross_block(xf, memn, xattn_norm_g[l], x_wq[l], x_wk[l], x_wv[l], x_q_g[l], x_k_g[l], x_wo[l],
                             ffn_norm_g[l], bsz, seq, mem_len)
        xf = _ffn_block(xf, h, f_w_gate, f_w_up, l, f_conv_w[l], f_conv_b[l], f_w_down[l], seq)
    return xf.reshape(bsz, seq, d)
```
